```python
import math
import jax
import jax.numpy as jnp
from jax import lax
import numpy as np

D_MODEL = 1024
BATCH = 8
SEQ = 2048
DEPTH = 4
DEC_BATCH = 32
DEC_SEQ = 1
PAST_LEN = 8192
PAGE_SIZE = 128

N_MIXERS = 4
RMS_EPS = 1e-6
F32 = jnp.float32

RW_HEAD = 64
RW_HEADS = D_MODEL // RW_HEAD
RW_LORA_W = 64
RW_LORA_A = 64
RW_GN_EPS = 64e-5
RW_N_MIX = 6

MB_INNER = 2 * D_MODEL
MB_HEADDIM = 64
MB_HEADS = MB_INNER // MB_HEADDIM
MB_GROUPS = 4
MB_STATE = 128
MB_CONV = 4
MB_CHUNK = 128
MB_CONV_DIM = MB_INNER + 2 * MB_GROUPS * MB_STATE
MB_NORM_EPS = 1e-5

DA_HEADS = 8
DA_HEAD = 64
DA_VDIM = 2 * DA_HEAD
DA_WIDTH = DA_HEADS * DA_VDIM
DA_SCALE = DA_HEAD ** -0.5
DA_SUBLN_EPS = 1e-5
ROPE_THETA = 500000.0
ROPE_DIM = DA_HEAD // 4
Q_BLOCK = 128

CF_WIDTH = D_MODEL
CF_KERNEL = 31
CF_LN_EPS = 1e-5

kernel_name = 'hybrid_rwkv7_mamba2_diffattn_conformer_step'


def rms_norm(x, w, eps=RMS_EPS):
    xf = x.astype(F32)
    y = xf * lax.rsqrt(jnp.mean(xf * xf, axis=-1, keepdims=True) + eps)
    return (y * w.astype(F32)).astype(x.dtype)


def layer_norm(x, w, b, eps):
    xf = x.astype(F32)
    mu = jnp.mean(xf, axis=-1, keepdims=True)
    var = jnp.mean(jnp.square(xf - mu), axis=-1, keepdims=True)
    return ((xf - mu) * lax.rsqrt(var + eps) * w.astype(F32) + b.astype(F32)).astype(x.dtype)


def causal_depthwise_conv(x_hist, w, b):
    c = x_hist.shape[-1]
    y = lax.conv_general_dilated(x_hist, w[:, None, :].astype(x_hist.dtype), window_strides=(1,),
                                 padding='VALID', dimension_numbers=('NWC', 'WIO', 'NWC'),
                                 feature_group_count=c)
    return y + b.astype(y.dtype)


def rwkv7_mixer(x, shift_state, wkv_state, mu, w_r, w_k, w_v, w_g, w0, w_w1, w_w2, a0, w_a1, w_a2,
                k_k, k_a, r_k, gn_w, gn_b, w_o):
    b, t, d = x.shape
    h, n = RW_HEADS, RW_HEAD
    x_prev = jnp.concatenate([shift_state[:, None, :].astype(x.dtype), x[:, :-1]], axis=1)
    xx = x_prev - x
    xm = x[None] + xx[None] * mu[:, None, None, :]
    r = (xm[0] @ w_r).reshape(b, t, h, n)
    k = xm[2] @ w_k
    v = (xm[3] @ w_v).reshape(b, t, h, n)
    g = jax.nn.silu(xm[5] @ w_g)
    w_log = -jax.nn.softplus(-(w0 + jnp.tanh(xm[1] @ w_w1) @ w_w2)) - 0.5
    a = jax.nn.sigmoid(a0 + (xm[4] @ w_a1) @ w_a2)
    kk = (k * k_k).reshape(b, t, h, n).astype(F32)
    kk = kk * lax.rsqrt(jnp.maximum(jnp.sum(kk * kk, axis=-1, keepdims=True), 1e-24))
    k = (k * (1.0 + (a - 1.0) * k_a)).reshape(b, t, h, n)
    a = a.reshape(b, t, h, n).astype(F32)
    decay = jnp.exp(-jnp.exp(w_log.astype(F32))).reshape(b, t, h, n)
    seq = tuple(z.astype(F32).swapaxes(0, 1) for z in (r, k, v, kk, kk * a, decay))

    def step(s, inp):
        r_t, k_t, v_t, kk_t, b_t, d_t = inp
        sa = jnp.einsum('bhvk,bhk->bhv', s, kk_t)
        s = (s * d_t[:, :, None, :] - sa[..., None] * b_t[:, :, None, :]
             + v_t[..., None] * k_t[:, :, None, :])
        return s, jnp.einsum('bhvk,bhk->bhv', s, r_t)

    s_final, o = lax.scan(step, wkv_state.astype(F32), seq)
    o = o.swapaxes(0, 1)
    o = layer_norm(o, gn_w.reshape(h, n), gn_b.reshape(h, n), RW_GN_EPS)
    rf, kf, vf = r.astype(F32), k.astype(F32), v.astype(F32)
    o = o + jnp.sum(rf * kf * r_k.astype(F32), axis=-1, keepdims=True) * vf
    y = (o.reshape(b, t, d) * g) @ w_o
    return y.astype(x.dtype), x[:, -1], s_final


def ssd_chunked(xdt, da, bm, cm, init_state, chunk):
    b, t, h, p = xdt.shape
    g, n = bm.shape[2], bm.shape[3]
    j = h // g
    nc = t // chunk
    x_c = xdt.astype(F32).reshape(b, nc, chunk, g, j, p)
    a_c = da.astype(F32).reshape(b, nc, chunk, g, j)
    b_c = bm.astype(F32).reshape(b, nc, chunk, g, n)
    c_c = cm.astype(F32).reshape(b, nc, chunk, g, n)
    a_cs = jnp.cumsum(a_c, axis=2)
    causal = jnp.tril(jnp.ones((chunk, chunk), dtype=bool))
    seg = a_cs[:, :, :, None] - a_cs[:, :, None, :]
    lmat = jnp.exp(jnp.where(causal[:, :, None, None], seg, -jnp.inf))
    cb = jnp.einsum('bclgn,bcsgn->bclsg', c_c, b_c)
    y_diag = jnp.einsum('bclsgj,bcsgjp->bclgjp', cb[..., None] * lmat, x_c)
    decay_states = jnp.exp(a_cs[:, :, -1:] - a_cs)
    states = jnp.einsum('bclgn,bclgj,bclgjp->bcgjpn', b_c, decay_states, x_c)
    init = init_state.astype(F32).reshape(b, g, j, p, n)
    states = jnp.concatenate([init[:, None], states], axis=1)
    chunk_tot = jnp.pad(a_cs[:, :, -1], ((0, 0), (1, 0), (0, 0), (0, 0)))
    ccs = jnp.cumsum(chunk_tot, axis=1)
    seg_c = ccs[:, :, None] - ccs[:, None, :]
    causal_c = jnp.tril(jnp.ones((nc + 1, nc + 1), dtype=bool))
    decay_chunk = jnp.exp(jnp.where(causal_c[:, :, None, None], seg_c, -jnp.inf))
    new_states = jnp.einsum('bzcgj,bcgjpn->bzgjpn', decay_chunk, states)
    prev_states, final = new_states[:, :-1], new_states[:, -1]
    y_off = jnp.einsum('bclgn,bcgjpn,bclgj->bclgjp', c_c, prev_states, jnp.exp(a_cs))
    y = (y_diag + y_off).reshape(b, t, h, p)
    return y, final.reshape(b, h, p, n)


def mamba2_mixer(x, conv_state, ssm_state, w_in, conv_w, conv_b, dt_bias, a_log, d_skip, norm_w, w_out):
    b, t, _ = x.shape
    hproj = x @ w_in
    z = hproj[..., :MB_INNER]
    xbc = hproj[..., MB_INNER:MB_INNER + MB_CONV_DIM]
    dt_raw = hproj[..., MB_INNER + MB_CONV_DIM:]
    xbc_hist = jnp.concatenate([conv_state.astype(xbc.dtype), xbc], axis=1)
    new_conv = xbc_hist[:, -(MB_CONV - 1):]
    xbc = jax.nn.silu(causal_depthwise_conv(xbc_hist, conv_w, conv_b))
    xs = xbc[..., :MB_INNER].reshape(b, t, MB_HEADS, MB_HEADDIM)
    bm = xbc[..., MB_INNER:MB_INNER + MB_GROUPS * MB_STATE].reshape(b, t, MB_GROUPS, MB_STATE)
    cm = xbc[..., MB_INNER + MB_GROUPS * MB_STATE:].reshape(b, t, MB_GROUPS, MB_STATE)
    dt = jax.nn.softplus(dt_raw.astype(F32) + dt_bias.astype(F32))
    a = -jnp.exp(a_log.astype(F32))
    chunk = MB_CHUNK if t % MB_CHUNK == 0 else t
    y, final = ssd_chunked(xs.astype(F32) * dt[..., None], dt * a, bm, cm, ssm_state, chunk)
    y = y + d_skip.astype(F32)[:, None] * xs.astype(F32)
    gsz = MB_INNER // MB_GROUPS
    yg = (y.reshape(b, t, MB_INNER) * jax.nn.silu(z.astype(F32))).reshape(b, t, MB_GROUPS, gsz)
    yg = rms_norm(yg, norm_w.reshape(MB_GROUPS, gsz), MB_NORM_EPS).reshape(b, t, MB_INNER)
    return (yg @ w_out).astype(x.dtype), new_conv, final


def rope_partial(t, pos):
    half = ROPE_DIM // 2
    inv_freq = ROPE_THETA ** (-jnp.arange(0, ROPE_DIM, 2, dtype=F32) / ROPE_DIM)
    ang = pos.astype(F32)[:, None] * inv_freq[None, :]
    cos = jnp.cos(ang)[None, :, None, None, :]
    sin = jnp.sin(ang)[None, :, None, None, :]
    t1 = t[..., :half].astype(F32)
    t2 = t[..., half:ROPE_DIM].astype(F32)
    rot = jnp.concatenate([t1 * cos - t2 * sin, t2 * cos + t1 * sin], axis=-1).astype(t.dtype)
    return jnp.concatenate([rot, t[..., ROPE_DIM:]], axis=-1)


def diff_attn_project(x, pos, w_qkvg):
    b, t, _ = x.shape
    q, k, v, gate = jnp.split(x @ w_qkvg, 4, axis=-1)
    q = rope_partial(q.reshape(b, t, DA_HEADS, 2, DA_HEAD), pos)
    k = rope_partial(k.reshape(b, t, DA_HEADS, 2, DA_HEAD), pos)
    return q, k, v.reshape(b, t, DA_HEADS, DA_VDIM), gate


def diff_attn_prompt(q, k, v, lam):
    b, t = q.shape[0], q.shape[1]
    nb = t // Q_BLOCK
    qb = q.reshape(b, nb, Q_BLOCK, DA_HEADS, 2, DA_HEAD).swapaxes(0, 1)
    qpos = jnp.arange(t, dtype=jnp.int32).reshape(nb, Q_BLOCK)
    kpos = jnp.arange(t, dtype=jnp.int32)
    vf = v.astype(F32)

    def block(args):
        q_i, pos_i = args
        s = jnp.einsum('bqhcd,bkhcd->bhcqk', q_i, k).astype(F32) * DA_SCALE
        s = jnp.where(pos_i[:, None] >= kpos[None, :], s, -jnp.inf)
        pr = jax.nn.softmax(s, axis=-1)
        pr = pr[:, :, 0] - lam * pr[:, :, 1]
        return jnp.einsum('bhqk,bkhe->bqhe', pr, vf)

    o = lax.map(block, (qb, qpos))
    return o.swapaxes(0, 1).reshape(b, t, DA_HEADS, DA_VDIM)


def diff_attn_sample(q, k_new, v_new, cache_k, cache_v, page_table, lam):
    b, tn = q.shape[0], q.shape[1]
    causal = jnp.tril(jnp.ones((tn, tn), dtype=bool))
    s = jnp.einsum('bqhcd,bkhcd->bhcqk', q, k_new).astype(F32) * DA_SCALE
    s = jnp.where(causal, s, -jnp.inf)
    m = jnp.max(s, axis=-1)
    pr = jnp.exp(s - m[..., None])
    l = jnp.sum(pr, axis=-1)
    acc = jnp.einsum('bhcqk,bkhe->bhcqe', pr, v_new.astype(F32))

    def page_step(carry, phys):
        m, l, acc = carry
        kp = cache_k[phys].reshape(b, PAGE_SIZE, DA_HEADS, 2, DA_HEAD)
        vp = cache_v[phys].astype(F32)
        sp = jnp.einsum('bqhcd,bkhcd->bhcqk', q, kp).astype(F32) * DA_SCALE
        m_new = jnp.maximum(m, jnp.max(sp, axis=-1))
        alpha = jnp.exp(m - m_new)
        pp = jnp.exp(sp - m_new[..., None])
        l = l * alpha + jnp.sum(pp, axis=-1)
        acc = acc * alpha[..., None] + jnp.einsum('bhcqk,bkhe->bhcqe', pp, vp)
        return (m_new, l, acc), None

    (m, l, acc), _ = lax.scan(page_step, (m, l, acc), page_table.T)
    o = acc / l[..., None]
    o = o[:, :, 0] - lam * o[:, :, 1]
    return o.transpose(0, 2, 1, 3)


def diff_attn_output(o, gate, subln_w, lam_init, w_o):
    b, t = o.shape[0], o.shape[1]
    o = rms_norm(o, subln_w, DA_SUBLN_EPS) * (1.0 - lam_init)
    y = (o.reshape(b, t, DA_WIDTH) * jax.nn.silu(gate.astype(F32))) @ w_o
    return y.astype(gate.dtype)


def conformer_conv_mixer(x, conv_state, w_in, conv_w, conv_b, ln_w, ln_b, w_out):
    hproj = x @ w_in
    u_a, u_b, gate = jnp.split(hproj, 3, axis=-1)
    u = u_a * jax.nn.sigmoid(u_b)
    u_hist = jnp.concatenate([conv_state.astype(u.dtype), u], axis=1)
    new_state = u_hist[:, -(CF_KERNEL - 1):]
    c = causal_depthwise_conv(u_hist, conv_w, conv_b)
    c = jax.nn.silu(layer_norm(c, ln_w, ln_b, CF_LN_EPS))
    y = (c * jax.nn.silu(gate)) @ w_out
    return y.astype(x.dtype), new_state


def setup_inputs(seed: int = 0) -> dict:
    key = jax.random.key(seed)
    ks = iter(jax.random.split(key, 64))

    def nrm(shape, scale=1.0):
        return jax.random.normal(next(ks), shape, F32) * scale

    def unif(shape, lo, hi):
        return jax.random.uniform(next(ks), shape, F32, lo, hi)

    d = D_MODEL
    dinv = d ** -0.5
    n_pages = PAST_LEN // PAGE_SIZE
    n_used = DEC_BATCH * n_pages
    n_pool = n_used + max(1, n_used // 4)
    inp = {}
    inp['x_prompt'] = nrm((BATCH, SEQ, d))
    inp['x_sample'] = nrm((DEC_BATCH, DEC_SEQ, d))
    inp['state_rwkv_shift'] = nrm((DEC_BATCH, d))
    inp['state_rwkv_wkv'] = nrm((DEC_BATCH, RW_HEADS, RW_HEAD, RW_HEAD), 0.3)
    inp['state_mamba_conv'] = nrm((DEC_BATCH, MB_CONV - 1, MB_CONV_DIM))
    inp['state_mamba_ssm'] = nrm((DEC_BATCH, MB_HEADS, MB_HEADDIM, MB_STATE), 0.1)
    inp['cache_k'] = nrm((n_pool, PAGE_SIZE, DA_HEADS, 2 * DA_HEAD))
    inp['cache_v'] = nrm((n_pool, PAGE_SIZE, DA_HEADS, DA_VDIM))
    inp['page_table'] = jax.random.permutation(next(ks), n_pool)[:n_used].reshape(DEC_BATCH, n_pages).astype(jnp.int32)
    inp['state_conformer_conv'] = nrm((DEC_BATCH, CF_KERNEL - 1, CF_WIDTH), 0.5)
    inp['norm_pre'] = 1.0 + nrm((DEPTH, d), 0.05)
    inp['norm_post'] = 1.0 + nrm((DEPTH, d), 0.05)
    inp['rw_mu'] = unif((RW_N_MIX, d), 0.0, 1.0)
    inp['rw_w_r'] = nrm((d, d), dinv)
    inp['rw_w_k'] = nrm((d, d), dinv)
    inp['rw_w_v'] = nrm((d, d), dinv)
    inp['rw_w_g'] = nrm((d, d), dinv)
    inp['rw_w0'] = unif((d,), -7.0, -1.0)
    inp['rw_w_w1'] = nrm((d, RW_LORA_W), dinv)
    inp['rw_w_w2'] = nrm((RW_LORA_W, d), 0.1 * RW_LORA_W ** -0.5)
    inp['rw_a0'] = nrm((d,), 0.5)
    inp['rw_w_a1'] = nrm((d, RW_LORA_A), dinv)
    inp['rw_w_a2'] = nrm((RW_LORA_A, d), 0.1 * RW_LORA_A ** -0.5)
    inp['rw_k_k'] = 0.85 + nrm((d,), 0.05)
    inp['rw_k_a'] = 1.0 + nrm((d,), 0.05)
    inp['rw_r_k'] = nrm((RW_HEADS, RW_HEAD), 0.1)
    inp['rw_gn_w'] = 1.0 + nrm((d,), 0.05)
    inp['rw_gn_b'] = nrm((d,), 0.02)
    inp['rw_w_o'] = nrm((d, d), dinv)
    dt0 = jnp.exp(unif((MB_HEADS,), math.log(1e-3), math.log(1e-1)))
    inp['mb_w_in'] = nrm((d, MB_INNER + MB_CONV_DIM + MB_HEADS), dinv)
    inp['mb_conv_w'] = nrm((MB_CONV, MB_CONV_DIM), MB_CONV ** -0.5)
    inp['mb_conv_b'] = nrm((MB_CONV_DIM,), 0.02)
    inp['mb_dt_bias'] = dt0 + jnp.log(-jnp.expm1(-dt0))
    inp['mb_a_log'] = jnp.log(unif((MB_HEADS,), 1.0, 16.0))
    inp['mb_d'] = 1.0 + nrm((MB_HEADS,), 0.05)
    inp['mb_norm_w'] = 1.0 + nrm((MB_INNER,), 0.05)
    inp['mb_w_out'] = nrm((MB_INNER, d), MB_INNER ** -0.5)
    inp['da_w_qkvg'] = nrm((d, 4 * DA_WIDTH), dinv)
    inp['da_lq1'] = nrm((DA_HEAD,), 0.1)
    inp['da_lk1'] = nrm((DA_HEAD,), 0.1)
    inp['da_lq2'] = nrm((DA_HEAD,), 0.1)
    inp['da_lk2'] = nrm((DA_HEAD,), 0.1)
    inp['da_subln_w'] = 1.0 + nrm((DA_VDIM,), 0.05)
    inp['da_w_o'] = nrm((DA_WIDTH, d), DA_WIDTH ** -0.5)
    inp['cf_w_in'] = nrm((d, 3 * CF_WIDTH), dinv)
    inp['cf_conv_w'] = nrm((CF_KERNEL, CF_WIDTH), CF_KERNEL ** -0.5)
    inp['cf_conv_b'] = nrm((CF_WIDTH,), 0.02)
    inp['cf_ln_w'] = 1.0 + nrm((CF_WIDTH,), 0.05)
    inp['cf_ln_b'] = nrm((CF_WIDTH,), 0.02)
    inp['cf_w_out'] = nrm((CF_WIDTH, d), CF_WIDTH ** -0.5)
    return inp


def reference(x_prompt, x_sample, state_rwkv_shift, state_rwkv_wkv, state_mamba_conv, state_mamba_ssm,
              cache_k, cache_v, page_table, state_conformer_conv, norm_pre, norm_post,
              rw_mu, rw_w_r, rw_w_k, rw_w_v, rw_w_g, rw_w0, rw_w_w1, rw_w_w2, rw_a0, rw_w_a1, rw_w_a2,
              rw_k_k, rw_k_a, rw_r_k, rw_gn_w, rw_gn_b, rw_w_o,
              mb_w_in, mb_conv_w, mb_conv_b, mb_dt_bias, mb_a_log, mb_d, mb_norm_w, mb_w_out,
              da_w_qkvg, da_lq1, da_lk1, da_lq2, da_lk2, da_subln_w, da_w_o,
              cf_w_in, cf_conv_w, cf_conv_b, cf_ln_w, cf_ln_b, cf_w_out):
    bp, tp = x_prompt.shape[0], x_prompt.shape[1]
    ts = x_sample.shape[1]
    pos_p = jnp.arange(tp, dtype=jnp.int32)
    pos_s = PAST_LEN + jnp.arange(ts, dtype=jnp.int32)
    rw = (rw_mu, rw_w_r, rw_w_k, rw_w_v, rw_w_g, rw_w0, rw_w_w1, rw_w_w2, rw_a0, rw_w_a1, rw_w_a2,
          rw_k_k, rw_k_a, rw_r_k, rw_gn_w, rw_gn_b, rw_w_o)
    mb = (mb_w_in, mb_conv_w, mb_conv_b, mb_dt_bias, mb_a_log, mb_d, mb_norm_w, mb_w_out)
    cf = (cf_w_in, cf_conv_w, cf_conv_b, cf_ln_w, cf_ln_b, cf_w_out)
    hp, hs = x_prompt, x_sample
    for layer in range(DEPTH):
        kind = layer % N_MIXERS
        xp = rms_norm(hp, norm_pre[layer])
        xs = rms_norm(hs, norm_pre[layer])
        if kind == 0:
            zero_shift = jnp.zeros((bp, D_MODEL), xp.dtype)
            zero_wkv = jnp.zeros((bp, RW_HEADS, RW_HEAD, RW_HEAD), F32)
            yp, rw_shift_p, rw_wkv_p = rwkv7_mixer(xp, zero_shift, zero_wkv, *rw)
            ys, rw_shift_s, rw_wkv_s = rwkv7_mixer(xs, state_rwkv_shift, state_rwkv_wkv, *rw)
        elif kind == 1:
            zero_conv = jnp.zeros((bp, MB_CONV - 1, MB_CONV_DIM), xp.dtype)
            zero_ssm = jnp.zeros((bp, MB_HEADS, MB_HEADDIM, MB_STATE), F32)
            yp, mb_conv_p, mb_ssm_p = mamba2_mixer(xp, zero_conv, zero_ssm, *mb)
            ys, mb_conv_s, mb_ssm_s = mamba2_mixer(xs, state_mamba_conv, state_mamba_ssm, *mb)
        elif kind == 2:
            lam_init = 0.8 - 0.6 * math.exp(-0.3 * layer)
            lam = (jnp.exp(jnp.sum(da_lq1.astype(F32) * da_lk1.astype(F32)))
                   - jnp.exp(jnp.sum(da_lq2.astype(F32) * da_lk2.astype(F32))) + lam_init)
            q_p, k_p, v_p, g_p = diff_attn_project(xp, pos_p, da_w_qkvg)
            yp = diff_attn_output(diff_attn_prompt(q_p, k_p, v_p, lam), g_p, da_subln_w, lam_init, da_w_o)
            q_s, k_s, v_s, g_s = diff_attn_project(xs, pos_s, da_w_qkvg)
            o_s = diff_attn_sample(q_s, k_s, v_s, cache_k, cache_v, page_table, lam)
            ys = diff_attn_output(o_s, g_s, da_subln_w, lam_init, da_w_o)
            k_rows_p = k_p.reshape(bp, tp, DA_HEADS, 2 * DA_HEAD)
            k_rows_s = k_s.reshape(k_s.shape[0], ts, DA_HEADS, 2 * DA_HEAD)
            v_rows_p = v_p
            v_rows_s = v_s
        else:
            zero_cf = jnp.zeros((bp, CF_KERNEL - 1, CF_WIDTH), xp.dtype)
            yp, cf_conv_p = conformer_conv_mixer(xp, zero_cf, *cf)
            ys, cf_conv_s = conformer_conv_mixer(xs, state_conformer_conv, *cf)
        hp = hp + rms_norm(yp, norm_post[layer])
        hs = hs + rms_norm(ys, norm_post[layer])
    return (hp, hs, rw_shift_p, rw_shift_s, rw_wkv_p, rw_wkv_s, mb_conv_p, mb_conv_s, mb_ssm_p, mb_ssm_s,
            k_rows_p, k_rows_s, v_rows_p, v_rows_s, cf_conv_p, cf_conv_s)
```

```python
import functools
import math

import jax
import jax.numpy as jnp
from jax import lax
from jax.experimental import pallas as pl
from jax.experimental.pallas import tpu as pltpu

F32 = jnp.float32
BF16 = jnp.bfloat16

D_MODEL = 1024
PAST_LEN = 8192
PAGE_SIZE = 128
RMS_EPS = 1e-6

RW_HEAD = 64
RW_HEADS = D_MODEL // RW_HEAD
RW_LORA = 64
RW_GN_EPS = 64e-5
RW_CHUNK = 64

MB_INNER = 2 * D_MODEL
MB_HEADDIM = 64
MB_HEADS = MB_INNER // MB_HEADDIM
MB_GROUPS = 4
MB_STATE = 128
MB_CONV = 4
MB_CHUNK = 128
MB_CONV_DIM = MB_INNER + 2 * MB_GROUPS * MB_STATE
MB_NORM_EPS = 1e-5
MB_GROUP_WIDTH = MB_INNER // MB_GROUPS

DA_HEADS = 8
DA_HEAD = 64
DA_VDIM = 2 * DA_HEAD
DA_WIDTH = DA_HEADS * DA_VDIM
DA_SCALE = DA_HEAD ** -0.5
DA_SUBLN_EPS = 1e-5
ROPE_THETA = 500000.0
ROPE_DIM = DA_HEAD // 4

CF_WIDTH = D_MODEL
CF_KERNEL = 31
CF_LN_EPS = 1e-5
CF_HIST = 32

LANES = 128
SUBLANES = 8
VMEM_LIMIT = 56 * 1024 * 1024

NEG_INF = float("-inf")


def _cparams(*sem):
    return pltpu.CompilerParams(dimension_semantics=sem, vmem_limit_bytes=VMEM_LIMIT)


def _dot(a, b):
    return jnp.dot(a.astype(BF16), b.astype(BF16), preferred_element_type=F32)


def _dot_nt(a, b):
    return lax.dot_general(a.astype(BF16), b.astype(BF16), (((1,), (1,)), ((), ())),
                           preferred_element_type=F32)


def _dot_tn(a, b):
    return lax.dot_general(a.astype(BF16), b.astype(BF16), (((0,), (0,)), ((), ())),
                           preferred_element_type=F32)


def _split2(a):
    hi = a.astype(BF16)
    lo = (a - hi.astype(F32)).astype(BF16)
    return hi, lo


def _split3(a):
    hi = a.astype(BF16)
    r1 = a - hi.astype(F32)
    mid = r1.astype(BF16)
    lo = (r1 - mid.astype(F32)).astype(BF16)
    return hi, mid, lo


def _dotx(a, b, dn=(((1,), (0,)), ((), ()))):
    ah, al = _split2(a)
    bh, bl = _split2(b)
    f = functools.partial(lax.dot_general, dimension_numbers=dn, preferred_element_type=F32)
    return f(ah, bh) + (f(ah, bl) + f(al, bh))


_NT = (((1,), (1,)), ((), ()))
_TN = (((0,), (0,)), ((), ()))


def _cumsum_rows(tri_bf16, x):
    hi, mid, lo = _split3(x)
    f = functools.partial(jnp.dot, preferred_element_type=F32)
    return f(tri_bf16, hi) + (f(tri_bf16, mid) + f(tri_bf16, lo))


def _sigmoid(x):
    return 1.0 / (1.0 + jnp.exp(-x))


def _silu(x):
    return x * _sigmoid(x)


def _softplus(x):
    return jnp.maximum(x, 0.0) + jnp.log(1.0 + jnp.exp(-jnp.abs(x)))


def _rms(x, w, eps):
    return x * lax.rsqrt(jnp.mean(x * x, axis=-1, keepdims=True) + eps) * w


def _row_tile(m):
    for t in (512, 256, 128, 64, 32, 16, 8):
        if m % t == 0:
            return t
    return m


def _full(shape):
    nd = len(shape)
    return pl.BlockSpec(shape, lambda *_: (0,) * nd)


def _out_proj_body(has_gate, *refs):
    if has_gate:
        y_ref, g_ref, h_ref, w_ref, pw_ref, o_ref = refs
        y = y_ref[...] * _silu(g_ref[...])
    else:
        y_ref, h_ref, w_ref, pw_ref, o_ref = refs
        y = y_ref[...]
    z = _dot(y, w_ref[...])
    o_ref[...] = h_ref[...] + _rms(z, pw_ref[...], RMS_EPS)


def _out_proj(y, gate, h, w, post_w):
    m, kdim = y.shape
    tm = _row_tile(m)
    row = lambda width: pl.BlockSpec((tm, width), lambda i: (i, 0))
    args, specs = [y], [row(kdim)]
    if gate is not None:
        args.append(gate)
        specs.append(row(D_MODEL))
    args += [h, w.astype(BF16), post_w.reshape(1, D_MODEL)]
    specs += [row(D_MODEL), _full((kdim, D_MODEL)), _full((1, D_MODEL))]
    return pl.pallas_call(
        functools.partial(_out_proj_body, gate is not None),
        grid=(m // tm,), in_specs=specs, out_specs=row(D_MODEL),
        out_shape=jax.ShapeDtypeStruct((m, D_MODEL), F32),
        compiler_params=_cparams("parallel"))(*args)


def _in_proj_body(n_w, n_row, n_full, epilogue, *refs):
    h_ref, nw_ref = refs[0], refs[1]
    w_refs = refs[2:2 + n_w]
    row_refs = refs[2 + n_w:2 + n_w + n_row]
    full_refs = refs[2 + n_w + n_row:2 + n_w + n_row + n_full]
    out_refs = refs[2 + n_w + n_row + n_full:]
    xn = _rms(h_ref[...], nw_ref[...], RMS_EPS).astype(BF16)
    ys = [jnp.dot(xn, w[...], preferred_element_type=F32) for w in w_refs]
    outs = epilogue(ys, [r[...] for r in row_refs], [r[...] for r in full_refs])
    for o_ref, o in zip(out_refs, outs):
        o_ref[...] = o


def _in_proj(h, norm_w, weights, epilogue, out_widths, row_extras=(), full_extras=(), tm=None):
    m = h.shape[0]
    tm = tm or _row_tile(m)
    row = lambda width: pl.BlockSpec((tm, width), lambda i: (i, 0))
    args = [h, norm_w.reshape(1, D_MODEL)] + [w.astype(BF16) for w in weights]
    specs = [row(D_MODEL), _full((1, D_MODEL))] + [_full(w.shape) for w in weights]
    for arr, imap, width in row_extras:
        args.append(arr)
        specs.append(pl.BlockSpec((tm, width), imap))
    for arr in full_extras:
        args.append(arr)
        specs.append(_full(arr.shape))
    return pl.pallas_call(
        functools.partial(_in_proj_body, len(weights), len(row_extras), len(full_extras), epilogue),
        grid=(m // tm,), in_specs=specs,
        out_specs=[row(wd) for wd in out_widths],
        out_shape=[jax.ShapeDtypeStruct((m, wd), F32) for wd in out_widths],
        compiler_params=_cparams("parallel"))(*args)


def _rwkv_mix(xn, xprev, mu, wr, wk, wv, wg, w0, ww1, ww2, a0, wa1, wa2):
    xx = xprev - xn
    xm = [xn + xx * mu[i:i + 1, :] for i in range(6)]
    r = _dot(xm[0], wr)
    k = _dot(xm[2], wk)
    v = _dot(xm[3], wv)
    g = _dot(xm[5], wg)
    zw = w0 + _dot(jnp.tanh(_dot(xm[1], ww1)), ww2)
    w_log = -_softplus(-zw) - 0.5
    lw = -jnp.exp(w_log)
    a = _sigmoid(a0 + _dot(_dot(xm[4], wa1), wa2))
    return r, k, v, g, a, lw


def _rwkv_proj_seq_body(tiles_per_seq, h_ref, hp_ref, nw_ref, mu_ref, wr, wk, wv, wg, w0, ww1, ww2,
                        a0, wa1, wa2, r_ref, k_ref, v_ref, g_ref, a_ref, lw_ref, last_ref):
    i = pl.program_id(0)
    nw = nw_ref[...]
    xn = _rms(h_ref[...], nw, RMS_EPS)
    tm = xn.shape[0]
    pn = _rms(hp_ref[...], nw, RMS_EPS)[SUBLANES - 1:SUBLANES, :]
    first = (i % tiles_per_seq) == 0
    prow = jnp.where(first, 0.0, pn)
    rows = lax.broadcasted_iota(jnp.int32, xn.shape, 0)
    xprev = jnp.where(rows == 0, prow, pltpu.roll(xn, 1, axis=0))
    outs = _rwkv_mix(xn, xprev, mu_ref[...], wr[...], wk[...], wv[...], wg[...], w0[...], ww1[...],
                     ww2[...], a0[...], wa1[...], wa2[...])
    for o_ref, o in zip((r_ref, k_ref, v_ref, g_ref, a_ref, lw_ref), outs):
        o_ref[...] = o
    last_ref[0] = xn[tm - 1:tm, :]


def _rwkv_proj_step_body(h_ref, sh_ref, nw_ref, mu_ref, wr, wk, wv, wg, w0, ww1, ww2,
                         a0, wa1, wa2, r_ref, k_ref, v_ref, g_ref, a_ref, lw_ref, last_ref):
    xn = _rms(h_ref[...], nw_ref[...], RMS_EPS)
    outs = _rwkv_mix(xn, sh_ref[...], mu_ref[...], wr[...], wk[...], wv[...], wg[...], w0[...], ww1[...],
                     ww2[...], a0[...], wa1[...], wa2[...])
    for o_ref, o in zip((r_ref, k_ref, v_ref, g_ref, a_ref, lw_ref), outs):
        o_ref[...] = o
    last_ref[...] = xn


def _rwkv_weights(p):
    d = D_MODEL
    vec = lambda x: x.reshape(1, d)
    args = [p['rw_mu'], p['rw_w_r'].astype(BF16), p['rw_w_k'].astype(BF16), p['rw_w_v'].astype(BF16),
            p['rw_w_g'].astype(BF16), vec(p['rw_w0']), p['rw_w_w1'].astype(BF16), p['rw_w_w2'].astype(BF16),
            vec(p['rw_a0']), p['rw_w_a1'].astype(BF16), p['rw_w_a2'].astype(BF16)]
    return args, [_full(a.shape) for a in args]


def _rwkv_proj_seq(h2, b, t, norm_w, p):
    m = b * t
    tm = min(_row_tile(t), 256)
    tps = t // tm
    row = pl.BlockSpec((tm, D_MODEL), lambda i: (i, 0))
    prev = pl.BlockSpec((SUBLANES, D_MODEL), lambda i: (jnp.maximum(i * (tm // SUBLANES) - 1, 0), 0))
    wargs, wspecs = _rwkv_weights(p)
    outs = pl.pallas_call(
        functools.partial(_rwkv_proj_seq_body, tps),
        grid=(m // tm,),
        in_specs=[row, prev, _full((1, D_MODEL))] + wspecs,
        out_specs=[row] * 6 + [pl.BlockSpec((1, 1, D_MODEL), lambda i: (i // tps, 0, 0))],
        out_shape=[jax.ShapeDtypeStruct((m, D_MODEL), F32)] * 6
        + [jax.ShapeDtypeStruct((b, 1, D_MODEL), F32)],
        compiler_params=_cparams("arbitrary"))(h2, h2, norm_w.reshape(1, D_MODEL), *wargs)
    return outs[:6], outs[6].reshape(b, D_MODEL)


def _rwkv_proj_step(h2, shift, norm_w, p):
    m = h2.shape[0]
    wargs, wspecs = _rwkv_weights(p)
    outs = pl.pallas_call(
        _rwkv_proj_step_body,
        grid=(1,),
        in_specs=[_full((m, D_MODEL)), _full((m, D_MODEL)), _full((1, D_MODEL))] + wspecs,
        out_specs=[_full((m, D_MODEL))] * 7,
        out_shape=[jax.ShapeDtypeStruct((m, D_MODEL), F32)] * 7,
        compiler_params=_cparams("arbitrary"))(h2, shift, norm_w.reshape(1, D_MODEL), *wargs)
    return outs[:6], outs[6]


def _pair_sum(x, m0):
    s0 = jnp.sum(jnp.where(m0, x, 0.0), axis=-1, keepdims=True)
    s1 = jnp.sum(jnp.where(m0, 0.0, x), axis=-1, keepdims=True)
    return jnp.where(m0, s0, s1)


def _rwkv_chunk_body(r_ref, k_ref, v_ref, a_ref, lw_ref, kk_ref, ka_ref, rk_ref, gw_ref, gb_ref,
                     o_ref, s_ref, bd_ref):
    c = pl.program_id(2)
    nc = pl.num_programs(2)
    cl = RW_CHUNK

    @pl.when(c == 0)
    def _():
        bd_ref[...] = jnp.zeros_like(bd_ref)

    r = r_ref[0]
    k = k_ref[0]
    v = v_ref[0]
    a = a_ref[0]
    lw = lw_ref[0]
    lane = lax.broadcasted_iota(jnp.int32, (cl, LANES), 1)
    m0 = lane < RW_HEAD

    kkv = k * kk_ref[...]
    ssq = _pair_sum(kkv * kkv, m0)
    kap = kkv * lax.rsqrt(jnp.maximum(ssq, 1e-24))
    kmod = k * (1.0 + (a - 1.0) * ka_ref[...])
    bvec = kap * a

    ti = lax.broadcasted_iota(jnp.int32, (cl, cl), 0)
    tj = lax.broadcasted_iota(jnp.int32, (cl, cl), 1)
    incl = ti >= tj
    strict = ti > tj
    tri = jnp.where(incl, 1.0, 0.0).astype(BF16)
    cum = _cumsum_rows(tri, lw)
    cum_last = cum[cl - 1:cl, :]
    kap_h = kap * jnp.exp(cum - lw)
    r_h = r * jnp.exp(cum)
    g_inv = jnp.exp(-cum)
    b_h = bvec * g_inv
    k_h = kmod * g_inv
    g_end = jnp.exp(cum_last - cum)
    b_e = bvec * g_end
    k_e = kmod * g_end
    g_all = jnp.exp(cum_last)

    eye = jnp.where(ti == tj, 1.0, 0.0)
    p_parts, q_parts, rt_parts, oi_parts = [], [], [], []
    for hd in range(2):
        mh = m0 if hd == 0 else jnp.logical_not(m0)
        lhs = jnp.concatenate([jnp.where(mh, kap_h, 0.0), jnp.where(mh, r_h, 0.0)], axis=0)
        ab = _dotx(lhs, b_h, _NT)
        ak = _dotx(lhs, k_h, _NT)
        a_ab = jnp.where(strict, ab[:cl], 0.0)
        a_ak = jnp.where(strict, ak[:cl], 0.0)
        a_rb = jnp.where(incl, ab[cl:], 0.0)
        a_rk = jnp.where(incl, ak[cl:], 0.0)
        tinv = eye - a_ab
        lp = a_ab
        for _ in range(5):
            lp = _dotx(lp, lp)
            tinv = tinv + _dotx(tinv, lp)
        x2 = _dotx(a_ak, v)
        pm = -_dotx(tinv, kap_h)
        qm = -_dotx(tinv, x2)
        p_parts.append(pm)
        q_parts.append(qm)
        rt_parts.append(r_h + _dotx(a_rb, pm))
        oi_parts.append(_dotx(a_rb, qm) + _dotx(a_rk, v))
    pm = jnp.where(m0, p_parts[0], p_parts[1])
    qm = jnp.where(m0, q_parts[0], q_parts[1])
    rt = jnp.where(m0, rt_parts[0], rt_parts[1])
    oi = jnp.where(m0, oi_parts[0], oi_parts[1])

    bd = bd_ref[...]
    o = _dotx(rt, bd, _NT) + oi
    si = lax.broadcasted_iota(jnp.int32, (LANES, LANES), 0)
    sj = lax.broadcasted_iota(jnp.int32, (LANES, LANES), 1)
    same = (si < RW_HEAD) == (sj < RW_HEAD)
    mmat = jnp.where(same, _dotx(pm, b_e, _TN), 0.0) + jnp.where(si == sj, g_all, 0.0)
    nmat = jnp.where(same, _dotx(qm, b_e, _TN) + _dotx(v, k_e, _TN), 0.0)
    bd_new = _dotx(bd, mmat) + nmat
    bd_ref[...] = bd_new

    mean = _pair_sum(o, m0) * (1.0 / RW_HEAD)
    oc = o - mean
    var = _pair_sum(oc * oc, m0) * (1.0 / RW_HEAD)
    on = oc * lax.rsqrt(var + RW_GN_EPS) * gw_ref[...] + gb_ref[...]
    bonus = _pair_sum(r * kmod * rk_ref[...], m0)
    o_ref[0] = on + bonus * v

    @pl.when(c == nc - 1)
    def _():
        s_ref[0, 0] = bd_new[:RW_HEAD, :RW_HEAD]
        s_ref[0, 1] = bd_new[RW_HEAD:, RW_HEAD:]


def _rwkv_chunk(r, k, v, a, lw, b, t, p):
    hp = RW_HEADS // 2
    nc = t // RW_CHUNK
    shp = lambda x: x.reshape(b, t, D_MODEL)
    blk = pl.BlockSpec((1, RW_CHUNK, LANES), lambda bi, hi, ci: (bi, ci, hi))
    vec = pl.BlockSpec((1, LANES), lambda bi, hi, ci: (0, hi))
    vrow = lambda x: x.reshape(1, D_MODEL)
    o, s = pl.pallas_call(
        _rwkv_chunk_body,
        grid=(b, hp, nc),
        in_specs=[blk] * 5 + [vec] * 5,
        out_specs=[blk, pl.BlockSpec((1, 2, RW_HEAD, RW_HEAD), lambda bi, hi, ci: (bi, hi, 0, 0))],
        out_shape=[jax.ShapeDtypeStruct((b, t, D_MODEL), F32),
                   jax.ShapeDtypeStruct((b, RW_HEADS, RW_HEAD, RW_HEAD), F32)],
        scratch_shapes=[pltpu.VMEM((LANES, LANES), F32)],
        compiler_params=_cparams("parallel", "parallel", "arbitrary"))(
            shp(r), shp(k), shp(v), shp(a), shp(lw), vrow(p['rw_k_k']), vrow(p['rw_k_a']),
            vrow(p['rw_r_k']), vrow(p['rw_gn_w']), vrow(p['rw_gn_b']))
    return o.reshape(b * t, D_MODEL), s


def _rwkv_step_body(r_ref, k_ref, vc_ref, a_ref, lw_ref, s_ref, kk_ref, ka_ref, rk_ref, gw_ref, gb_ref,
                    o_ref, so_ref):
    r = r_ref[0]
    k = k_ref[0]
    a = a_ref[0]
    lw = lw_ref[0]
    vc = vc_ref[0]
    s = s_ref[0]
    kkv = k * kk_ref[...]
    kap = kkv * lax.rsqrt(jnp.maximum(jnp.sum(kkv * kkv, axis=-1, keepdims=True), 1e-24))
    kmod = k * (1.0 + (a - 1.0) * ka_ref[...])
    bvec = kap * a
    dec = jnp.exp(lw)
    sa = jnp.sum(s * kap, axis=-1, keepdims=True)
    s_new = s * dec - sa * bvec + vc * kmod
    so_ref[0] = s_new
    o = jnp.sum(s_new * r, axis=-1, keepdims=True)
    mean = jnp.mean(o, axis=1, keepdims=True)
    oc = o - mean
    var = jnp.mean(oc * oc, axis=1, keepdims=True)
    on = oc * lax.rsqrt(var + RW_GN_EPS) * gw_ref[...] + gb_ref[...]
    bonus = jnp.sum(r * kmod * rk_ref[...], axis=-1, keepdims=True)
    o_ref[0] = on + bonus * vc


def _rwkv_step(r, k, v, a, lw, state, p):
    b = r.shape[0]
    hrow = (RW_HEADS, 1, RW_HEAD)
    hcol = (RW_HEADS, RW_HEAD, 1)
    rows = lambda x: x.reshape((b,) + hrow)
    rblk = pl.BlockSpec((1,) + hrow, lambda i: (i, 0, 0, 0))
    cblk = pl.BlockSpec((1,) + hcol, lambda i: (i, 0, 0, 0))
    sblk = pl.BlockSpec((1, RW_HEADS, RW_HEAD, RW_HEAD), lambda i: (i, 0, 0, 0))
    o, s = pl.pallas_call(
        _rwkv_step_body,
        grid=(b,),
        in_specs=[rblk, rblk, cblk, rblk, rblk, sblk] + [_full(hrow)] * 3 + [_full(hcol)] * 2,
        out_specs=[cblk, sblk],
        out_shape=[jax.ShapeDtypeStruct((b,) + hcol, F32),
                   jax.ShapeDtypeStruct((b, RW_HEADS, RW_HEAD, RW_HEAD), F32)],
        compiler_params=_cparams("parallel"))(
            rows(r), rows(k), v.reshape((b,) + hcol), rows(a), rows(lw), state,
            p['rw_k_k'].reshape(hrow), p['rw_k_a'].reshape(hrow), p['rw_r_k'].reshape(hrow),
            p['rw_gn_w'].reshape(hcol), p['rw_gn_b'].reshape(hcol))
    return o.reshape(b, D_MODEL), s


def _mamba_weights(p):
    w = p['mb_w_in']
    wz = w[:, :MB_INNER]
    wx = w[:, MB_INNER:MB_INNER + MB_CONV_DIM]
    wdt = jnp.pad(w[:, MB_INNER + MB_CONV_DIM:], ((0, 0), (0, LANES - MB_HEADS)))
    return [wz, wx, wdt]


def _mamba_vecs(p):
    pad = lambda x: jnp.pad(x.reshape(1, MB_HEADS), ((0, 0), (0, LANES - MB_HEADS)))
    return dict(conv_w=p['mb_conv_w'], conv_b=p['mb_conv_b'].reshape(1, MB_CONV_DIM),
                dt_bias=pad(p['mb_dt_bias']), a_log=pad(p['mb_a_log']),
                d_skip=jnp.repeat(p['mb_d'], MB_HEADDIM).reshape(1, MB_INNER),
                norm_w=p['mb_norm_w'].reshape(1, MB_INNER))


def _mamba_chunk_body(x_ref, xp_ref, z_ref, dt_ref, cw_ref, cb_ref, dtb_ref, alog_ref, dsk_ref, nw_ref,
                      y_ref, fs_ref, st_ref):
    c = pl.program_id(1)
    nc = pl.num_programs(1)
    cl = MB_CHUNK

    @pl.when(c == 0)
    def _():
        st_ref[...] = jnp.zeros_like(st_ref)

    hist = jnp.where(c == 0, 0.0, xp_ref[0])
    xcat = jnp.concatenate([hist, x_ref[0]], axis=0)
    cw = cw_ref[...]
    conv = cb_ref[...] + cw[0:1, :] * xcat[5:5 + cl]
    for j in range(1, MB_CONV):
        conv = conv + cw[j:j + 1, :] * xcat[5 + j:5 + j + cl]
    conv = _silu(conv)

    dt = _softplus(dt_ref[0] + dtb_ref[...])
    da = dt * (-jnp.exp(alog_ref[...]))
    ti = lax.broadcasted_iota(jnp.int32, (cl, cl), 0)
    tj = lax.broadcasted_iota(jnp.int32, (cl, cl), 1)
    causal = ti >= tj
    tri = jnp.where(causal, 1.0, 0.0).astype(BF16)
    a_cs = _cumsum_rows(tri, da)
    a_cs_t = a_cs.T
    a_last = a_cs[cl - 1:cl, :]
    dec_to_end = jnp.exp(a_last - a_cs)
    e_cs = jnp.exp(a_cs)
    e_last = jnp.exp(a_last)

    lane = lax.broadcasted_iota(jnp.int32, (cl, LANES), 1)
    m0 = lane < MB_HEADDIM
    row0 = lax.broadcasted_iota(jnp.int32, (LANES, LANES), 0) < MB_HEADDIM
    heads_per_group = MB_HEADS // MB_GROUPS
    pairs_per_group = heads_per_group // 2
    for g in range(MB_GROUPS):
        bm = conv[:, MB_INNER + g * MB_STATE:MB_INNER + (g + 1) * MB_STATE]
        cm = conv[:, MB_INNER + (MB_GROUPS + g) * MB_STATE:MB_INNER + (MB_GROUPS + g + 1) * MB_STATE]
        cb = _dot_nt(cm, bm)
        blocks = []
        for j in range(pairs_per_group):
            pi = g * pairs_per_group + j
            h0, h1 = 2 * pi, 2 * pi + 1
            lo, hi = pi * LANES, (pi + 1) * LANES
            xs = conv[:, lo:hi]
            dt2 = jnp.where(m0, dt[:, h0:h0 + 1], dt[:, h1:h1 + 1])
            xdt = xs * dt2
            l0 = jnp.exp(jnp.where(causal, a_cs[:, h0:h0 + 1] - a_cs_t[h0:h0 + 1, :], NEG_INF))
            l1 = jnp.exp(jnp.where(causal, a_cs[:, h1:h1 + 1] - a_cs_t[h1:h1 + 1, :], NEG_INF))
            lhs = jnp.concatenate([(cb * l0).astype(BF16), (cb * l1).astype(BF16)], axis=1)
            rhs = jnp.concatenate([jnp.where(m0, xdt, 0.0).astype(BF16),
                                   jnp.where(m0, 0.0, xdt).astype(BF16)], axis=0)
            y = jnp.dot(lhs, rhs, preferred_element_type=F32)
            st = st_ref[pi]
            e2 = jnp.where(m0, e_cs[:, h0:h0 + 1], e_cs[:, h1:h1 + 1])
            y = y + _dot_nt(cm, st) * e2
            d2 = jnp.where(m0, dec_to_end[:, h0:h0 + 1], dec_to_end[:, h1:h1 + 1])
            sc = jnp.where(row0, e_last[:, h0:h0 + 1], e_last[:, h1:h1 + 1])
            st_ref[pi] = st * sc + _dot_tn(xdt * d2, bm)
            y = y + dsk_ref[:, lo:hi] * xs
            blocks.append(y * _silu(z_ref[0, :, lo:hi]))
        ssq = blocks[0] * blocks[0]
        for blk in blocks[1:]:
            ssq = ssq + blk * blk
        scale = lax.rsqrt(jnp.sum(ssq, axis=-1, keepdims=True) * (1.0 / MB_GROUP_WIDTH) + MB_NORM_EPS)
        for j, blk in enumerate(blocks):
            lo = (g * pairs_per_group + j) * LANES
            y_ref[0, :, lo:lo + LANES] = blk * scale * nw_ref[:, lo:lo + LANES]

    @pl.when(c == nc - 1)
    def _():
        fs_ref[0] = st_ref[...]


def _mamba_chunk(z, xbc, dtr, b, t, vecs):
    nc = t // MB_CHUNK
    npairs = MB_HEADS // 2
    per = MB_CHUNK // SUBLANES
    y, fs = pl.pallas_call(
        _mamba_chunk_body,
        grid=(b, nc),
        in_specs=[pl.BlockSpec((1, MB_CHUNK, MB_CONV_DIM), lambda bi, ci: (bi, ci, 0)),
                  pl.BlockSpec((1, SUBLANES, MB_CONV_DIM),
                               lambda bi, ci: (bi, jnp.maximum(ci * per - 1, 0), 0)),
                  pl.BlockSpec((1, MB_CHUNK, MB_INNER), lambda bi, ci: (bi, ci, 0)),
                  pl.BlockSpec((1, MB_CHUNK, LANES), lambda bi, ci: (bi, ci, 0)),
                  _full((MB_CONV, MB_CONV_DIM)), _full((1, MB_CONV_DIM)), _full((1, LANES)),
                  _full((1, LANES)), _full((1, MB_INNER)), _full((1, MB_INNER))],
        out_specs=[pl.BlockSpec((1, MB_CHUNK, MB_INNER), lambda bi, ci: (bi, ci, 0)),
                   pl.BlockSpec((1, npairs, LANES, MB_STATE), lambda bi, ci: (bi, 0, 0, 0))],
        out_shape=[jax.ShapeDtypeStruct((b, t, MB_INNER), F32),
                   jax.ShapeDtypeStruct((b, npairs, LANES, MB_STATE), F32)],
        scratch_shapes=[pltpu.VMEM((npairs, LANES, MB_STATE), F32)],
        compiler_params=_cparams("parallel", "arbitrary"))(
            xbc.reshape(b, t, MB_CONV_DIM), xbc.reshape(b, t, MB_CONV_DIM), z.reshape(b, t, MB_INNER),
            dtr.reshape(b, t, LANES), vecs['conv_w'], vecs['conv_b'], vecs['dt_bias'], vecs['a_log'],
            vecs['d_skip'], vecs['norm_w'])
    return y.reshape(b * t, MB_INNER), fs.reshape(b, MB_HEADS, MB_HEADDIM, MB_STATE)


def _mamba_step_conv_body(x_ref, cs_ref, dt_ref, cw_ref, cb_ref, dtb_ref, alog_ref, conv_ref, dt_o, ed_o):
    cw = cw_ref[...]
    conv = cb_ref[...] + cw[MB_CONV - 1:MB_CONV, :] * x_ref[...]
    for j in range(MB_CONV - 1):
        conv = conv + cw[j:j + 1, :] * cs_ref[:, j, :]
    conv_ref[...] = _silu(conv)
    dt = _softplus(dt_ref[...] + dtb_ref[...])
    dt_o[...] = dt
    ed_o[...] = jnp.exp(dt * (-jnp.exp(alog_ref[...])))


def _mamba_step_state_body(s_ref, xc_ref, dt_ref, ed_ref, bm_ref, cm_ref, so_ref, y_ref):
    hpg = MB_HEADS // MB_GROUPS
    for g in range(MB_GROUPS):
        hs = slice(g * hpg, (g + 1) * hpg)
        s = s_ref[0, hs]
        xdt = xc_ref[0, hs] * dt_ref[0, hs]
        fin = s * ed_ref[0, hs] + xdt * bm_ref[0, g:g + 1]
        so_ref[0, hs] = fin
        y_ref[0, hs] = jnp.sum(fin * cm_ref[0, g:g + 1], axis=-1, keepdims=True)


def _mamba_step_out_body(y_ref, x_ref, z_ref, dsk_ref, nw_ref, o_ref):
    y = (y_ref[...] + dsk_ref[...] * x_ref[...]) * _silu(z_ref[...])
    for g in range(MB_GROUPS):
        lo, hi = g * MB_GROUP_WIDTH, (g + 1) * MB_GROUP_WIDTH
        o_ref[:, lo:hi] = _rms(y[:, lo:hi], nw_ref[:, lo:hi], MB_NORM_EPS)


def _mamba_step(z, xbc, dtr, conv_state, ssm_state, vecs):
    b = z.shape[0]
    conv, dt, ed = pl.pallas_call(
        _mamba_step_conv_body,
        grid=(1,),
        in_specs=[_full((b, MB_CONV_DIM)), _full((b, MB_CONV - 1, MB_CONV_DIM)), _full((b, LANES)),
                  _full((MB_CONV, MB_CONV_DIM)), _full((1, MB_CONV_DIM)), _full((1, LANES)), _full((1, LANES))],
        out_specs=[_full((b, MB_CONV_DIM)), _full((b, LANES)), _full((b, LANES))],
        out_shape=[jax.ShapeDtypeStruct((b, MB_CONV_DIM), F32), jax.ShapeDtypeStruct((b, LANES), F32),
                   jax.ShapeDtypeStruct((b, LANES), F32)],
        compiler_params=_cparams("arbitrary"))(
            xbc, conv_state, dtr, vecs['conv_w'], vecs['conv_b'], vecs['dt_bias'], vecs['a_log'])
    xs = conv[:, :MB_INNER]
    bm = conv[:, MB_INNER:MB_INNER + MB_GROUPS * MB_STATE].reshape(b, MB_GROUPS, MB_STATE)
    cm = conv[:, MB_INNER + MB_GROUPS * MB_STATE:].reshape(b, MB_GROUPS, MB_STATE)
    hp1 = (MB_HEADS, MB_HEADDIM, 1)
    h11 = (MB_HEADS, 1, 1)
    per = lambda shape: pl.BlockSpec((1,) + shape, lambda i: (i,) + (0,) * len(shape))
    st_shape = (MB_HEADS, MB_HEADDIM, MB_STATE)
    fin, y = pl.pallas_call(
        _mamba_step_state_body,
        grid=(b,),
        in_specs=[per(st_shape), per(hp1), per(h11), per(h11), per((MB_GROUPS, MB_STATE)),
                  per((MB_GROUPS, MB_STATE))],
        out_specs=[per(st_shape), per(hp1)],
        out_shape=[jax.ShapeDtypeStruct((b,) + st_shape, F32), jax.ShapeDtypeStruct((b,) + hp1, F32)],
        compiler_params=_cparams("parallel"))(
            ssm_state, xs.reshape((b,) + hp1), dt[:, :MB_HEADS].reshape((b,) + h11),
            ed[:, :MB_HEADS].reshape((b,) + h11), bm, cm)
    yg = pl.pallas_call(
        _mamba_step_out_body,
        grid=(1,),
        in_specs=[_full((b, MB_INNER))] * 3 + [_full((1, MB_INNER))] * 2,
        out_specs=_full((b, MB_INNER)),
        out_shape=jax.ShapeDtypeStruct((b, MB_INNER), F32),
        compiler_params=_cparams("arbitrary"))(
            y.reshape(b, MB_INNER), xs, z, vecs['d_skip'], vecs['norm_w'])
    return yg, fin


def _rope_tables(pos):
    half = ROPE_DIM // 2
    inv_freq = ROPE_THETA ** (-jnp.arange(0, ROPE_DIM, 2, dtype=F32) / ROPE_DIM)
    ang = pos.astype(F32)[:, None] * inv_freq[None, :]
    cos, sin = jnp.cos(ang), jnp.sin(ang)
    n = pos.shape[0]
    ones = jnp.ones((n, DA_HEAD - ROPE_DIM), F32)
    zeros = jnp.zeros((n, DA_HEAD - ROPE_DIM), F32)
    zh = jnp.zeros((n, half), F32)
    ct = jnp.concatenate([cos, cos, ones], axis=1)
    s_up = jnp.concatenate([-sin, zh, zeros], axis=1)
    s_dn = jnp.concatenate([zh, sin, zeros], axis=1)
    two = lambda x: jnp.concatenate([x, x], axis=1)
    return two(ct), two(s_up), two(s_dn)


def _rope(x, ct, s_up, s_dn):
    half = ROPE_DIM // 2
    cols = []
    for j in range(x.shape[1] // LANES):
        xj = x[:, j * LANES:(j + 1) * LANES]
        cols.append(xj * ct + pltpu.roll(xj, LANES - half, axis=1) * s_up + pltpu.roll(xj, half, axis=1) * s_dn)
    return jnp.concatenate(cols, axis=1)


def _da_epilogue(ys, rows, fulls):
    y = ys[0]
    ct, s_up, s_dn = rows
    q = _rope(y[:, :DA_WIDTH], ct, s_up, s_dn)
    k = _rope(y[:, DA_WIDTH:2 * DA_WIDTH], ct, s_up, s_dn)
    return q, k, y[:, 2 * DA_WIDTH:3 * DA_WIDTH], y[:, 3 * DA_WIDTH:]


def _lambda(lq1, lk1, lq2, lk2, lam_init):
    return (jnp.exp(jnp.sum(lq1 * lk1, axis=-1, keepdims=True))
            - jnp.exp(jnp.sum(lq2 * lk2, axis=-1, keepdims=True)) + lam_init)


def _flash_body(lam_init, tq, q_ref, k_ref, v_ref, lq1, lk1, lq2, lk2, sw_ref, o_ref,
                m1, l1, m2, l2, acc1, acc2):
    qi = pl.program_id(2)
    kj = pl.program_id(3)

    @pl.when(kj == 0)
    def _():
        for m in (m1, m2):
            m[...] = jnp.full_like(m, NEG_INF)
        for z in (l1, l2, acc1, acc2):
            z[...] = jnp.zeros_like(z)

    @pl.when(kj <= qi)
    def _():
        q = q_ref[0] * DA_SCALE
        k = k_ref[0].astype(BF16)
        v = v_ref[0].astype(BF16)
        lane = lax.broadcasted_iota(jnp.int32, q.shape, 1)
        m0 = lane < DA_HEAD
        rows = qi * tq + lax.broadcasted_iota(jnp.int32, (tq, tq), 0)
        cols = kj * tq + lax.broadcasted_iota(jnp.int32, (tq, tq), 1)
        keep = rows >= cols
        for comp, (m_ref, l_ref, acc_ref) in enumerate(((m1, l1, acc1), (m2, l2, acc2))):
            qc = jnp.where(m0, q, 0.0) if comp == 0 else jnp.where(m0, 0.0, q)
            s = lax.dot_general(qc.astype(BF16), k, _NT, preferred_element_type=F32)
            s = jnp.where(keep, s, NEG_INF)
            m_old = m_ref[...]
            m_new = jnp.maximum(m_old, jnp.max(s, axis=-1, keepdims=True))
            alpha = jnp.exp(m_old - m_new)
            pr = jnp.exp(s - m_new)
            l_ref[...] = l_ref[...] * alpha + jnp.sum(pr, axis=-1, keepdims=True)
            acc_ref[...] = acc_ref[...] * alpha + jnp.dot(pr.astype(BF16), v, preferred_element_type=F32)
            m_ref[...] = m_new

    @pl.when(kj == qi)
    def _():
        lam = _lambda(lq1[...], lk1[...], lq2[...], lk2[...], lam_init)
        o = acc1[...] / l1[...] - lam * (acc2[...] / l2[...])
        o_ref[0] = _rms(o, sw_ref[...], DA_SUBLN_EPS) * (1.0 - lam_init)


def _flash(q, k, v, b, t, p, lam_init):
    tq = min(t, 512)
    nq = t // tq
    shp = lambda x: x.reshape(b, t, DA_WIDTH)
    qblk = pl.BlockSpec((1, tq, LANES), lambda bi, hi, qi, kj: (bi, qi, hi))
    kblk = pl.BlockSpec((1, tq, LANES), lambda bi, hi, qi, kj: (bi, jnp.minimum(kj, qi), hi))
    vec = lambda x: x.reshape(1, -1)
    stat = pltpu.VMEM((tq, 1), F32)
    acc = pltpu.VMEM((tq, LANES), F32)
    o = pl.pallas_call(
        functools.partial(_flash_body, lam_init, tq),
        grid=(b, DA_HEADS, nq, nq),
        in_specs=[qblk, kblk, kblk] + [_full((1, DA_HEAD))] * 4 + [_full((1, DA_VDIM))],
        out_specs=qblk,
        out_shape=jax.ShapeDtypeStruct((b, t, DA_WIDTH), F32),
        scratch_shapes=[stat, stat, stat, stat, acc, acc],
        compiler_params=_cparams("parallel", "parallel", "parallel", "arbitrary"))(
            shp(q), shp(k), shp(v), vec(p['da_lq1']), vec(p['da_lk1']), vec(p['da_lq2']), vec(p['da_lk2']),
            vec(p['da_subln_w']))
    return o.reshape(b * t, DA_WIDTH)


def _decode_body(lam_init, pt_ref, q_ref, kn_ref, vn_ref, ck_ref, cv_ref, lq1, lk1, lq2, lk2, sw_ref,
                 o_ref, qbd, m_s, l_s, acc):
    j = pl.program_id(1)
    nj = pl.num_programs(1)
    nrow = 2 * DA_HEADS

    @pl.when(j == 0)
    def _():
        rid = lax.broadcasted_iota(jnp.int32, (nrow, DA_WIDTH), 0)
        grp = lax.broadcasted_iota(jnp.int32, (nrow, DA_WIDTH), 1) // DA_HEAD
        qm = jnp.where(rid == grp, q_ref[0] * DA_SCALE, 0.0)
        qbd[...] = qm
        m_s[...] = jnp.sum(qm * kn_ref[0], axis=-1, keepdims=True)
        l_s[...] = jnp.ones_like(l_s)
        acc[...] = jnp.broadcast_to(vn_ref[0], acc.shape)

    s = _dot_nt(qbd[...], ck_ref[0])
    m_old = m_s[...]
    m_new = jnp.maximum(m_old, jnp.max(s, axis=-1, keepdims=True))
    alpha = jnp.exp(m_old - m_new)
    pr = jnp.exp(s - m_new)
    l_s[...] = l_s[...] * alpha + jnp.sum(pr, axis=-1, keepdims=True)
    acc[...] = acc[...] * alpha + _dot(pr, cv_ref[0])
    m_s[...] = m_new

    @pl.when(j == nj - 1)
    def _():
        lam = _lambda(lq1[...], lk1[...], lq2[...], lk2[...], lam_init)
        on = acc[...] / l_s[...]
        rid = lax.broadcasted_iota(jnp.int32, (nrow, LANES), 0)
        for hd in range(DA_HEADS):
            blk = on[:, hd * DA_VDIM:(hd + 1) * DA_VDIM]
            coef = jnp.where(rid == 2 * hd, 1.0, jnp.where(rid == 2 * hd + 1, -lam, 0.0))
            oh = jnp.sum(blk * coef, axis=0, keepdims=True)
            o_ref[0, :, hd * DA_VDIM:(hd + 1) * DA_VDIM] = (
                _rms(oh, sw_ref[...], DA_SUBLN_EPS) * (1.0 - lam_init))


def _decode(q, k_new, v_new, cache_k, cache_v, page_table, p, lam_init):
    b = q.shape[0]
    n_pages = page_table.shape[1]
    n_pool = cache_k.shape[0]
    row = pl.BlockSpec((1, 1, DA_WIDTH), lambda bi, j, pt: (bi, 0, 0))
    page = pl.BlockSpec((1, PAGE_SIZE, DA_WIDTH), lambda bi, j, pt: (pt[bi, j], 0, 0))
    vec = lambda x: x.reshape(1, -1)
    small = lambda n: pl.BlockSpec((1, n), lambda bi, j, pt: (0, 0))
    nrow = 2 * DA_HEADS
    o = pl.pallas_call(
        functools.partial(_decode_body, lam_init),
        grid_spec=pltpu.PrefetchScalarGridSpec(
            num_scalar_prefetch=1, grid=(b, n_pages),
            in_specs=[row, row, row, page, page] + [small(DA_HEAD)] * 4 + [small(DA_VDIM)],
            out_specs=row,
            scratch_shapes=[pltpu.VMEM((nrow, DA_WIDTH), F32), pltpu.VMEM((nrow, 1), F32),
                            pltpu.VMEM((nrow, 1), F32), pltpu.VMEM((nrow, DA_WIDTH), F32)]),
        out_shape=jax.ShapeDtypeStruct((b, 1, DA_WIDTH), F32),
        compiler_params=_cparams("parallel", "arbitrary"))(
            page_table, q.reshape(b, 1, DA_WIDTH), k_new.reshape(b, 1, DA_WIDTH), v_new.reshape(b, 1, DA_WIDTH),
            cache_k.reshape(n_pool, PAGE_SIZE, DA_WIDTH), cache_v.reshape(n_pool, PAGE_SIZE, DA_WIDTH),
            vec(p['da_lq1']), vec(p['da_lk1']), vec(p['da_lq2']), vec(p['da_lk2']), vec(p['da_subln_w']))
    return o.reshape(b, DA_WIDTH)


def _cf_epilogue(ys, rows, fulls):
    y = ys[0]
    u = y[:, :CF_WIDTH] * _sigmoid(y[:, CF_WIDTH:2 * CF_WIDTH])
    return u, y[:, 2 * CF_WIDTH:]


def _cf_conv_body(tc, u_ref, up_ref, cw_ref, cb_ref, lw_ref, lb_ref, o_ref, buf):
    c = pl.program_id(1)
    buf[0:CF_HIST, :] = jnp.where(c == 0, 0.0, up_ref[0])
    buf[CF_HIST:CF_HIST + tc, :] = u_ref[0]
    off = CF_HIST - (CF_KERNEL - 1)
    acc = cb_ref[...] + cw_ref[0:1, :] * buf[off:off + tc, :]
    for j in range(1, CF_KERNEL):
        acc = acc + cw_ref[j:j + 1, :] * buf[off + j:off + j + tc, :]
    mu = jnp.mean(acc, axis=-1, keepdims=True)
    xc = acc - mu
    var = jnp.mean(xc * xc, axis=-1, keepdims=True)
    o_ref[0] = _silu(xc * lax.rsqrt(var + CF_LN_EPS) * lw_ref[...] + lb_ref[...])


def _cf_conv(u, b, t, p):
    tc = min(t, 256)
    per = tc // CF_HIST
    blk = pl.BlockSpec((1, tc, CF_WIDTH), lambda bi, ci: (bi, ci, 0))
    prev = pl.BlockSpec((1, CF_HIST, CF_WIDTH), lambda bi, ci: (bi, jnp.maximum(ci * per - 1, 0), 0))
    vec = lambda x: x.reshape(1, CF_WIDTH)
    u3 = u.reshape(b, t, CF_WIDTH)
    o = pl.pallas_call(
        functools.partial(_cf_conv_body, tc),
        grid=(b, t // tc),
        in_specs=[blk, prev, _full((CF_KERNEL, CF_WIDTH))] + [_full((1, CF_WIDTH))] * 3,
        out_specs=blk,
        out_shape=jax.ShapeDtypeStruct((b, t, CF_WIDTH), F32),
        scratch_shapes=[pltpu.VMEM((CF_HIST + tc, CF_WIDTH), F32)],
        compiler_params=_cparams("parallel", "parallel"))(
            u3, u3, p['cf_conv_w'], vec(p['cf_conv_b']), vec(p['cf_ln_w']), vec(p['cf_ln_b']))
    return o.reshape(b * t, CF_WIDTH)


def _cf_step_body(u_ref, st_ref, cw_ref, cb_ref, lw_ref, lb_ref, o_ref):
    nb = u_ref.shape[0]
    cw = cw_ref[...]
    hist = jnp.concatenate([jnp.sum(st_ref[i] * cw[:CF_KERNEL - 1, :], axis=0, keepdims=True)
                            for i in range(nb)], axis=0)
    acc = cb_ref[...] + hist + cw[CF_KERNEL - 1:CF_KERNEL, :] * u_ref[...]
    mu = jnp.mean(acc, axis=-1, keepdims=True)
    xc = acc - mu
    var = jnp.mean(xc * xc, axis=-1, keepdims=True)
    o_ref[...] = _silu(xc * lax.rsqrt(var + CF_LN_EPS) * lw_ref[...] + lb_ref[...])


def _cf_step(u, state, p):
    b = u.shape[0]
    tb = SUBLANES
    vec = lambda x: x.reshape(1, CF_WIDTH)
    return pl.pallas_call(
        _cf_step_body,
        grid=(b // tb,),
        in_specs=[pl.BlockSpec((tb, CF_WIDTH), lambda i: (i, 0)),
                  pl.BlockSpec((tb, CF_KERNEL - 1, CF_WIDTH), lambda i: (i, 0, 0)),
                  _full((CF_KERNEL, CF_WIDTH))] + [_full((1, CF_WIDTH))] * 3,
        out_specs=pl.BlockSpec((tb, CF_WIDTH), lambda i: (i, 0)),
        out_shape=jax.ShapeDtypeStruct((b, CF_WIDTH), F32),
        compiler_params=_cparams("parallel"))(
            u, state, p['cf_conv_w'], vec(p['cf_conv_b']), vec(p['cf_ln_w']), vec(p['cf_ln_b']))


def _forward(x_prompt, x_sample, state_rwkv_shift, state_rwkv_wkv, state_mamba_conv, state_mamba_ssm,
             cache_k, cache_v, page_table, state_conformer_conv, p):
    bp, tp, d = x_prompt.shape
    bs = x_sample.shape[0]
    hp = x_prompt.reshape(bp * tp, d)
    hs = x_sample.reshape(bs, d)
    npre, npost = p['norm_pre'], p['norm_post']

    (r, k, v, g, a, lw), rw_shift_p = _rwkv_proj_seq(hp, bp, tp, npre[0], p)
    o, rw_wkv_p = _rwkv_chunk(r, k, v, a, lw, bp, tp, p)
    hp = _out_proj(o, g, hp, p['rw_w_o'], npost[0])
    (r, k, v, g, a, lw), rw_shift_s = _rwkv_proj_step(hs, state_rwkv_shift, npre[0], p)
    o, rw_wkv_s = _rwkv_step(r, k, v, a, lw, state_rwkv_wkv, p)
    hs = _out_proj(o, g, hs, p['rw_w_o'], npost[0])

    mw = _mamba_weights(p)
    mv = _mamba_vecs(p)
    ident = lambda ys, rows, fulls: ys
    widths = (MB_INNER, MB_CONV_DIM, LANES)
    z, xbc, dtr = _in_proj(hp, npre[1], mw, ident, widths, tm=256)
    yg, mb_ssm_p = _mamba_chunk(z, xbc, dtr, bp, tp, mv)
    mb_conv_p = xbc.reshape(bp, tp, MB_CONV_DIM)[:, tp - (MB_CONV - 1):]
    hp = _out_proj(yg, None, hp, p['mb_w_out'], npost[1])
    z, xbc, dtr = _in_proj(hs, npre[1], mw, ident, widths)
    yg, mb_ssm_s = _mamba_step(z, xbc, dtr, state_mamba_conv, state_mamba_ssm, mv)
    mb_conv_s = jnp.concatenate([state_mamba_conv[:, 1:], xbc[:, None, :]], axis=1)
    hs = _out_proj(yg, None, hs, p['mb_w_out'], npost[1])

    lam_init = 0.8 - 0.6 * math.exp(-0.3 * 2)
    tabs_p = _rope_tables(jnp.arange(tp, dtype=jnp.int32))
    tm = min(_row_tile(tp), 256)
    tps = tp // tm
    rows_p = [(tb, lambda i: (i % tps, 0), LANES) for tb in tabs_p]
    q, k, v, g = _in_proj(hp, npre[2], [p['da_w_qkvg']], _da_epilogue, (DA_WIDTH,) * 4, row_extras=rows_p, tm=tm)
    k_rows_p = k.reshape(bp, tp, DA_HEADS, 2 * DA_HEAD)
    v_rows_p = v.reshape(bp, tp, DA_HEADS, DA_VDIM)
    o = _flash(q, k, v, bp, tp, p, lam_init)
    hp = _out_proj(o, g, hp, p['da_w_o'], npost[2])
    tabs_s = _rope_tables(jnp.full((bs,), PAST_LEN, dtype=jnp.int32))
    rows_s = [(tb, lambda i: (i, 0), LANES) for tb in tabs_s]
    q, k, v, g = _in_proj(hs, npre[2], [p['da_w_qkvg']], _da_epilogue, (DA_WIDTH,) * 4, row_extras=rows_s)
    k_rows_s = k.reshape(bs, 1, DA_HEADS, 2 * DA_HEAD)
    v_rows_s = v.reshape(bs, 1, DA_HEADS, DA_VDIM)
    o = _decode(q, k, v, cache_k, cache_v, page_table, p, lam_init)
    hs = _out_proj(o, g, hs, p['da_w_o'], npost[2])

    u, g = _in_proj(hp, npre[3], [p['cf_w_in']], _cf_epilogue, (CF_WIDTH,) * 2)
    cf_conv_p = u.reshape(bp, tp, CF_WIDTH)[:, tp - (CF_KERNEL - 1):]
    c = _cf_conv(u, bp, tp, p)
    hp = _out_proj(c, g, hp, p['cf_w_out'], npost[3])
    u, g = _in_proj(hs, npre[3], [p['cf_w_in']], _cf_epilogue, (CF_WIDTH,) * 2)
    cf_conv_s = jnp.concatenate([state_conformer_conv[:, 1:], u[:, None, :]], axis=1)
    c = _cf_step(u, state_conformer_conv, p)
    hs = _out_proj(c, g, hs, p['cf_w_out'], npost[3])

    return (hp.reshape(bp, tp, d), hs.reshape(bs, 1, d), rw_shift_p, rw_shift_s, rw_wkv_p, rw_wkv_s,
            mb_conv_p, mb_conv_s, mb_ssm_p, mb_ssm_s, k_rows_p, k_rows_s, v_rows_p, v_rows_s,
            cf_conv_p, cf_conv_s)


def kernel(x_prompt, x_sample, state_rwkv_shift, state_rwkv_wkv, state_mamba_conv, state_mamba_ssm, cache_k, cache_v, page_table, state_conformer_conv, norm_pre, norm_post, rw_mu, rw_w_r, rw_w_k, rw_w_v, rw_w_g, rw_w0, rw_w_w1, rw_w_w2, rw_a0, rw_w_a1, rw_w_a2, rw_k_k, rw_k_a, rw_r_k, rw_gn_w, rw_gn_b, rw_w_o, mb_w_in, mb_conv_w, mb_conv_b, mb_dt_bias, mb_a_log, mb_d, mb_norm_w, mb_w_out, da_w_qkvg, da_lq1, da_lk1, da_lq2, da_lk2, da_subln_w, da_w_o, cf_w_in, cf_conv_w, cf_conv_b, cf_ln_w, cf_ln_b, cf_w_out):
    p = dict(norm_pre=norm_pre, norm_post=norm_post, rw_mu=rw_mu, rw_w_r=rw_w_r, rw_w_k=rw_w_k, rw_w_v=rw_w_v,
             rw_w_g=rw_w_g, rw_w0=rw_w0, rw_w_w1=rw_w_w1, rw_w_w2=rw_w_w2, rw_a0=rw_a0, rw_w_a1=rw_w_a1,
             rw_w_a2=rw_w_a2, rw_k_k=rw_k_k, rw_k_a=rw_k_a, rw_r_k=rw_r_k, rw_gn_w=rw_gn_w, rw_gn_b=rw_gn_b,
             rw_w_o=rw_w_o, mb_w_in=mb_w_in, mb_conv_w=mb_conv_w, mb_conv_b=mb_conv_b, mb_dt_bias=mb_dt_bias,
             mb_a_log=mb_a_log, mb_d=mb_d, mb_norm_w=mb_norm_w, mb_w_out=mb_w_out, da_w_qkvg=da_w_qkvg,
             da_lq1=da_lq1, da_lk1=da_lk1, da_lq2=da_lq2, da_lk2=da_lk2, da_subln_w=da_subln_w, da_w_o=da_w_o,
             cf_w_in=cf_w_in, cf_conv_w=cf_conv_w, cf_conv_b=cf_conv_b, cf_ln_w=cf_ln_w, cf_ln_b=cf_ln_b,
             cf_w_out=cf_w_out)
    return _forward(x_prompt, x_sample, state_rwkv_shift, state_rwkv_wkv, state_mamba_conv, state_mamba_ssm,
                    cache_k, cache_v, page_table, state_conformer_conv, p)
```

```python
import functools
import math

import jax
import jax.numpy as jnp
from jax import lax
from jax.experimental import pallas as pl
from jax.experimental.pallas import tpu as pltpu

F32 = jnp.float32
BF16 = jnp.bfloat16

D_MODEL = 1024
PAST_LEN = 8192
PAGE_SIZE = 128
RMS_EPS = 1e-6

RW_HEAD = 64
RW_HEADS = D_MODEL // RW_HEAD
RW_LORA = 64
RW_GN_EPS = 64e-5
RW_CHUNK = 64
RW_GROUP = 4

MB_INNER = 2 * D_MODEL
MB_HEADDIM = 64
MB_HEADS = MB_INNER // MB_HEADDIM
MB_GROUPS = 4
MB_STATE = 128
MB_CONV = 4
MB_CHUNK = 128
MB_CONV_DIM = MB_INNER + 2 * MB_GROUPS * MB_STATE
MB_NORM_EPS = 1e-5
MB_GROUP_WIDTH = MB_INNER // MB_GROUPS

DA_HEADS = 8
DA_HEAD = 64
DA_VDIM = 2 * DA_HEAD
DA_WIDTH = DA_HEADS * DA_VDIM
DA_SCALE = DA_HEAD ** -0.5
DA_SUBLN_EPS = 1e-5
ROPE_THETA = 500000.0
ROPE_DIM = DA_HEAD // 4
DEC_PAGES = 4

CF_WIDTH = D_MODEL
CF_KERNEL = 31
CF_LN_EPS = 1e-5
CF_HIST = 32

LANES = 128
SUBLANES = 8
VMEM_LIMIT = 56 * 1024 * 1024

NEG_INF = float("-inf")


def _cparams(*sem):
    return pltpu.CompilerParams(dimension_semantics=sem, vmem_limit_bytes=VMEM_LIMIT)


def _dot(a, b):
    return jnp.dot(a.astype(BF16), b.astype(BF16), preferred_element_type=F32)


def _dot_nt(a, b):
    return lax.dot_general(a.astype(BF16), b.astype(BF16), (((1,), (1,)), ((), ())),
                           preferred_element_type=F32)


def _dot_tn(a, b):
    return lax.dot_general(a.astype(BF16), b.astype(BF16), (((0,), (0,)), ((), ())),
                           preferred_element_type=F32)


def _split2(a):
    hi = a.astype(BF16)
    lo = (a - hi.astype(F32)).astype(BF16)
    return hi, lo


def _split3(a):
    hi = a.astype(BF16)
    r1 = a - hi.astype(F32)
    mid = r1.astype(BF16)
    lo = (r1 - mid.astype(F32)).astype(BF16)
    return hi, mid, lo


def _dotx(a, b, dn=(((1,), (0,)), ((), ()))):
    ah, al = _split2(a)
    bh, bl = _split2(b)
    f = functools.partial(lax.dot_general, dimension_numbers=dn, preferred_element_type=F32)
    return f(ah, bh) + (f(ah, bl) + f(al, bh))


_NT = (((1,), (1,)), ((), ()))
_TN = (((0,), (0,)), ((), ()))


def _cumsum_rows(tri_bf16, x):
    hi, mid, lo = _split3(x)
    f = functools.partial(jnp.dot, preferred_element_type=F32)
    return f(tri_bf16, hi) + (f(tri_bf16, mid) + f(tri_bf16, lo))


def _sigmoid(x):
    return 1.0 / (1.0 + jnp.exp(-x))


def _silu(x):
    return x * _sigmoid(x)


def _softplus(x):
    return jnp.maximum(x, 0.0) + jnp.log(1.0 + jnp.exp(-jnp.abs(x)))


def _rms(x, w, eps):
    return x * lax.rsqrt(jnp.mean(x * x, axis=-1, keepdims=True) + eps) * w


def _row_tile(m):
    for t in (512, 256, 128, 64, 32, 16, 8):
        if m % t == 0:
            return t
    return m


def _full(shape):
    nd = len(shape)
    return pl.BlockSpec(shape, lambda *_: (0,) * nd)


def _out_proj_body(has_gate, *refs):
    if has_gate:
        y_ref, g_ref, h_ref, w_ref, pw_ref, o_ref = refs
        y = y_ref[...] * _silu(g_ref[...])
    else:
        y_ref, h_ref, w_ref, pw_ref, o_ref = refs
        y = y_ref[...]
    z = _dot(y, w_ref[...])
    o_ref[...] = h_ref[...] + _rms(z, pw_ref[...], RMS_EPS)


def _out_proj(y, gate, h, w, post_w):
    m, kdim = y.shape
    tm = _row_tile(m)
    row = lambda width: pl.BlockSpec((tm, width), lambda i: (i, 0))
    args, specs = [y], [row(kdim)]
    if gate is not None:
        args.append(gate)
        specs.append(row(D_MODEL))
    args += [h, w.astype(BF16), post_w.reshape(1, D_MODEL)]
    specs += [row(D_MODEL), _full((kdim, D_MODEL)), _full((1, D_MODEL))]
    return pl.pallas_call(
        functools.partial(_out_proj_body, gate is not None),
        grid=(m // tm,), in_specs=specs, out_specs=row(D_MODEL),
        out_shape=jax.ShapeDtypeStruct((m, D_MODEL), F32),
        compiler_params=_cparams("parallel"))(*args)


def _in_proj_body(n_w, n_row, n_full, epilogue, *refs):
    h_ref, nw_ref = refs[0], refs[1]
    w_refs = refs[2:2 + n_w]
    row_refs = refs[2 + n_w:2 + n_w + n_row]
    full_refs = refs[2 + n_w + n_row:2 + n_w + n_row + n_full]
    out_refs = refs[2 + n_w + n_row + n_full:]
    xn = _rms(h_ref[...], nw_ref[...], RMS_EPS).astype(BF16)
    ys = [jnp.dot(xn, w[...], preferred_element_type=F32) for w in w_refs]
    outs = epilogue(ys, [r[...] for r in row_refs], [r[...] for r in full_refs])
    for o_ref, o in zip(out_refs, outs):
        o_ref[...] = o


def _in_proj(h, norm_w, weights, epilogue, out_widths, row_extras=(), full_extras=(), tm=None):
    m = h.shape[0]
    tm = tm or _row_tile(m)
    row = lambda width: pl.BlockSpec((tm, width), lambda i: (i, 0))
    args = [h, norm_w.reshape(1, D_MODEL)] + [w.astype(BF16) for w in weights]
    specs = [row(D_MODEL), _full((1, D_MODEL))] + [_full(w.shape) for w in weights]
    for arr, imap, width in row_extras:
        args.append(arr)
        specs.append(pl.BlockSpec((tm, width), imap))
    for arr in full_extras:
        args.append(arr)
        specs.append(_full(arr.shape))
    return pl.pallas_call(
        functools.partial(_in_proj_body, len(weights), len(row_extras), len(full_extras), epilogue),
        grid=(m // tm,), in_specs=specs,
        out_specs=[row(wd) for wd in out_widths],
        out_shape=[jax.ShapeDtypeStruct((m, wd), F32) for wd in out_widths],
        compiler_params=_cparams("parallel"))(*args)


def _rwkv_mix(xn, xprev, mu, wr, wk, wv, wg, w0, ww1, ww2, a0, wa1, wa2):
    xx = xprev - xn
    xm = [xn + xx * mu[i:i + 1, :] for i in range(6)]
    r = _dot(xm[0], wr)
    k = _dot(xm[2], wk)
    v = _dot(xm[3], wv)
    g = _dot(xm[5], wg)
    zw = w0 + _dot(jnp.tanh(_dot(xm[1], ww1)), ww2)
    w_log = -_softplus(-zw) - 0.5
    lw = -jnp.exp(w_log)
    a = _sigmoid(a0 + _dot(_dot(xm[4], wa1), wa2))
    return r, k, v, g, a, lw


def _rwkv_proj_seq_body(tiles_per_seq, h_ref, hp_ref, nw_ref, mu_ref, wr, wk, wv, wg, w0, ww1, ww2,
                        a0, wa1, wa2, r_ref, k_ref, v_ref, g_ref, a_ref, lw_ref, last_ref):
    i = pl.program_id(0)
    nw = nw_ref[...]
    xn = _rms(h_ref[...], nw, RMS_EPS)
    tm = xn.shape[0]
    pn = _rms(hp_ref[...], nw, RMS_EPS)[SUBLANES - 1:SUBLANES, :]
    first = (i % tiles_per_seq) == 0
    prow = jnp.where(first, 0.0, pn)
    rows = lax.broadcasted_iota(jnp.int32, xn.shape, 0)
    xprev = jnp.where(rows == 0, prow, pltpu.roll(xn, 1, axis=0))
    outs = _rwkv_mix(xn, xprev, mu_ref[...], wr[...], wk[...], wv[...], wg[...], w0[...], ww1[...],
                     ww2[...], a0[...], wa1[...], wa2[...])
    for o_ref, o in zip((r_ref, k_ref, v_ref, g_ref, a_ref, lw_ref), outs):
        o_ref[...] = o
    last_ref[0] = xn[tm - 1:tm, :]


def _rwkv_proj_step_body(h_ref, sh_ref, nw_ref, mu_ref, wr, wk, wv, wg, w0, ww1, ww2,
                         a0, wa1, wa2, r_ref, k_ref, v_ref, g_ref, a_ref, lw_ref, last_ref):
    xn = _rms(h_ref[...], nw_ref[...], RMS_EPS)
    outs = _rwkv_mix(xn, sh_ref[...], mu_ref[...], wr[...], wk[...], wv[...], wg[...], w0[...], ww1[...],
                     ww2[...], a0[...], wa1[...], wa2[...])
    for o_ref, o in zip((r_ref, k_ref, v_ref, g_ref, a_ref, lw_ref), outs):
        o_ref[...] = o
    last_ref[...] = xn


def _rwkv_weights(p):
    d = D_MODEL
    vec = lambda x: x.reshape(1, d)
    args = [p['rw_mu'], p['rw_w_r'].astype(BF16), p['rw_w_k'].astype(BF16), p['rw_w_v'].astype(BF16),
            p['rw_w_g'].astype(BF16), vec(p['rw_w0']), p['rw_w_w1'].astype(BF16), p['rw_w_w2'].astype(BF16),
            vec(p['rw_a0']), p['rw_w_a1'].astype(BF16), p['rw_w_a2'].astype(BF16)]
    return args, [_full(a.shape) for a in args]


def _rwkv_proj_seq(h2, b, t, norm_w, p):
    m = b * t
    tm = min(_row_tile(t), 256)
    tps = t // tm
    row = pl.BlockSpec((tm, D_MODEL), lambda i: (i, 0))
    prev = pl.BlockSpec((SUBLANES, D_MODEL), lambda i: (jnp.maximum(i * (tm // SUBLANES) - 1, 0), 0))
    wargs, wspecs = _rwkv_weights(p)
    outs = pl.pallas_call(
        functools.partial(_rwkv_proj_seq_body, tps),
        grid=(m // tm,),
        in_specs=[row, prev, _full((1, D_MODEL))] + wspecs,
        out_specs=[row] * 6 + [pl.BlockSpec((1, 1, D_MODEL), lambda i: (i // tps, 0, 0))],
        out_shape=[jax.ShapeDtypeStruct((m, D_MODEL), F32)] * 6
        + [jax.ShapeDtypeStruct((b, 1, D_MODEL), F32)],
        compiler_params=_cparams("arbitrary"))(h2, h2, norm_w.reshape(1, D_MODEL), *wargs)
    return outs[:6], outs[6].reshape(b, D_MODEL)


def _rwkv_proj_step(h2, shift, norm_w, p):
    m = h2.shape[0]
    wargs, wspecs = _rwkv_weights(p)
    outs = pl.pallas_call(
        _rwkv_proj_step_body,
        grid=(1,),
        in_specs=[_full((m, D_MODEL)), _full((m, D_MODEL)), _full((1, D_MODEL))] + wspecs,
        out_specs=[_full((m, D_MODEL))] * 7,
        out_shape=[jax.ShapeDtypeStruct((m, D_MODEL), F32)] * 7,
        compiler_params=_cparams("arbitrary"))(h2, shift, norm_w.reshape(1, D_MODEL), *wargs)
    return outs[:6], outs[6]


def _pair_sum(x, m0):
    s0 = jnp.sum(jnp.where(m0, x, 0.0), axis=-1, keepdims=True)
    s1 = jnp.sum(jnp.where(m0, 0.0, x), axis=-1, keepdims=True)
    return jnp.where(m0, s0, s1)


def _rwkv_chunk_body(gsz, r_ref, k_ref, v_ref, a_ref, lw_ref, kk_ref, ka_ref, rk_ref, gw_ref, gb_ref,
                     o_ref, s_ref, bd_ref):
    cg = pl.program_id(2)
    ncg = pl.num_programs(2)
    cl = RW_CHUNK
    two = 2 * cl

    @pl.when(cg == 0)
    def _():
        bd_ref[...] = jnp.zeros_like(bd_ref)

    lane = lax.broadcasted_iota(jnp.int32, (cl, LANES), 1)
    m0 = lane < RW_HEAD
    ti = lax.broadcasted_iota(jnp.int32, (cl, cl), 0)
    tj = lax.broadcasted_iota(jnp.int32, (cl, cl), 1)
    tri = jnp.where(ti >= tj, 1.0, 0.0).astype(BF16)
    row2 = lax.broadcasted_iota(jnp.int32, (two, LANES), 0)
    col2 = lax.broadcasted_iota(jnp.int32, (two, LANES), 1)
    same = (row2 >= cl) == (col2 >= RW_HEAD)
    step_r = jnp.bitwise_and(row2, cl - 1)
    step_c = jnp.bitwise_and(col2, cl - 1)
    bd_strict = jnp.logical_and(same, step_r > step_c)
    bd_incl = jnp.logical_and(same, step_r >= step_c)
    eye = row2 == col2
    kk_w, ka_w, rk_w = kk_ref[...], ka_ref[...], rk_ref[...]

    def stack(x):
        return jnp.where(same, jnp.concatenate([x, x], axis=0), 0.0)

    chunks = range(gsz)
    st = []
    for gi in chunks:
        sl = slice(gi * cl, (gi + 1) * cl)
        r = r_ref[0, sl, :]
        k = k_ref[0, sl, :]
        v = v_ref[0, sl, :]
        a = a_ref[0, sl, :]
        lw = lw_ref[0, sl, :]
        kkv = k * kk_w
        kap = kkv * lax.rsqrt(jnp.maximum(_pair_sum(kkv * kkv, m0), 1e-24))
        kmod = k * (1.0 + (a - 1.0) * ka_w)
        st.append(dict(r=r, v=v, lw=lw, kap=kap, kmod=kmod, bvec=kap * a,
                       bonus=_pair_sum(r * kmod * rk_w, m0) * v))
    for s in st:
        s['cum'] = _cumsum_rows(tri, s['lw'])
    for s in st:
        cum = s['cum']
        cum_last = cum[cl - 1:cl, :]
        g_inv = jnp.exp(-cum)
        g_end = jnp.exp(cum_last - cum)
        s['g_all'] = jnp.exp(cum_last)
        s['kapm'] = stack(s['kap'] * jnp.exp(cum - s['lw']))
        s['rm'] = stack(s['r'] * jnp.exp(cum))
        s['v2m'] = stack(s['v'])
        s['b_h'] = s['bvec'] * g_inv
        s['k_h'] = s['kmod'] * g_inv
        s['b_e'] = stack(s['bvec'] * g_end)
        s['k_e'] = stack(s['kmod'] * g_end)
    for s in st:
        lhs = jnp.concatenate([s['kapm'], s['rm']], axis=0)
        gb = _dot_nt(lhs, jnp.concatenate([s['b_h'], s['b_h']], axis=0))
        gk = _dot_nt(lhs, jnp.concatenate([s['k_h'], s['k_h']], axis=0))
        s['a_ab'] = jnp.where(bd_strict, gb[:two], 0.0)
        s['a_rb'] = jnp.where(bd_incl, gb[two:], 0.0)
        s['a_ak'] = jnp.where(bd_strict, gk[:two], 0.0)
        s['a_rk'] = jnp.where(bd_incl, gk[two:], 0.0)
    for s in st:
        s['tinv'] = jnp.where(eye, 1.0, 0.0) - s['a_ab']
        s['lp'] = _dot(s['a_ab'], s['a_ab'])
        s['x2m'] = _dot(s['a_ak'], s['v2m'])
    for _ in range(4):
        for s in st:
            both = _dot(jnp.concatenate([s['tinv'], s['lp']], axis=0), s['lp'])
            s['tinv'] = s['tinv'] + both[:two]
            s['lp'] = both[two:]
    for s in st:
        s['tinv'] = s['tinv'] + _dot(s['tinv'], s['lp'])
    for s in st:
        pq = _dot(s['tinv'], jnp.concatenate([s['kapm'], s['x2m']], axis=1))
        s['pm'] = -pq[:, :LANES]
        s['qm'] = -pq[:, LANES:]
    for s in st:
        ro = _dot(s['a_rb'], jnp.concatenate([s['pm'], s['qm']], axis=1))
        rtm = s['rm'] + ro[:, :LANES]
        oim = ro[:, LANES:] + _dot(s['a_rk'], s['v2m'])
        s['rt'] = rtm[:cl] + rtm[cl:]
        s['oi'] = oim[:cl] + oim[cl:]
    for s in st:
        s['corr'] = _dotx(s['pm'], s['b_e'], _TN)
        s['nmat'] = _dot_tn(jnp.concatenate([s['qm'], s['v2m']], axis=0),
                            jnp.concatenate([s['b_e'], s['k_e']], axis=0))
    pre = [(s['rt'], s['oi'], s['corr'], s['nmat'], s['g_all'], s['bonus']) for s in st]

    bd = bd_ref[...]
    for gi, (rt, oi, corr, nmat, g_all, bonus) in enumerate(pre):
        o = _dot_nt(rt, bd) + oi
        bd = bd * g_all + _dotx(bd, corr) + nmat
        mean = _pair_sum(o, m0) * (1.0 / RW_HEAD)
        oc = o - mean
        var = _pair_sum(oc * oc, m0) * (1.0 / RW_HEAD)
        o_ref[0, gi * cl:(gi + 1) * cl, :] = (oc * lax.rsqrt(var + RW_GN_EPS) * gw_ref[...] + gb_ref[...]
                                              + bonus)
    bd_ref[...] = bd

    @pl.when(cg == ncg - 1)
    def _():
        s_ref[0, 0] = bd[:RW_HEAD, :RW_HEAD]
        s_ref[0, 1] = bd[RW_HEAD:, RW_HEAD:]


def _rwkv_chunk(r, k, v, a, lw, b, t, p):
    hp = RW_HEADS // 2
    gsz = RW_GROUP if t % (RW_GROUP * RW_CHUNK) == 0 else 1
    rows = gsz * RW_CHUNK
    shp = lambda x: x.reshape(b, t, D_MODEL)
    blk = pl.BlockSpec((1, rows, LANES), lambda bi, hi, ci: (bi, ci, hi))
    vec = pl.BlockSpec((1, LANES), lambda bi, hi, ci: (0, hi))
    vrow = lambda x: x.reshape(1, D_MODEL)
    o, s = pl.pallas_call(
        functools.partial(_rwkv_chunk_body, gsz),
        grid=(b, hp, t // rows),
        in_specs=[blk] * 5 + [vec] * 5,
        out_specs=[blk, pl.BlockSpec((1, 2, RW_HEAD, RW_HEAD), lambda bi, hi, ci: (bi, hi, 0, 0))],
        out_shape=[jax.ShapeDtypeStruct((b, t, D_MODEL), F32),
                   jax.ShapeDtypeStruct((b, RW_HEADS, RW_HEAD, RW_HEAD), F32)],
        scratch_shapes=[pltpu.VMEM((LANES, LANES), F32)],
        compiler_params=_cparams("parallel", "parallel", "arbitrary"))(
            shp(r), shp(k), shp(v), shp(a), shp(lw), vrow(p['rw_k_k']), vrow(p['rw_k_a']),
            vrow(p['rw_r_k']), vrow(p['rw_gn_w']), vrow(p['rw_gn_b']))
    return o.reshape(b * t, D_MODEL), s


def _rwkv_step_body(r_ref, k_ref, vc_ref, a_ref, lw_ref, s_ref, kk_ref, ka_ref, rk_ref, gw_ref, gb_ref,
                    o_ref, so_ref):
    r = r_ref[0]
    k = k_ref[0]
    a = a_ref[0]
    lw = lw_ref[0]
    vc = vc_ref[0]
    s = s_ref[0]
    kkv = k * kk_ref[...]
    kap = kkv * lax.rsqrt(jnp.maximum(jnp.sum(kkv * kkv, axis=-1, keepdims=True), 1e-24))
    kmod = k * (1.0 + (a - 1.0) * ka_ref[...])
    bvec = kap * a
    dec = jnp.exp(lw)
    sa = jnp.sum(s * kap, axis=-1, keepdims=True)
    s_new = s * dec - sa * bvec + vc * kmod
    so_ref[0] = s_new
    o = jnp.sum(s_new * r, axis=-1, keepdims=True)
    mean = jnp.mean(o, axis=1, keepdims=True)
    oc = o - mean
    var = jnp.mean(oc * oc, axis=1, keepdims=True)
    on = oc * lax.rsqrt(var + RW_GN_EPS) * gw_ref[...] + gb_ref[...]
    bonus = jnp.sum(r * kmod * rk_ref[...], axis=-1, keepdims=True)
    o_ref[0] = on + bonus * vc


def _rwkv_step(r, k, v, a, lw, state, p):
    b = r.shape[0]
    hrow = (RW_HEADS, 1, RW_HEAD)
    hcol = (RW_HEADS, RW_HEAD, 1)
    rows = lambda x: x.reshape((b,) + hrow)
    rblk = pl.BlockSpec((1,) + hrow, lambda i: (i, 0, 0, 0))
    cblk = pl.BlockSpec((1,) + hcol, lambda i: (i, 0, 0, 0))
    sblk = pl.BlockSpec((1, RW_HEADS, RW_HEAD, RW_HEAD), lambda i: (i, 0, 0, 0))
    o, s = pl.pallas_call(
        _rwkv_step_body,
        grid=(b,),
        in_specs=[rblk, rblk, cblk, rblk, rblk, sblk] + [_full(hrow)] * 3 + [_full(hcol)] * 2,
        out_specs=[cblk, sblk],
        out_shape=[jax.ShapeDtypeStruct((b,) + hcol, F32),
                   jax.ShapeDtypeStruct((b, RW_HEADS, RW_HEAD, RW_HEAD), F32)],
        compiler_params=_cparams("parallel"))(
            rows(r), rows(k), v.reshape((b,) + hcol), rows(a), rows(lw), state,
            p['rw_k_k'].reshape(hrow), p['rw_k_a'].reshape(hrow), p['rw_r_k'].reshape(hrow),
            p['rw_gn_w'].reshape(hcol), p['rw_gn_b'].reshape(hcol))
    return o.reshape(b, D_MODEL), s


def _mamba_weights(p):
    w = p['mb_w_in']
    wz = w[:, :MB_INNER]
    wx = w[:, MB_INNER:MB_INNER + MB_CONV_DIM]
    wdt = jnp.pad(w[:, MB_INNER + MB_CONV_DIM:], ((0, 0), (0, LANES - MB_HEADS)))
    return [wz, wx, wdt]


def _mamba_vecs(p):
    pad = lambda x: jnp.pad(x.reshape(1, MB_HEADS), ((0, 0), (0, LANES - MB_HEADS)))
    return dict(conv_w=p['mb_conv_w'], conv_b=p['mb_conv_b'].reshape(1, MB_CONV_DIM),
                dt_bias=pad(p['mb_dt_bias']), a_log=pad(p['mb_a_log']),
                d_skip=jnp.repeat(p['mb_d'], MB_HEADDIM).reshape(1, MB_INNER),
                norm_w=p['mb_norm_w'].reshape(1, MB_INNER))


def _mamba_chunk_body(x_ref, xp_ref, z_ref, dt_ref, cw_ref, cb_ref, dtb_ref, alog_ref, dsk_ref, nw_ref,
                      y_ref, fs_ref, st_ref):
    c = pl.program_id(1)
    nc = pl.num_programs(1)
    cl = MB_CHUNK

    @pl.when(c == 0)
    def _():
        st_ref[...] = jnp.zeros_like(st_ref)

    hist = jnp.where(c == 0, 0.0, xp_ref[0])
    xcat = jnp.concatenate([hist, x_ref[0]], axis=0)
    cw = cw_ref[...]
    conv = cb_ref[...] + cw[0:1, :] * xcat[5:5 + cl]
    for j in range(1, MB_CONV):
        conv = conv + cw[j:j + 1, :] * xcat[5 + j:5 + j + cl]
    conv = _silu(conv)

    dt = _softplus(dt_ref[0] + dtb_ref[...])
    da = dt * (-jnp.exp(alog_ref[...]))
    ti = lax.broadcasted_iota(jnp.int32, (cl, cl), 0)
    tj = lax.broadcasted_iota(jnp.int32, (cl, cl), 1)
    causal = ti >= tj
    tri = jnp.where(causal, 1.0, 0.0).astype(BF16)
    a_cs = _cumsum_rows(tri, da)
    a_cs_t = a_cs.T
    a_last = a_cs[cl - 1:cl, :]
    dec_to_end = jnp.exp(a_last - a_cs)
    e_cs = jnp.exp(a_cs)
    e_last = jnp.exp(a_last)

    lane = lax.broadcasted_iota(jnp.int32, (cl, LANES), 1)
    m0 = lane < MB_HEADDIM
    row0 = lax.broadcasted_iota(jnp.int32, (LANES, LANES), 0) < MB_HEADDIM
    heads_per_group = MB_HEADS // MB_GROUPS
    pairs_per_group = heads_per_group // 2
    for g in range(MB_GROUPS):
        bm = conv[:, MB_INNER + g * MB_STATE:MB_INNER + (g + 1) * MB_STATE]
        cm = conv[:, MB_INNER + (MB_GROUPS + g) * MB_STATE:MB_INNER + (MB_GROUPS + g + 1) * MB_STATE]
        cb = _dot_nt(cm, bm)
        blocks = []
        for j in range(pairs_per_group):
            pi = g * pairs_per_group + j
            h0, h1 = 2 * pi, 2 * pi + 1
            lo, hi = pi * LANES, (pi + 1) * LANES
            xs = conv[:, lo:hi]
            dt2 = jnp.where(m0, dt[:, h0:h0 + 1], dt[:, h1:h1 + 1])
            xdt = xs * dt2
            l0 = jnp.exp(jnp.where(causal, a_cs[:, h0:h0 + 1] - a_cs_t[h0:h0 + 1, :], NEG_INF))
            l1 = jnp.exp(jnp.where(causal, a_cs[:, h1:h1 + 1] - a_cs_t[h1:h1 + 1, :], NEG_INF))
            lhs = jnp.concatenate([(cb * l0).astype(BF16), (cb * l1).astype(BF16)], axis=1)
            rhs = jnp.concatenate([jnp.where(m0, xdt, 0.0).astype(BF16),
                                   jnp.where(m0, 0.0, xdt).astype(BF16)], axis=0)
            y = jnp.dot(lhs, rhs, preferred_element_type=F32)
            st = st_ref[pi]
            e2 = jnp.where(m0, e_cs[:, h0:h0 + 1], e_cs[:, h1:h1 + 1])
            y = y + _dot_nt(cm, st) * e2
            d2 = jnp.where(m0, dec_to_end[:, h0:h0 + 1], dec_to_end[:, h1:h1 + 1])
            sc = jnp.where(row0, e_last[:, h0:h0 + 1], e_last[:, h1:h1 + 1])
            st_ref[pi] = st * sc + _dot_tn(xdt * d2, bm)
            y = y + dsk_ref[:, lo:hi] * xs
            blocks.append(y * _silu(z_ref[0, :, lo:hi]))
        ssq = blocks[0] * blocks[0]
        for blk in blocks[1:]:
            ssq = ssq + blk * blk
        scale = lax.rsqrt(jnp.sum(ssq, axis=-1, keepdims=True) * (1.0 / MB_GROUP_WIDTH) + MB_NORM_EPS)
        for j, blk in enumerate(blocks):
            lo = (g * pairs_per_group + j) * LANES
            y_ref[0, :, lo:lo + LANES] = blk * scale * nw_ref[:, lo:lo + LANES]

    @pl.when(c == nc - 1)
    def _():
        fs_ref[0] = st_ref[...]


def _mamba_chunk(z, xbc, dtr, b, t, vecs):
    nc = t // MB_CHUNK
    npairs = MB_HEADS // 2
    per = MB_CHUNK // SUBLANES
    y, fs = pl.pallas_call(
        _mamba_chunk_body,
        grid=(b, nc),
        in_specs=[pl.BlockSpec((1, MB_CHUNK, MB_CONV_DIM), lambda bi, ci: (bi, ci, 0)),
                  pl.BlockSpec((1, SUBLANES, MB_CONV_DIM),
                               lambda bi, ci: (bi, jnp.maximum(ci * per - 1, 0), 0)),
                  pl.BlockSpec((1, MB_CHUNK, MB_INNER), lambda bi, ci: (bi, ci, 0)),
                  pl.BlockSpec((1, MB_CHUNK, LANES), lambda bi, ci: (bi, ci, 0)),
                  _full((MB_CONV, MB_CONV_DIM)), _full((1, MB_CONV_DIM)), _full((1, LANES)),
                  _full((1, LANES)), _full((1, MB_INNER)), _full((1, MB_INNER))],
        out_specs=[pl.BlockSpec((1, MB_CHUNK, MB_INNER), lambda bi, ci: (bi, ci, 0)),
                   pl.BlockSpec((1, npairs, LANES, MB_STATE), lambda bi, ci: (bi, 0, 0, 0))],
        out_shape=[jax.ShapeDtypeStruct((b, t, MB_INNER), F32),
                   jax.ShapeDtypeStruct((b, npairs, LANES, MB_STATE), F32)],
        scratch_shapes=[pltpu.VMEM((npairs, LANES, MB_STATE), F32)],
        compiler_params=_cparams("parallel", "arbitrary"))(
            xbc.reshape(b, t, MB_CONV_DIM), xbc.reshape(b, t, MB_CONV_DIM), z.reshape(b, t, MB_INNER),
            dtr.reshape(b, t, LANES), vecs['conv_w'], vecs['conv_b'], vecs['dt_bias'], vecs['a_log'],
            vecs['d_skip'], vecs['norm_w'])
    return y.reshape(b * t, MB_INNER), fs.reshape(b, MB_HEADS, MB_HEADDIM, MB_STATE)


def _mamba_step_conv_body(x_ref, cs_ref, dt_ref, cw_ref, cb_ref, dtb_ref, alog_ref, conv_ref, dt_o, ed_o):
    cw = cw_ref[...]
    conv = cb_ref[...] + cw[MB_CONV - 1:MB_CONV, :] * x_ref[...]
    for j in range(MB_CONV - 1):
        conv = conv + cw[j:j + 1, :] * cs_ref[:, j, :]
    conv_ref[...] = _silu(conv)
    dt = _softplus(dt_ref[...] + dtb_ref[...])
    dt_o[...] = dt
    ed_o[...] = jnp.exp(dt * (-jnp.exp(alog_ref[...])))


def _mamba_step_state_body(s_ref, xc_ref, dt_ref, ed_ref, bm_ref, cm_ref, so_ref, y_ref):
    hpg = MB_HEADS // MB_GROUPS
    for g in range(MB_GROUPS):
        hs = slice(g * hpg, (g + 1) * hpg)
        s = s_ref[0, hs]
        xdt = xc_ref[0, hs] * dt_ref[0, hs]
        fin = s * ed_ref[0, hs] + xdt * bm_ref[0, g:g + 1]
        so_ref[0, hs] = fin
        y_ref[0, hs] = jnp.sum(fin * cm_ref[0, g:g + 1], axis=-1, keepdims=True)


def _mamba_step_out_body(y_ref, x_ref, z_ref, dsk_ref, nw_ref, o_ref):
    y = (y_ref[...] + dsk_ref[...] * x_ref[...]) * _silu(z_ref[...])
    for g in range(MB_GROUPS):
        lo, hi = g * MB_GROUP_WIDTH, (g + 1) * MB_GROUP_WIDTH
        o_ref[:, lo:hi] = _rms(y[:, lo:hi], nw_ref[:, lo:hi], MB_NORM_EPS)


def _mamba_step(z, xbc, dtr, conv_state, ssm_state, vecs):
    b = z.shape[0]
    conv, dt, ed = pl.pallas_call(
        _mamba_step_conv_body,
        grid=(1,),
        in_specs=[_full((b, MB_CONV_DIM)), _full((b, MB_CONV - 1, MB_CONV_DIM)), _full((b, LANES)),
                  _full((MB_CONV, MB_CONV_DIM)), _full((1, MB_CONV_DIM)), _full((1, LANES)), _full((1, LANES))],
        out_specs=[_full((b, MB_CONV_DIM)), _full((b, LANES)), _full((b, LANES))],
        out_shape=[jax.ShapeDtypeStruct((b, MB_CONV_DIM), F32), jax.ShapeDtypeStruct((b, LANES), F32),
                   jax.ShapeDtypeStruct((b, LANES), F32)],
        compiler_params=_cparams("arbitrary"))(
            xbc, conv_state, dtr, vecs['conv_w'], vecs['conv_b'], vecs['dt_bias'], vecs['a_log'])
    xs = conv[:, :MB_INNER]
    bm = conv[:, MB_INNER:MB_INNER + MB_GROUPS * MB_STATE].reshape(b, MB_GROUPS, MB_STATE)
    cm = conv[:, MB_INNER + MB_GROUPS * MB_STATE:].reshape(b, MB_GROUPS, MB_STATE)
    hp1 = (MB_HEADS, MB_HEADDIM, 1)
    h11 = (MB_HEADS, 1, 1)
    per = lambda shape: pl.BlockSpec((1,) + shape, lambda i: (i,) + (0,) * len(shape))
    st_shape = (MB_HEADS, MB_HEADDIM, MB_STATE)
    fin, y = pl.pallas_call(
        _mamba_step_state_body,
        grid=(b,),
        in_specs=[per(st_shape), per(hp1), per(h11), per(h11), per((MB_GROUPS, MB_STATE)),
                  per((MB_GROUPS, MB_STATE))],
        out_specs=[per(st_shape), per(hp1)],
        out_shape=[jax.ShapeDtypeStruct((b,) + st_shape, F32), jax.ShapeDtypeStruct((b,) + hp1, F32)],
        compiler_params=_cparams("parallel"))(
            ssm_state, xs.reshape((b,) + hp1), dt[:, :MB_HEADS].reshape((b,) + h11),
            ed[:, :MB_HEADS].reshape((b,) + h11), bm, cm)
    yg = pl.pallas_call(
        _mamba_step_out_body,
        grid=(1,),
        in_specs=[_full((b, MB_INNER))] * 3 + [_full((1, MB_INNER))] * 2,
        out_specs=_full((b, MB_INNER)),
        out_shape=jax.ShapeDtypeStruct((b, MB_INNER), F32),
        compiler_params=_cparams("arbitrary"))(
            y.reshape(b, MB_INNER), xs, z, vecs['d_skip'], vecs['norm_w'])
    return yg, fin


def _rope_tables(pos):
    half = ROPE_DIM // 2
    inv_freq = ROPE_THETA ** (-jnp.arange(0, ROPE_DIM, 2, dtype=F32) / ROPE_DIM)
    ang = pos.astype(F32)[:, None] * inv_freq[None, :]
    cos, sin = jnp.cos(ang), jnp.sin(ang)
    n = pos.shape[0]
    ones = jnp.ones((n, DA_HEAD - ROPE_DIM), F32)
    zeros = jnp.zeros((n, DA_HEAD - ROPE_DIM), F32)
    zh = jnp.zeros((n, half), F32)
    ct = jnp.concatenate([cos, cos, ones], axis=1)
    s_up = jnp.concatenate([-sin, zh, zeros], axis=1)
    s_dn = jnp.concatenate([zh, sin, zeros], axis=1)
    two = lambda x: jnp.concatenate([x, x], axis=1)
    return two(ct), two(s_up), two(s_dn)


def _rope(x, ct, s_up, s_dn):
    half = ROPE_DIM // 2
    cols = []
    for j in range(x.shape[1] // LANES):
        xj = x[:, j * LANES:(j + 1) * LANES]
        cols.append(xj * ct + pltpu.roll(xj, LANES - half, axis=1) * s_up + pltpu.roll(xj, half, axis=1) * s_dn)
    return jnp.concatenate(cols, axis=1)


def _da_epilogue(ys, rows, fulls):
    y = ys[0]
    ct, s_up, s_dn = rows
    q = _rope(y[:, :DA_WIDTH], ct, s_up, s_dn)
    k = _rope(y[:, DA_WIDTH:2 * DA_WIDTH], ct, s_up, s_dn)
    return q, k, y[:, 2 * DA_WIDTH:3 * DA_WIDTH], y[:, 3 * DA_WIDTH:]


def _lambda(lq1, lk1, lq2, lk2, lam_init):
    return (jnp.exp(jnp.sum(lq1 * lk1, axis=-1, keepdims=True))
            - jnp.exp(jnp.sum(lq2 * lk2, axis=-1, keepdims=True)) + lam_init)


def _flash_body(lam_init, tq, q_ref, k_ref, v_ref, lq1, lk1, lq2, lk2, sw_ref, o_ref,
                m1, l1, m2, l2, acc1, acc2):
    qi = pl.program_id(2)
    kj = pl.program_id(3)

    @pl.when(kj == 0)
    def _():
        for m in (m1, m2):
            m[...] = jnp.full_like(m, NEG_INF)
        for z in (l1, l2, acc1, acc2):
            z[...] = jnp.zeros_like(z)

    @pl.when(kj <= qi)
    def _():
        q = q_ref[0] * DA_SCALE
        k = k_ref[0].astype(BF16)
        v = v_ref[0].astype(BF16)
        lane = lax.broadcasted_iota(jnp.int32, q.shape, 1)
        m0 = lane < DA_HEAD
        rows = qi * tq + lax.broadcasted_iota(jnp.int32, (tq, tq), 0)
        cols = kj * tq + lax.broadcasted_iota(jnp.int32, (tq, tq), 1)
        keep = rows >= cols
        for comp, (m_ref, l_ref, acc_ref) in enumerate(((m1, l1, acc1), (m2, l2, acc2))):
            qc = jnp.where(m0, q, 0.0) if comp == 0 else jnp.where(m0, 0.0, q)
            s = lax.dot_general(qc.astype(BF16), k, _NT, preferred_element_type=F32)
            s = jnp.where(keep, s, NEG_INF)
            m_old = m_ref[...]
            m_new = jnp.maximum(m_old, jnp.max(s, axis=-1, keepdims=True))
            alpha = jnp.exp(m_old - m_new)
            pr = jnp.exp(s - m_new)
            l_ref[...] = l_ref[...] * alpha + jnp.sum(pr, axis=-1, keepdims=True)
            acc_ref[...] = acc_ref[...] * alpha + jnp.dot(pr.astype(BF16), v, preferred_element_type=F32)
            m_ref[...] = m_new

    @pl.when(kj == qi)
    def _():
        lam = _lambda(lq1[...], lk1[...], lq2[...], lk2[...], lam_init)
        o = acc1[...] / l1[...] - lam * (acc2[...] / l2[...])
        o_ref[0] = _rms(o, sw_ref[...], DA_SUBLN_EPS) * (1.0 - lam_init)


def _flash(q, k, v, b, t, p, lam_init):
    tq = min(t, 512)
    nq = t // tq
    shp = lambda x: x.reshape(b, t, DA_WIDTH)
    qblk = pl.BlockSpec((1, tq, LANES), lambda bi, hi, qi, kj: (bi, qi, hi))
    kblk = pl.BlockSpec((1, tq, LANES), lambda bi, hi, qi, kj: (bi, jnp.minimum(kj, qi), hi))
    vec = lambda x: x.reshape(1, -1)
    stat = pltpu.VMEM((tq, 1), F32)
    acc = pltpu.VMEM((tq, LANES), F32)
    o = pl.pallas_call(
        functools.partial(_flash_body, lam_init, tq),
        grid=(b, DA_HEADS, nq, nq),
        in_specs=[qblk, kblk, kblk] + [_full((1, DA_HEAD))] * 4 + [_full((1, DA_VDIM))],
        out_specs=qblk,
        out_shape=jax.ShapeDtypeStruct((b, t, DA_WIDTH), F32),
        scratch_shapes=[stat, stat, stat, stat, acc, acc],
        compiler_params=_cparams("parallel", "parallel", "parallel", "arbitrary"))(
            shp(q), shp(k), shp(v), vec(p['da_lq1']), vec(p['da_lk1']), vec(p['da_lq2']), vec(p['da_lk2']),
            vec(p['da_subln_w']))
    return o.reshape(b * t, DA_WIDTH)


def _decode_body(lam_init, npg, pt_ref, q_ref, kn_ref, vn_ref, *refs):
    ck_refs = refs[:npg]
    cv_refs = refs[npg:2 * npg]
    lq1, lk1, lq2, lk2, sw_ref, o_ref, qm_s, m_s, l_s, acc = refs[2 * npg:]
    j = pl.program_id(1)
    nj = pl.num_programs(1)
    nrow = 2 * DA_HEADS
    ncol = PAGE_SIZE * DA_HEADS
    rid = lax.broadcasted_iota(jnp.int32, (nrow, LANES), 0)

    @pl.when(j == 0)
    def _():
        comp = lax.broadcasted_iota(jnp.int32, (nrow, LANES), 1) // DA_HEAD
        qm = jnp.where(jnp.bitwise_and(rid, 1) == comp, q_ref[0] * DA_SCALE, 0.0)
        qm_s[...] = qm
        m_s[...] = jnp.sum(qm * kn_ref[0], axis=-1, keepdims=True)
        l_s[...] = jnp.ones_like(l_s)
        acc[...] = vn_ref[0]

    qm = qm_s[...]
    keep = (lax.rem(lax.broadcasted_iota(jnp.int32, (nrow, ncol), 1), DA_HEADS)
            == lax.broadcasted_iota(jnp.int32, (nrow, ncol), 0) // 2)
    scores = [jnp.where(keep, _dot_nt(qm, ck[0].reshape(ncol, DA_VDIM)), NEG_INF) for ck in ck_refs]
    m_old = m_s[...]
    m_new = m_old
    for s in scores:
        m_new = jnp.maximum(m_new, jnp.max(s, axis=-1, keepdims=True))
    alpha = jnp.exp(m_old - m_new)
    l_new = l_s[...] * alpha
    a_new = acc[...] * alpha
    for s, cv in zip(scores, cv_refs):
        pr = jnp.exp(s - m_new)
        l_new = l_new + jnp.sum(pr, axis=-1, keepdims=True)
        a_new = a_new + _dot(pr, cv[0].reshape(ncol, DA_VDIM))
    m_s[...] = m_new
    l_s[...] = l_new
    acc[...] = a_new

    @pl.when(j == nj - 1)
    def _():
        lam = _lambda(lq1[...], lk1[...], lq2[...], lk2[...], lam_init)
        d = (a_new / l_new) * jnp.where(jnp.bitwise_and(rid, 1) == 0, 1.0, -lam)
        pair = d + pltpu.roll(d, nrow - 1, axis=0)
        o_ref[0] = _rms(pair, sw_ref[...], DA_SUBLN_EPS) * (1.0 - lam_init)


def _decode(q, k_new, v_new, cache_k, cache_v, page_table, p, lam_init):
    b = q.shape[0]
    n_pages = page_table.shape[1]
    npg = DEC_PAGES if n_pages % DEC_PAGES == 0 else 1
    nrow = 2 * DA_HEADS
    rows16 = lambda x: jnp.repeat(x.reshape(b, DA_HEADS, DA_VDIM), 2, axis=1)
    row = pl.BlockSpec((1, nrow, DA_VDIM), lambda bi, j, pt: (bi, 0, 0))
    pages = [pl.BlockSpec((1, PAGE_SIZE, DA_HEADS, DA_VDIM),
                          lambda bi, j, pt, i=i: (pt[bi, j * npg + i], 0, 0, 0)) for i in range(npg)]
    vec = lambda x: x.reshape(1, -1)
    small = lambda n: pl.BlockSpec((1, n), lambda bi, j, pt: (0, 0))
    o = pl.pallas_call(
        functools.partial(_decode_body, lam_init, npg),
        grid_spec=pltpu.PrefetchScalarGridSpec(
            num_scalar_prefetch=1, grid=(b, n_pages // npg),
            in_specs=[row, row, row] + pages + pages + [small(DA_HEAD)] * 4 + [small(DA_VDIM)],
            out_specs=row,
            scratch_shapes=[pltpu.VMEM((nrow, DA_VDIM), F32), pltpu.VMEM((nrow, 1), F32),
                            pltpu.VMEM((nrow, 1), F32), pltpu.VMEM((nrow, DA_VDIM), F32)]),
        out_shape=jax.ShapeDtypeStruct((b, nrow, DA_VDIM), F32),
        compiler_params=_cparams("parallel", "arbitrary"))(
            page_table, rows16(q), rows16(k_new), rows16(v_new), *([cache_k] * npg), *([cache_v] * npg),
            vec(p['da_lq1']), vec(p['da_lk1']), vec(p['da_lq2']), vec(p['da_lk2']), vec(p['da_subln_w']))
    return o[:, ::2, :].reshape(b, DA_WIDTH)


def _cf_epilogue(ys, rows, fulls):
    y = ys[0]
    u = y[:, :CF_WIDTH] * _sigmoid(y[:, CF_WIDTH:2 * CF_WIDTH])
    return u, y[:, 2 * CF_WIDTH:]


def _cf_conv_body(tc, u_ref, up_ref, cw_ref, cb_ref, lw_ref, lb_ref, o_ref, buf):
    c = pl.program_id(1)
    buf[0:CF_HIST, :] = jnp.where(c == 0, 0.0, up_ref[0])
    buf[CF_HIST:CF_HIST + tc, :] = u_ref[0]
    off = CF_HIST - (CF_KERNEL - 1)
    acc = cb_ref[...] + cw_ref[0:1, :] * buf[off:off + tc, :]
    for j in range(1, CF_KERNEL):
        acc = acc + cw_ref[j:j + 1, :] * buf[off + j:off + j + tc, :]
    mu = jnp.mean(acc, axis=-1, keepdims=True)
    xc = acc - mu
    var = jnp.mean(xc * xc, axis=-1, keepdims=True)
    o_ref[0] = _silu(xc * lax.rsqrt(var + CF_LN_EPS) * lw_ref[...] + lb_ref[...])


def _cf_conv(u, b, t, p):
    tc = min(t, 256)
    per = tc // CF_HIST
    blk = pl.BlockSpec((1, tc, CF_WIDTH), lambda bi, ci: (bi, ci, 0))
    prev = pl.BlockSpec((1, CF_HIST, CF_WIDTH), lambda bi, ci: (bi, jnp.maximum(ci * per - 1, 0), 0))
    vec = lambda x: x.reshape(1, CF_WIDTH)
    u3 = u.reshape(b, t, CF_WIDTH)
    o = pl.pallas_call(
        functools.partial(_cf_conv_body, tc),
        grid=(b, t // tc),
        in_specs=[blk, prev, _full((CF_KERNEL, CF_WIDTH))] + [_full((1, CF_WIDTH))] * 3,
        out_specs=blk,
        out_shape=jax.ShapeDtypeStruct((b, t, CF_WIDTH), F32),
        scratch_shapes=[pltpu.VMEM((CF_HIST + tc, CF_WIDTH), F32)],
        compiler_params=_cparams("parallel", "parallel"))(
            u3, u3, p['cf_conv_w'], vec(p['cf_conv_b']), vec(p['cf_ln_w']), vec(p['cf_ln_b']))
    return o.reshape(b * t, CF_WIDTH)


def _cf_step_body(u_ref, st_ref, cw_ref, cb_ref, lw_ref, lb_ref, o_ref):
    nb = u_ref.shape[0]
    cw = cw_ref[...]
    hist = jnp.concatenate([jnp.sum(st_ref[i] * cw[:CF_KERNEL - 1, :], axis=0, keepdims=True)
                            for i in range(nb)], axis=0)
    acc = cb_ref[...] + hist + cw[CF_KERNEL - 1:CF_KERNEL, :] * u_ref[...]
    mu = jnp.mean(acc, axis=-1, keepdims=True)
    xc = acc - mu
    var = jnp.mean(xc * xc, axis=-1, keepdims=True)
    o_ref[...] = _silu(xc * lax.rsqrt(var + CF_LN_EPS) * lw_ref[...] + lb_ref[...])


def _cf_step(u, state, p):
    b = u.shape[0]
    tb = SUBLANES
    vec = lambda x: x.reshape(1, CF_WIDTH)
    return pl.pallas_call(
        _cf_step_body,
        grid=(b // tb,),
        in_specs=[pl.BlockSpec((tb, CF_WIDTH), lambda i: (i, 0)),
                  pl.BlockSpec((tb, CF_KERNEL - 1, CF_WIDTH), lambda i: (i, 0, 0)),
                  _full((CF_KERNEL, CF_WIDTH))] + [_full((1, CF_WIDTH))] * 3,
        out_specs=pl.BlockSpec((tb, CF_WIDTH), lambda i: (i, 0)),
        out_shape=jax.ShapeDtypeStruct((b, CF_WIDTH), F32),
        compiler_params=_cparams("parallel"))(
            u, state, p['cf_conv_w'], vec(p['cf_conv_b']), vec(p['cf_ln_w']), vec(p['cf_ln_b']))


def _forward(x_prompt, x_sample, state_rwkv_shift, state_rwkv_wkv, state_mamba_conv, state_mamba_ssm,
             cache_k, cache_v, page_table, state_conformer_conv, p):
    bp, tp, d = x_prompt.shape
    bs = x_sample.shape[0]
    hp = x_prompt.reshape(bp * tp, d)
    hs = x_sample.reshape(bs, d)
    npre, npost = p['norm_pre'], p['norm_post']

    (r, k, v, g, a, lw), rw_shift_p = _rwkv_proj_seq(hp, bp, tp, npre[0], p)
    o, rw_wkv_p = _rwkv_chunk(r, k, v, a, lw, bp, tp, p)
    hp = _out_proj(o, g, hp, p['rw_w_o'], npost[0])
    (r, k, v, g, a, lw), rw_shift_s = _rwkv_proj_step(hs, state_rwkv_shift, npre[0], p)
    o, rw_wkv_s = _rwkv_step(r, k, v, a, lw, state_rwkv_wkv, p)
    hs = _out_proj(o, g, hs, p['rw_w_o'], npost[0])

    mw = _mamba_weights(p)
    mv = _mamba_vecs(p)
    ident = lambda ys, rows, fulls: ys
    widths = (MB_INNER, MB_CONV_DIM, LANES)
    z, xbc, dtr = _in_proj(hp, npre[1], mw, ident, widths, tm=256)
    yg, mb_ssm_p = _mamba_chunk(z, xbc, dtr, bp, tp, mv)
    mb_conv_p = xbc.reshape(bp, tp, MB_CONV_DIM)[:, tp - (MB_CONV - 1):]
    hp = _out_proj(yg, None, hp, p['mb_w_out'], npost[1])
    z, xbc, dtr = _in_proj(hs, npre[1], mw, ident, widths)
    yg, mb_ssm_s = _mamba_step(z, xbc, dtr, state_mamba_conv, state_mamba_ssm, mv)
    mb_conv_s = jnp.concatenate([state_mamba_conv[:, 1:], xbc[:, None, :]], axis=1)
    hs = _out_proj(yg, None, hs, p['mb_w_out'], npost[1])

    lam_init = 0.8 - 0.6 * math.exp(-0.3 * 2)
    tabs_p = _rope_tables(jnp.arange(tp, dtype=jnp.int32))
    tm = min(_row_tile(tp), 256)
    tps = tp // tm
    rows_p = [(tb, lambda i: (i % tps, 0), LANES) for tb in tabs_p]
    q, k, v, g = _in_proj(hp, npre[2], [p['da_w_qkvg']], _da_epilogue, (DA_WIDTH,) * 4, row_extras=rows_p, tm=tm)
    k_rows_p = k.reshape(bp, tp, DA_HEADS, 2 * DA_HEAD)
    v_rows_p = v.reshape(bp, tp, DA_HEADS, DA_VDIM)
    o = _flash(q, k, v, bp, tp, p, lam_init)
    hp = _out_proj(o, g, hp, p['da_w_o'], npost[2])
    tabs_s = _rope_tables(jnp.full((bs,), PAST_LEN, dtype=jnp.int32))
    rows_s = [(tb, lambda i: (i, 0), LANES) for tb in tabs_s]
    q, k, v, g = _in_proj(hs, npre[2], [p['da_w_qkvg']], _da_epilogue, (DA_WIDTH,) * 4, row_extras=rows_s)
    k_rows_s = k.reshape(bs, 1, DA_HEADS, 2 * DA_HEAD)
    v_rows_s = v.reshape(bs, 1, DA_HEADS, DA_VDIM)
    o = _decode(q, k, v, cache_k, cache_v, page_table, p, lam_init)
    hs = _out_proj(o, g, hs, p['da_w_o'], npost[2])

    u, g = _in_proj(hp, npre[3], [p['cf_w_in']], _cf_epilogue, (CF_WIDTH,) * 2)
    cf_conv_p = u.reshape(bp, tp, CF_WIDTH)[:, tp - (CF_KERNEL - 1):]
    c = _cf_conv(u, bp, tp, p)
    hp = _out_proj(c, g, hp, p['cf_w_out'], npost[3])
    u, g = _in_proj(hs, npre[3], [p['cf_w_in']], _cf_epilogue, (CF_WIDTH,) * 2)
    cf_conv_s = jnp.concatenate([state_conformer_conv[:, 1:], u[:, None, :]], axis=1)
    c = _cf_step(u, state_conformer_conv, p)
    hs = _out_proj(c, g, hs, p['cf_w_out'], npost[3])

    return (hp.reshape(bp, tp, d), hs.reshape(bs, 1, d), rw_shift_p, rw_shift_s, rw_wkv_p, rw_wkv_s,
            mb_conv_p, mb_conv_s, mb_ssm_p, mb_ssm_s, k_rows_p, k_rows_s, v_rows_p, v_rows_s,
            cf_conv_p, cf_conv_s)


def kernel(x_prompt, x_sample, state_rwkv_shift, state_rwkv_wkv, state_mamba_conv, state_mamba_ssm, cache_k, cache_v, page_table, state_conformer_conv, norm_pre, norm_post, rw_mu, rw_w_r, rw_w_k, rw_w_v, rw_w_g, rw_w0, rw_w_w1, rw_w_w2, rw_a0, rw_w_a1, rw_w_a2, rw_k_k, rw_k_a, rw_r_k, rw_gn_w, rw_gn_b, rw_w_o, mb_w_in, mb_conv_w, mb_conv_b, mb_dt_bias, mb_a_log, mb_d, mb_norm_w, mb_w_out, da_w_qkvg, da_lq1, da_lk1, da_lq2, da_lk2, da_subln_w, da_w_o, cf_w_in, cf_conv_w, cf_conv_b, cf_ln_w, cf_ln_b, cf_w_out):
    p = dict(norm_pre=norm_pre, norm_post=norm_post, rw_mu=rw_mu, rw_w_r=rw_w_r, rw_w_k=rw_w_k, rw_w_v=rw_w_v,
             rw_w_g=rw_w_g, rw_w0=rw_w0, rw_w_w1=rw_w_w1, rw_w_w2=rw_w_w2, rw_a0=rw_a0, rw_w_a1=rw_w_a1,
             rw_w_a2=rw_w_a2, rw_k_k=rw_k_k, rw_k_a=rw_k_a, rw_r_k=rw_r_k, rw_gn_w=rw_gn_w, rw_gn_b=rw_gn_b,
             rw_w_o=rw_w_o, mb_w_in=mb_w_in, mb_conv_w=mb_conv_w, mb_conv_b=mb_conv_b, mb_dt_bias=mb_dt_bias,
             mb_a_log=mb_a_log, mb_d=mb_d, mb_norm_w=mb_norm_w, mb_w_out=mb_w_out, da_w_qkvg=da_w_qkvg,
             da_lq1=da_lq1, da_lk1=da_lk1, da_lq2=da_lq2, da_lk2=da_lk2, da_subln_w=da_subln_w, da_w_o=da_w_o,
             cf_w_in=cf_w_in, cf_conv_w=cf_conv_w, cf_conv_b=cf_conv_b, cf_ln_w=cf_ln_w, cf_ln_b=cf_ln_b,
             cf_w_out=cf_w_out)
    return _forward(x_prompt, x_sample, state_rwkv_shift, state_rwkv_wkv, state_mamba_conv, state_mamba_ssm,
                    cache_k, cache_v, page_table, state_conformer_conv, p)
```

```python
import functools
import math

import jax
import jax.numpy as jnp
from jax import lax
from jax.experimental import pallas as pl
from jax.experimental.pallas import tpu as pltpu

F32 = jnp.float32
BF16 = jnp.bfloat16

D_MODEL = 1024
PAST_LEN = 8192
PAGE_SIZE = 128
RMS_EPS = 1e-6

RW_HEAD = 64
RW_HEADS = D_MODEL // RW_HEAD
RW_LORA = 64
RW_GN_EPS = 64e-5
RW_CHUNK = 64
RW_GROUP = 4
RW_PAIRS = 2

MB_INNER = 2 * D_MODEL
MB_HEADDIM = 64
MB_HEADS = MB_INNER // MB_HEADDIM
MB_GROUPS = 4
MB_STATE = 128
MB_CONV = 4
MB_CHUNK = 128
MB_CONV_DIM = MB_INNER + 2 * MB_GROUPS * MB_STATE
MB_NORM_EPS = 1e-5
MB_GROUP_WIDTH = MB_INNER // MB_GROUPS

DA_HEADS = 8
DA_HEAD = 64
DA_VDIM = 2 * DA_HEAD
DA_WIDTH = DA_HEADS * DA_VDIM
DA_SCALE = DA_HEAD ** -0.5
DA_SUBLN_EPS = 1e-5
ROPE_THETA = 500000.0
ROPE_DIM = DA_HEAD // 4
DEC_PAGES = 4

CF_WIDTH = D_MODEL
CF_KERNEL = 31
CF_LN_EPS = 1e-5
CF_HIST = 32
CF_ROWS = 32

LANES = 128
SUBLANES = 8
VMEM_LIMIT = 56 * 1024 * 1024

NEG_INF = float("-inf")


def _cparams(*sem):
    return pltpu.CompilerParams(dimension_semantics=sem, vmem_limit_bytes=VMEM_LIMIT)


def _dot(a, b):
    return jnp.dot(a.astype(BF16), b.astype(BF16), preferred_element_type=F32)


def _dot_nt(a, b):
    return lax.dot_general(a.astype(BF16), b.astype(BF16), (((1,), (1,)), ((), ())),
                           preferred_element_type=F32)


def _dot_tn(a, b):
    return lax.dot_general(a.astype(BF16), b.astype(BF16), (((0,), (0,)), ((), ())),
                           preferred_element_type=F32)


def _split2(a):
    hi = a.astype(BF16)
    lo = (a - hi.astype(F32)).astype(BF16)
    return hi, lo


def _split3(a):
    hi = a.astype(BF16)
    r1 = a - hi.astype(F32)
    mid = r1.astype(BF16)
    lo = (r1 - mid.astype(F32)).astype(BF16)
    return hi, mid, lo


def _dotx(a, b, dn=(((1,), (0,)), ((), ()))):
    ah, al = _split2(a)
    bh, bl = _split2(b)
    f = functools.partial(lax.dot_general, dimension_numbers=dn, preferred_element_type=F32)
    return f(ah, bh) + (f(ah, bl) + f(al, bh))


_NT = (((1,), (1,)), ((), ()))
_TN = (((0,), (0,)), ((), ()))


def _cumsum_rows(tri_bf16, x):
    hi, mid, lo = _split3(x)
    f = functools.partial(jnp.dot, preferred_element_type=F32)
    return f(tri_bf16, hi) + (f(tri_bf16, mid) + f(tri_bf16, lo))


def _sigmoid(x):
    return 1.0 / (1.0 + jnp.exp(-x))


def _silu(x):
    return x * _sigmoid(x)


def _softplus(x):
    return jnp.maximum(x, 0.0) + jnp.log(1.0 + jnp.exp(-jnp.abs(x)))


def _rms(x, w, eps):
    return x * lax.rsqrt(jnp.mean(x * x, axis=-1, keepdims=True) + eps) * w


def _row_tile(m):
    for t in (512, 256, 128, 64, 32, 16, 8):
        if m % t == 0:
            return t
    return m


def _full(shape):
    nd = len(shape)
    return pl.BlockSpec(shape, lambda *_: (0,) * nd)


def _out_proj_body(has_gate, *refs):
    if has_gate:
        y_ref, g_ref, h_ref, w_ref, pw_ref, o_ref = refs
        y = y_ref[...] * _silu(g_ref[...])
    else:
        y_ref, h_ref, w_ref, pw_ref, o_ref = refs
        y = y_ref[...]
    z = _dot(y, w_ref[...])
    o_ref[...] = h_ref[...] + _rms(z, pw_ref[...], RMS_EPS)


def _out_proj(y, gate, h, w, post_w):
    m, kdim = y.shape
    tm = _row_tile(m)
    row = lambda width: pl.BlockSpec((tm, width), lambda i: (i, 0))
    args, specs = [y], [row(kdim)]
    if gate is not None:
        args.append(gate)
        specs.append(row(D_MODEL))
    args += [h, w.astype(BF16), post_w.reshape(1, D_MODEL)]
    specs += [row(D_MODEL), _full((kdim, D_MODEL)), _full((1, D_MODEL))]
    return pl.pallas_call(
        functools.partial(_out_proj_body, gate is not None),
        grid=(m // tm,), in_specs=specs, out_specs=row(D_MODEL),
        out_shape=jax.ShapeDtypeStruct((m, D_MODEL), F32),
        compiler_params=_cparams("parallel"))(*args)


def _in_proj_body(n_w, n_row, n_full, epilogue, *refs):
    h_ref, nw_ref = refs[0], refs[1]
    w_refs = refs[2:2 + n_w]
    row_refs = refs[2 + n_w:2 + n_w + n_row]
    full_refs = refs[2 + n_w + n_row:2 + n_w + n_row + n_full]
    out_refs = refs[2 + n_w + n_row + n_full:]
    xn = _rms(h_ref[...], nw_ref[...], RMS_EPS).astype(BF16)
    ys = [jnp.dot(xn, w[...], preferred_element_type=F32) for w in w_refs]
    outs = epilogue(ys, [r[...] for r in row_refs], [r[...] for r in full_refs])
    for o_ref, o in zip(out_refs, outs):
        o_ref[...] = o.astype(o_ref.dtype)


def _in_proj(h, norm_w, weights, epilogue, out_widths, row_extras=(), full_extras=(), tm=None):
    m = h.shape[0]
    tm = tm or _row_tile(m)
    row = lambda width: pl.BlockSpec((tm, width), lambda i: (i, 0))
    args = [h, norm_w.reshape(1, D_MODEL)] + [w.astype(BF16) for w in weights]
    specs = [row(D_MODEL), _full((1, D_MODEL))] + [_full(w.shape) for w in weights]
    for arr, imap, width in row_extras:
        args.append(arr)
        specs.append(pl.BlockSpec((tm, width), imap))
    for arr in full_extras:
        args.append(arr)
        specs.append(_full(arr.shape))
    out_specs, out_shape = [], []
    for wd in out_widths:
        width, dtype, transposed = wd if isinstance(wd, tuple) else (wd, F32, False)
        if transposed:
            out_specs.append(pl.BlockSpec((width, tm), lambda i: (0, i)))
            out_shape.append(jax.ShapeDtypeStruct((width, m), dtype))
        else:
            out_specs.append(row(width))
            out_shape.append(jax.ShapeDtypeStruct((m, width), dtype))
    return pl.pallas_call(
        functools.partial(_in_proj_body, len(weights), len(row_extras), len(full_extras), epilogue),
        grid=(m // tm,), in_specs=specs, out_specs=out_specs, out_shape=out_shape,
        compiler_params=_cparams("parallel"))(*args)


def _rwkv_mix(xn, xprev, mu, wr, wk, wv, wg, w0, ww1, ww2, a0, wa1, wa2):
    xx = xprev - xn
    xm = [xn + xx * mu[i:i + 1, :] for i in range(6)]
    r = _dot(xm[0], wr)
    k = _dot(xm[2], wk)
    v = _dot(xm[3], wv)
    g = _dot(xm[5], wg)
    zw = w0 + _dot(jnp.tanh(_dot(xm[1], ww1)), ww2)
    w_log = -_softplus(-zw) - 0.5
    lw = -jnp.exp(w_log)
    a = _sigmoid(a0 + _dot(_dot(xm[4], wa1), wa2))
    return r, k, v, g, a, lw


def _rwkv_proj_seq_body(tiles_per_seq, h_ref, hp_ref, nw_ref, mu_ref, wr, wk, wv, wg, w0, ww1, ww2,
                        a0, wa1, wa2, r_ref, k_ref, v_ref, g_ref, a_ref, lw_ref, last_ref):
    i = pl.program_id(0)
    nw = nw_ref[...]
    xn = _rms(h_ref[...], nw, RMS_EPS)
    tm = xn.shape[0]
    pn = _rms(hp_ref[...], nw, RMS_EPS)[SUBLANES - 1:SUBLANES, :]
    first = (i % tiles_per_seq) == 0
    prow = jnp.where(first, 0.0, pn)
    rows = lax.broadcasted_iota(jnp.int32, xn.shape, 0)
    xprev = jnp.where(rows == 0, prow, pltpu.roll(xn, 1, axis=0))
    outs = _rwkv_mix(xn, xprev, mu_ref[...], wr[...], wk[...], wv[...], wg[...], w0[...], ww1[...],
                     ww2[...], a0[...], wa1[...], wa2[...])
    for o_ref, o in zip((r_ref, k_ref, v_ref, g_ref, a_ref, lw_ref), outs):
        o_ref[...] = o
    last_ref[0] = xn[tm - 1:tm, :]


def _rwkv_proj_step_body(h_ref, sh_ref, nw_ref, mu_ref, wr, wk, wv, wg, w0, ww1, ww2,
                         a0, wa1, wa2, r_ref, k_ref, v_ref, g_ref, a_ref, lw_ref, last_ref):
    xn = _rms(h_ref[...], nw_ref[...], RMS_EPS)
    outs = _rwkv_mix(xn, sh_ref[...], mu_ref[...], wr[...], wk[...], wv[...], wg[...], w0[...], ww1[...],
                     ww2[...], a0[...], wa1[...], wa2[...])
    for o_ref, o in zip((r_ref, k_ref, v_ref, g_ref, a_ref, lw_ref), outs):
        o_ref[...] = o
    last_ref[...] = xn


def _rwkv_weights(p):
    d = D_MODEL
    vec = lambda x: x.reshape(1, d)
    args = [p['rw_mu'], p['rw_w_r'].astype(BF16), p['rw_w_k'].astype(BF16), p['rw_w_v'].astype(BF16),
            p['rw_w_g'].astype(BF16), vec(p['rw_w0']), p['rw_w_w1'].astype(BF16), p['rw_w_w2'].astype(BF16),
            vec(p['rw_a0']), p['rw_w_a1'].astype(BF16), p['rw_w_a2'].astype(BF16)]
    return args, [_full(a.shape) for a in args]


def _rwkv_proj_seq(h2, b, t, norm_w, p):
    m = b * t
    tm = min(_row_tile(t), 256)
    tps = t // tm
    row = pl.BlockSpec((tm, D_MODEL), lambda i: (i, 0))
    prev = pl.BlockSpec((SUBLANES, D_MODEL), lambda i: (jnp.maximum(i * (tm // SUBLANES) - 1, 0), 0))
    wargs, wspecs = _rwkv_weights(p)
    outs = pl.pallas_call(
        functools.partial(_rwkv_proj_seq_body, tps),
        grid=(m // tm,),
        in_specs=[row, prev, _full((1, D_MODEL))] + wspecs,
        out_specs=[row] * 6 + [pl.BlockSpec((1, 1, D_MODEL), lambda i: (i // tps, 0, 0))],
        out_shape=[jax.ShapeDtypeStruct((m, D_MODEL), F32)] * 6
        + [jax.ShapeDtypeStruct((b, 1, D_MODEL), F32)],
        compiler_params=_cparams("arbitrary"))(h2, h2, norm_w.reshape(1, D_MODEL), *wargs)
    return outs[:6], outs[6].reshape(b, D_MODEL)


def _rwkv_proj_step(h2, shift, norm_w, p):
    m = h2.shape[0]
    wargs, wspecs = _rwkv_weights(p)
    outs = pl.pallas_call(
        _rwkv_proj_step_body,
        grid=(1,),
        in_specs=[_full((m, D_MODEL)), _full((m, D_MODEL)), _full((1, D_MODEL))] + wspecs,
        out_specs=[_full((m, D_MODEL))] * 7,
        out_shape=[jax.ShapeDtypeStruct((m, D_MODEL), F32)] * 7,
        compiler_params=_cparams("arbitrary"))(h2, shift, norm_w.reshape(1, D_MODEL), *wargs)
    return outs[:6], outs[6]


def _pair_sum(x, m0):
    s0 = jnp.sum(jnp.where(m0, x, 0.0), axis=-1, keepdims=True)
    s1 = jnp.sum(jnp.where(m0, 0.0, x), axis=-1, keepdims=True)
    return jnp.where(m0, s0, s1)


def _rwkv_chunk_body(gsz, npair, r_ref, k_ref, v_ref, a_ref, lw_ref, kk_ref, ka_ref, rk_ref, gw_ref, gb_ref,
                     o_ref, s_ref, bd_ref):
    cg = pl.program_id(2)
    ncg = pl.num_programs(2)
    cl = RW_CHUNK
    two = 2 * cl

    @pl.when(cg == 0)
    def _():
        bd_ref[...] = jnp.zeros_like(bd_ref)

    lane = lax.broadcasted_iota(jnp.int32, (cl, LANES), 1)
    m0 = lane < RW_HEAD
    ti = lax.broadcasted_iota(jnp.int32, (cl, cl), 0)
    tj = lax.broadcasted_iota(jnp.int32, (cl, cl), 1)
    tri = jnp.where(ti >= tj, 1.0, 0.0).astype(BF16)
    row2 = lax.broadcasted_iota(jnp.int32, (two, LANES), 0)
    col2 = lax.broadcasted_iota(jnp.int32, (two, LANES), 1)
    same = (row2 >= cl) == (col2 >= RW_HEAD)
    step_r = jnp.bitwise_and(row2, cl - 1)
    step_c = jnp.bitwise_and(col2, cl - 1)
    bd_strict = jnp.logical_and(same, step_r > step_c)
    bd_incl = jnp.logical_and(same, step_r >= step_c)
    eye = row2 == col2

    def stack(x):
        return jnp.where(same, jnp.concatenate([x, x], axis=0), 0.0)

    chunks = range(gsz)
    st = []
    for pi in range(npair):
        cols = slice(pi * LANES, (pi + 1) * LANES)
        kk_w, ka_w, rk_w = kk_ref[:, cols], ka_ref[:, cols], rk_ref[:, cols]
        for gi in chunks:
            sl = slice(gi * cl, (gi + 1) * cl)
            r = r_ref[0, sl, cols]
            k = k_ref[0, sl, cols]
            v = v_ref[0, sl, cols]
            a = a_ref[0, sl, cols]
            lw = lw_ref[0, sl, cols]
            kkv = k * kk_w
            kap = kkv * lax.rsqrt(jnp.maximum(_pair_sum(kkv * kkv, m0), 1e-24))
            kmod = k * (1.0 + (a - 1.0) * ka_w)
            st.append(dict(r=r, v=v, lw=lw, kap=kap, kmod=kmod, bvec=kap * a,
                           bonus=_pair_sum(r * kmod * rk_w, m0) * v))
    for s in st:
        s['cum'] = _cumsum_rows(tri, s['lw'])
    for s in st:
        cum = s['cum']
        cum_last = cum[cl - 1:cl, :]
        g_inv = jnp.exp(-cum)
        g_end = jnp.exp(cum_last - cum)
        s['g_all'] = jnp.exp(cum_last)
        s['kapm'] = stack(s['kap'] * jnp.exp(cum - s['lw']))
        s['rm'] = stack(s['r'] * jnp.exp(cum))
        s['v2m'] = stack(s['v'])
        s['b_h'] = s['bvec'] * g_inv
        s['k_h'] = s['kmod'] * g_inv
        s['b_e'] = stack(s['bvec'] * g_end)
        s['k_e'] = stack(s['kmod'] * g_end)
    for s in st:
        lhs = jnp.concatenate([s['kapm'], s['rm']], axis=0)
        gb = _dot_nt(lhs, jnp.concatenate([s['b_h'], s['b_h']], axis=0))
        gk = _dot_nt(lhs, jnp.concatenate([s['k_h'], s['k_h']], axis=0))
        s['a_ab'] = jnp.where(bd_strict, gb[:two], 0.0)
        s['a_rb'] = jnp.where(bd_incl, gb[two:], 0.0)
        s['a_ak'] = jnp.where(bd_strict, gk[:two], 0.0)
        s['a_rk'] = jnp.where(bd_incl, gk[two:], 0.0)
    for s in st:
        s['tinv'] = jnp.where(eye, 1.0, 0.0) - s['a_ab']
        s['lp'] = _dot(s['a_ab'], s['a_ab'])
        s['x2m'] = _dot(s['a_ak'], s['v2m'])
    for _ in range(4):
        for s in st:
            both = _dot(jnp.concatenate([s['tinv'], s['lp']], axis=0), s['lp'])
            s['tinv'] = s['tinv'] + both[:two]
            s['lp'] = both[two:]
    for s in st:
        s['tinv'] = s['tinv'] + _dot(s['tinv'], s['lp'])
    for s in st:
        pq = _dot(s['tinv'], jnp.concatenate([s['kapm'], s['x2m']], axis=1))
        s['pm'] = -pq[:, :LANES]
        s['qm'] = -pq[:, LANES:]
    for s in st:
        ro = _dot(s['a_rb'], jnp.concatenate([s['pm'], s['qm']], axis=1))
        rtm = s['rm'] + ro[:, :LANES]
        oim = ro[:, LANES:] + _dot(s['a_rk'], s['v2m'])
        s['rt'] = rtm[:cl] + rtm[cl:]
        s['oi'] = oim[:cl] + oim[cl:]
    for s in st:
        s['corr'] = _dotx(s['pm'], s['b_e'], _TN)
        s['nmat'] = _dot_tn(jnp.concatenate([s['qm'], s['v2m']], axis=0),
                            jnp.concatenate([s['b_e'], s['k_e']], axis=0))
    bds = [bd_ref[pi] for pi in range(npair)]
    for gi in chunks:
        for pi in range(npair):
            s = st[pi * gsz + gi]
            cols = slice(pi * LANES, (pi + 1) * LANES)
            o = _dot_nt(s['rt'], bds[pi]) + s['oi']
            bds[pi] = bds[pi] * s['g_all'] + _dotx(bds[pi], s['corr']) + s['nmat']
            mean = _pair_sum(o, m0) * (1.0 / RW_HEAD)
            oc = o - mean
            var = _pair_sum(oc * oc, m0) * (1.0 / RW_HEAD)
            o_ref[0, gi * cl:(gi + 1) * cl, cols] = (
                oc * lax.rsqrt(var + RW_GN_EPS) * gw_ref[:, cols] + gb_ref[:, cols] + s['bonus'])
    for pi in range(npair):
        bd_ref[pi] = bds[pi]

    @pl.when(cg == ncg - 1)
    def _():
        for pi in range(npair):
            s_ref[0, 2 * pi] = bds[pi][:RW_HEAD, :RW_HEAD]
            s_ref[0, 2 * pi + 1] = bds[pi][RW_HEAD:, RW_HEAD:]


def _rwkv_chunk(r, k, v, a, lw, b, t, p):
    npair = RW_PAIRS
    hp = RW_HEADS // (2 * npair)
    gsz = RW_GROUP if t % (RW_GROUP * RW_CHUNK) == 0 else 1
    rows = gsz * RW_CHUNK
    shp = lambda x: x.reshape(b, t, D_MODEL)
    blk = pl.BlockSpec((1, rows, npair * LANES), lambda bi, hi, ci: (bi, ci, hi))
    vec = pl.BlockSpec((1, npair * LANES), lambda bi, hi, ci: (0, hi))
    vrow = lambda x: x.reshape(1, D_MODEL)
    o, s = pl.pallas_call(
        functools.partial(_rwkv_chunk_body, gsz, npair),
        grid=(b, hp, t // rows),
        in_specs=[blk] * 5 + [vec] * 5,
        out_specs=[blk, pl.BlockSpec((1, 2 * npair, RW_HEAD, RW_HEAD), lambda bi, hi, ci: (bi, hi, 0, 0))],
        out_shape=[jax.ShapeDtypeStruct((b, t, D_MODEL), F32),
                   jax.ShapeDtypeStruct((b, RW_HEADS, RW_HEAD, RW_HEAD), F32)],
        scratch_shapes=[pltpu.VMEM((npair, LANES, LANES), F32)],
        compiler_params=_cparams("parallel", "parallel", "arbitrary"))(
            shp(r), shp(k), shp(v), shp(a), shp(lw), vrow(p['rw_k_k']), vrow(p['rw_k_a']),
            vrow(p['rw_r_k']), vrow(p['rw_gn_w']), vrow(p['rw_gn_b']))
    return o.reshape(b * t, D_MODEL), s


def _rwkv_step_body(r_ref, k_ref, vc_ref, a_ref, lw_ref, s_ref, kk_ref, ka_ref, rk_ref, gw_ref, gb_ref,
                    o_ref, so_ref):
    r = r_ref[0]
    k = k_ref[0]
    a = a_ref[0]
    lw = lw_ref[0]
    vc = vc_ref[0]
    s = s_ref[0]
    kkv = k * kk_ref[...]
    kap = kkv * lax.rsqrt(jnp.maximum(jnp.sum(kkv * kkv, axis=-1, keepdims=True), 1e-24))
    kmod = k * (1.0 + (a - 1.0) * ka_ref[...])
    bvec = kap * a
    dec = jnp.exp(lw)
    sa = jnp.sum(s * kap, axis=-1, keepdims=True)
    s_new = s * dec - sa * bvec + vc * kmod
    so_ref[0] = s_new
    o = jnp.sum(s_new * r, axis=-1, keepdims=True)
    mean = jnp.mean(o, axis=1, keepdims=True)
    oc = o - mean
    var = jnp.mean(oc * oc, axis=1, keepdims=True)
    on = oc * lax.rsqrt(var + RW_GN_EPS) * gw_ref[...] + gb_ref[...]
    bonus = jnp.sum(r * kmod * rk_ref[...], axis=-1, keepdims=True)
    o_ref[0] = on + bonus * vc


def _rwkv_step(r, k, v, a, lw, state, p):
    b = r.shape[0]
    hrow = (RW_HEADS, 1, RW_HEAD)
    hcol = (RW_HEADS, RW_HEAD, 1)
    rows = lambda x: x.reshape((b,) + hrow)
    rblk = pl.BlockSpec((1,) + hrow, lambda i: (i, 0, 0, 0))
    cblk = pl.BlockSpec((1,) + hcol, lambda i: (i, 0, 0, 0))
    sblk = pl.BlockSpec((1, RW_HEADS, RW_HEAD, RW_HEAD), lambda i: (i, 0, 0, 0))
    o, s = pl.pallas_call(
        _rwkv_step_body,
        grid=(b,),
        in_specs=[rblk, rblk, cblk, rblk, rblk, sblk] + [_full(hrow)] * 3 + [_full(hcol)] * 2,
        out_specs=[cblk, sblk],
        out_shape=[jax.ShapeDtypeStruct((b,) + hcol, F32),
                   jax.ShapeDtypeStruct((b, RW_HEADS, RW_HEAD, RW_HEAD), F32)],
        compiler_params=_cparams("parallel"))(
            rows(r), rows(k), v.reshape((b,) + hcol), rows(a), rows(lw), state,
            p['rw_k_k'].reshape(hrow), p['rw_k_a'].reshape(hrow), p['rw_r_k'].reshape(hrow),
            p['rw_gn_w'].reshape(hcol), p['rw_gn_b'].reshape(hcol))
    return o.reshape(b, D_MODEL), s


def _mamba_weights(p):
    w = p['mb_w_in']
    wz = w[:, :MB_INNER]
    wx = w[:, MB_INNER:MB_INNER + MB_CONV_DIM]
    wdt = jnp.pad(w[:, MB_INNER + MB_CONV_DIM:], ((0, 0), (0, LANES - MB_HEADS)))
    return [wz, wx, wdt]


def _mamba_vecs(p):
    pad = lambda x: jnp.pad(x.reshape(1, MB_HEADS), ((0, 0), (0, LANES - MB_HEADS)))
    return dict(conv_w=p['mb_conv_w'], conv_b=p['mb_conv_b'].reshape(1, MB_CONV_DIM),
                dt_bias=pad(p['mb_dt_bias']), a_log=pad(p['mb_a_log']),
                d_skip=jnp.repeat(p['mb_d'], MB_HEADDIM).reshape(1, MB_INNER),
                norm_w=p['mb_norm_w'].reshape(1, MB_INNER))


def _mamba_chunk_body(x_ref, xp_ref, z_ref, dt_ref, cw_ref, cb_ref, dtb_ref, alog_ref, dsk_ref, nw_ref,
                      y_ref, fs_ref, st_ref):
    c = pl.program_id(1)
    nc = pl.num_programs(1)
    cl = MB_CHUNK

    @pl.when(c == 0)
    def _():
        st_ref[...] = jnp.zeros_like(st_ref)

    hist = jnp.where(c == 0, 0.0, xp_ref[0])
    xcat = jnp.concatenate([hist, x_ref[0]], axis=0)
    cw = cw_ref[...]
    conv = cb_ref[...] + cw[0:1, :] * xcat[5:5 + cl]
    for j in range(1, MB_CONV):
        conv = conv + cw[j:j + 1, :] * xcat[5 + j:5 + j + cl]
    conv = _silu(conv)

    dt = _softplus(dt_ref[0] + dtb_ref[...])
    da = dt * (-jnp.exp(alog_ref[...]))
    ti = lax.broadcasted_iota(jnp.int32, (cl, cl), 0)
    tj = lax.broadcasted_iota(jnp.int32, (cl, cl), 1)
    causal = ti >= tj
    tri = jnp.where(causal, 1.0, 0.0).astype(BF16)
    a_cs = _cumsum_rows(tri, da)
    a_cs_t = a_cs.T
    a_last = a_cs[cl - 1:cl, :]
    dec_to_end = jnp.exp(a_last - a_cs)
    e_cs = jnp.exp(a_cs)
    e_last = jnp.exp(a_last)

    lane = lax.broadcasted_iota(jnp.int32, (cl, LANES), 1)
    m0 = lane < MB_HEADDIM
    row0 = lax.broadcasted_iota(jnp.int32, (LANES, LANES), 0) < MB_HEADDIM
    heads_per_group = MB_HEADS // MB_GROUPS
    pairs_per_group = heads_per_group // 2
    for g in range(MB_GROUPS):
        bm = conv[:, MB_INNER + g * MB_STATE:MB_INNER + (g + 1) * MB_STATE]
        cm = conv[:, MB_INNER + (MB_GROUPS + g) * MB_STATE:MB_INNER + (MB_GROUPS + g + 1) * MB_STATE]
        cb = _dot_nt(cm, bm)
        blocks = []
        for j in range(pairs_per_group):
            pi = g * pairs_per_group + j
            h0, h1 = 2 * pi, 2 * pi + 1
            lo, hi = pi * LANES, (pi + 1) * LANES
            xs = conv[:, lo:hi]
            dt2 = jnp.where(m0, dt[:, h0:h0 + 1], dt[:, h1:h1 + 1])
            xdt = xs * dt2
            l0 = jnp.exp(jnp.where(causal, a_cs[:, h0:h0 + 1] - a_cs_t[h0:h0 + 1, :], NEG_INF))
            l1 = jnp.exp(jnp.where(causal, a_cs[:, h1:h1 + 1] - a_cs_t[h1:h1 + 1, :], NEG_INF))
            lhs = jnp.concatenate([(cb * l0).astype(BF16), (cb * l1).astype(BF16)], axis=1)
            rhs = jnp.concatenate([jnp.where(m0, xdt, 0.0).astype(BF16),
                                   jnp.where(m0, 0.0, xdt).astype(BF16)], axis=0)
            y = jnp.dot(lhs, rhs, preferred_element_type=F32)
            st = st_ref[pi]
            e2 = jnp.where(m0, e_cs[:, h0:h0 + 1], e_cs[:, h1:h1 + 1])
            y = y + _dot_nt(cm, st) * e2
            d2 = jnp.where(m0, dec_to_end[:, h0:h0 + 1], dec_to_end[:, h1:h1 + 1])
            sc = jnp.where(row0, e_last[:, h0:h0 + 1], e_last[:, h1:h1 + 1])
            st_ref[pi] = st * sc + _dot_tn(xdt * d2, bm)
            y = y + dsk_ref[:, lo:hi] * xs
            blocks.append(y * _silu(z_ref[0, :, lo:hi]))
        ssq = blocks[0] * blocks[0]
        for blk in blocks[1:]:
            ssq = ssq + blk * blk
        scale = lax.rsqrt(jnp.sum(ssq, axis=-1, keepdims=True) * (1.0 / MB_GROUP_WIDTH) + MB_NORM_EPS)
        for j, blk in enumerate(blocks):
            lo = (g * pairs_per_group + j) * LANES
            y_ref[0, :, lo:lo + LANES] = blk * scale * nw_ref[:, lo:lo + LANES]

    @pl.when(c == nc - 1)
    def _():
        fs_ref[0] = st_ref[...]


def _mamba_chunk(z, xbc, dtr, b, t, vecs):
    nc = t // MB_CHUNK
    npairs = MB_HEADS // 2
    per = MB_CHUNK // SUBLANES
    y, fs = pl.pallas_call(
        _mamba_chunk_body,
        grid=(b, nc),
        in_specs=[pl.BlockSpec((1, MB_CHUNK, MB_CONV_DIM), lambda bi, ci: (bi, ci, 0)),
                  pl.BlockSpec((1, SUBLANES, MB_CONV_DIM),
                               lambda bi, ci: (bi, jnp.maximum(ci * per - 1, 0), 0)),
                  pl.BlockSpec((1, MB_CHUNK, MB_INNER), lambda bi, ci: (bi, ci, 0)),
                  pl.BlockSpec((1, MB_CHUNK, LANES), lambda bi, ci: (bi, ci, 0)),
                  _full((MB_CONV, MB_CONV_DIM)), _full((1, MB_CONV_DIM)), _full((1, LANES)),
                  _full((1, LANES)), _full((1, MB_INNER)), _full((1, MB_INNER))],
        out_specs=[pl.BlockSpec((1, MB_CHUNK, MB_INNER), lambda bi, ci: (bi, ci, 0)),
                   pl.BlockSpec((1, npairs, LANES, MB_STATE), lambda bi, ci: (bi, 0, 0, 0))],
        out_shape=[jax.ShapeDtypeStruct((b, t, MB_INNER), F32),
                   jax.ShapeDtypeStruct((b, npairs, LANES, MB_STATE), F32)],
        scratch_shapes=[pltpu.VMEM((npairs, LANES, MB_STATE), F32)],
        compiler_params=_cparams("parallel", "arbitrary"))(
            xbc.reshape(b, t, MB_CONV_DIM), xbc.reshape(b, t, MB_CONV_DIM), z.reshape(b, t, MB_INNER),
            dtr.reshape(b, t, LANES), vecs['conv_w'], vecs['conv_b'], vecs['dt_bias'], vecs['a_log'],
            vecs['d_skip'], vecs['norm_w'])
    return y.reshape(b * t, MB_INNER), fs.reshape(b, MB_HEADS, MB_HEADDIM, MB_STATE)


def _mamba_step_conv_body(x_ref, cs_ref, dt_ref, cw_ref, cb_ref, dtb_ref, alog_ref, conv_ref, dt_o, ed_o):
    cw = cw_ref[...]
    conv = cb_ref[...] + cw[MB_CONV - 1:MB_CONV, :] * x_ref[...]
    for j in range(MB_CONV - 1):
        conv = conv + cw[j:j + 1, :] * cs_ref[:, j, :]
    conv_ref[...] = _silu(conv)
    dt = _softplus(dt_ref[...] + dtb_ref[...])
    dt_o[...] = dt
    ed_o[...] = jnp.exp(dt * (-jnp.exp(alog_ref[...])))


def _mamba_step_state_body(s_ref, xc_ref, dt_ref, ed_ref, bm_ref, cm_ref, so_ref, y_ref):
    hpg = MB_HEADS // MB_GROUPS
    for g in range(MB_GROUPS):
        hs = slice(g * hpg, (g + 1) * hpg)
        s = s_ref[0, hs]
        xdt = xc_ref[0, hs] * dt_ref[0, hs]
        fin = s * ed_ref[0, hs] + xdt * bm_ref[0, g:g + 1]
        so_ref[0, hs] = fin
        y_ref[0, hs] = jnp.sum(fin * cm_ref[0, g:g + 1], axis=-1, keepdims=True)


def _mamba_step_out_body(y_ref, x_ref, z_ref, dsk_ref, nw_ref, o_ref):
    y = (y_ref[...] + dsk_ref[...] * x_ref[...]) * _silu(z_ref[...])
    for g in range(MB_GROUPS):
        lo, hi = g * MB_GROUP_WIDTH, (g + 1) * MB_GROUP_WIDTH
        o_ref[:, lo:hi] = _rms(y[:, lo:hi], nw_ref[:, lo:hi], MB_NORM_EPS)


def _mamba_step(z, xbc, dtr, conv_state, ssm_state, vecs):
    b = z.shape[0]
    conv, dt, ed = pl.pallas_call(
        _mamba_step_conv_body,
        grid=(1,),
        in_specs=[_full((b, MB_CONV_DIM)), _full((b, MB_CONV - 1, MB_CONV_DIM)), _full((b, LANES)),
                  _full((MB_CONV, MB_CONV_DIM)), _full((1, MB_CONV_DIM)), _full((1, LANES)), _full((1, LANES))],
        out_specs=[_full((b, MB_CONV_DIM)), _full((b, LANES)), _full((b, LANES))],
        out_shape=[jax.ShapeDtypeStruct((b, MB_CONV_DIM), F32), jax.ShapeDtypeStruct((b, LANES), F32),
                   jax.ShapeDtypeStruct((b, LANES), F32)],
        compiler_params=_cparams("arbitrary"))(
            xbc, conv_state, dtr, vecs['conv_w'], vecs['conv_b'], vecs['dt_bias'], vecs['a_log'])
    xs = conv[:, :MB_INNER]
    bm = conv[:, MB_INNER:MB_INNER + MB_GROUPS * MB_STATE].reshape(b, MB_GROUPS, MB_STATE)
    cm = conv[:, MB_INNER + MB_GROUPS * MB_STATE:].reshape(b, MB_GROUPS, MB_STATE)
    hp1 = (MB_HEADS, MB_HEADDIM, 1)
    h11 = (MB_HEADS, 1, 1)
    per = lambda shape: pl.BlockSpec((1,) + shape, lambda i: (i,) + (0,) * len(shape))
    st_shape = (MB_HEADS, MB_HEADDIM, MB_STATE)
    fin, y = pl.pallas_call(
        _mamba_step_state_body,
        grid=(b,),
        in_specs=[per(st_shape), per(hp1), per(h11), per(h11), per((MB_GROUPS, MB_STATE)),
                  per((MB_GROUPS, MB_STATE))],
        out_specs=[per(st_shape), per(hp1)],
        out_shape=[jax.ShapeDtypeStruct((b,) + st_shape, F32), jax.ShapeDtypeStruct((b,) + hp1, F32)],
        compiler_params=_cparams("parallel"))(
            ssm_state, xs.reshape((b,) + hp1), dt[:, :MB_HEADS].reshape((b,) + h11),
            ed[:, :MB_HEADS].reshape((b,) + h11), bm, cm)
    yg = pl.pallas_call(
        _mamba_step_out_body,
        grid=(1,),
        in_specs=[_full((b, MB_INNER))] * 3 + [_full((1, MB_INNER))] * 2,
        out_specs=_full((b, MB_INNER)),
        out_shape=jax.ShapeDtypeStruct((b, MB_INNER), F32),
        compiler_params=_cparams("arbitrary"))(
            y.reshape(b, MB_INNER), xs, z, vecs['d_skip'], vecs['norm_w'])
    return yg, fin


def _rope_tables(pos):
    half = ROPE_DIM // 2
    inv_freq = ROPE_THETA ** (-jnp.arange(0, ROPE_DIM, 2, dtype=F32) / ROPE_DIM)
    ang = pos.astype(F32)[:, None] * inv_freq[None, :]
    cos, sin = jnp.cos(ang), jnp.sin(ang)
    n = pos.shape[0]
    ones = jnp.ones((n, DA_HEAD - ROPE_DIM), F32)
    zeros = jnp.zeros((n, DA_HEAD - ROPE_DIM), F32)
    zh = jnp.zeros((n, half), F32)
    ct = jnp.concatenate([cos, cos, ones], axis=1)
    s_up = jnp.concatenate([-sin, zh, zeros], axis=1)
    s_dn = jnp.concatenate([zh, sin, zeros], axis=1)
    two = lambda x: jnp.concatenate([x, x], axis=1)
    return two(ct), two(s_up), two(s_dn)


def _rope(x, ct, s_up, s_dn):
    half = ROPE_DIM // 2
    cols = []
    for j in range(x.shape[1] // LANES):
        xj = x[:, j * LANES:(j + 1) * LANES]
        cols.append(xj * ct + pltpu.roll(xj, LANES - half, axis=1) * s_up + pltpu.roll(xj, half, axis=1) * s_dn)
    return jnp.concatenate(cols, axis=1)


def _da_epilogue(ys, rows, fulls):
    y = ys[0]
    ct, s_up, s_dn = rows
    q = _rope(y[:, :DA_WIDTH], ct, s_up, s_dn)
    k = _rope(y[:, DA_WIDTH:2 * DA_WIDTH], ct, s_up, s_dn)
    return q, k, y[:, 2 * DA_WIDTH:3 * DA_WIDTH], y[:, 3 * DA_WIDTH:]


def _lambda(lq1, lk1, lq2, lk2, lam_init):
    return (jnp.exp(jnp.sum(lq1 * lk1, axis=-1, keepdims=True))
            - jnp.exp(jnp.sum(lq2 * lk2, axis=-1, keepdims=True)) + lam_init)


def _da_epilogue_seq(ys, rows, fulls):
    q, k, v, g = _da_epilogue(ys, rows, fulls)
    return k, v, g, (q * DA_SCALE).T, k, v.T


def _flash_body(lam_init, qt_ref, k_ref, vt_ref, lq1, lk1, lq2, lk2, sw_ref, o_ref,
                m1, l1, m2, l2, acc1, acc2):
    qi = pl.program_id(2)
    kj = pl.program_id(3)

    @pl.when(kj == 0)
    def _():
        for m in (m1, m2):
            m[...] = jnp.full_like(m, NEG_INF)
        for z in (l1, l2, acc1, acc2):
            z[...] = jnp.zeros_like(z)

    def update(masked):
        qt = qt_ref[...]
        k = k_ref[...]
        vt = vt_ref[...]
        tk, tq = k.shape[0], qt.shape[1]
        frow = lax.broadcasted_iota(jnp.int32, qt.shape, 0)
        zero = jnp.zeros_like(qt)
        if masked:
            keep = (lax.broadcasted_iota(jnp.int32, (tk, tq), 0)
                    <= lax.broadcasted_iota(jnp.int32, (tk, tq), 1))
        for comp, (m_ref, l_ref, acc_ref) in enumerate(((m1, l1, acc1), (m2, l2, acc2))):
            qc = jnp.where(frow < DA_HEAD, qt, zero) if comp == 0 else jnp.where(frow < DA_HEAD, zero, qt)
            s = jnp.dot(k, qc, preferred_element_type=F32)
            if masked:
                s = jnp.where(keep, s, NEG_INF)
            m_old = m_ref[...]
            m_new = jnp.maximum(m_old, jnp.max(s, axis=0, keepdims=True))
            alpha = jnp.exp(m_old - m_new)
            pr = jnp.exp(s - m_new)
            l_ref[...] = l_ref[...] * alpha + jnp.sum(pr, axis=0, keepdims=True)
            acc_ref[...] = acc_ref[...] * alpha + jnp.dot(vt, pr.astype(BF16), preferred_element_type=F32)
            m_ref[...] = m_new

    @pl.when(kj < qi)
    def _():
        update(False)

    @pl.when(kj == qi)
    def _():
        update(True)
        lam = _lambda(lq1[...], lk1[...], lq2[...], lk2[...], lam_init)
        ot = acc1[...] / l1[...] - lam * (acc2[...] / l2[...])
        scale = lax.rsqrt(jnp.mean(ot * ot, axis=0, keepdims=True) + DA_SUBLN_EPS)
        o_ref[...] = (ot * scale * (sw_ref[...] * (1.0 - lam_init))).T


def _flash(qt, kb, vt, b, t, p, lam_init):
    tq = min(t, 512)
    nq = t // tq
    qblk = pl.BlockSpec((DA_VDIM, tq), lambda bi, hi, qi, kj: (hi, bi * nq + qi))
    vblk = pl.BlockSpec((DA_VDIM, tq), lambda bi, hi, qi, kj: (hi, bi * nq + jnp.minimum(kj, qi)))
    kblk = pl.BlockSpec((tq, DA_VDIM), lambda bi, hi, qi, kj: (bi * nq + jnp.minimum(kj, qi), hi))
    oblk = pl.BlockSpec((tq, DA_VDIM), lambda bi, hi, qi, kj: (bi * nq + qi, hi))
    vec = lambda x: x.reshape(1, -1)
    stat = pltpu.VMEM((1, tq), F32)
    acc = pltpu.VMEM((DA_VDIM, tq), F32)
    return pl.pallas_call(
        functools.partial(_flash_body, lam_init),
        grid=(b, DA_HEADS, nq, nq),
        in_specs=[qblk, kblk, vblk] + [_full((1, DA_HEAD))] * 4 + [_full((DA_VDIM, 1))],
        out_specs=oblk,
        out_shape=jax.ShapeDtypeStruct((b * t, DA_WIDTH), F32),
        scratch_shapes=[stat, stat, stat, stat, acc, acc],
        compiler_params=_cparams("parallel", "parallel", "parallel", "arbitrary"))(
            qt, kb, vt, vec(p['da_lq1']), vec(p['da_lk1']), vec(p['da_lq2']), vec(p['da_lk2']),
            p['da_subln_w'].reshape(DA_VDIM, 1))


def _decode_body(lam_init, npg, pt_ref, q_ref, kn_ref, vn_ref, *refs):
    ck_refs = refs[:npg]
    cv_refs = refs[npg:2 * npg]
    lq1, lk1, lq2, lk2, sw_ref, o_ref, qm_s, m_s, l_s, acc = refs[2 * npg:]
    j = pl.program_id(1)
    nj = pl.num_programs(1)
    nrow = 2 * DA_HEADS
    ncol = PAGE_SIZE * DA_HEADS
    rid = lax.broadcasted_iota(jnp.int32, (nrow, LANES), 0)

    @pl.when(j == 0)
    def _():
        comp = lax.broadcasted_iota(jnp.int32, (nrow, LANES), 1) // DA_HEAD
        qm = jnp.where(jnp.bitwise_and(rid, 1) == comp, q_ref[0] * DA_SCALE, 0.0)
        qm_s[...] = qm
        m_s[...] = jnp.sum(qm * kn_ref[0], axis=-1, keepdims=True)
        l_s[...] = jnp.ones_like(l_s)
        acc[...] = vn_ref[0]

    qm = qm_s[...]
    keep = (lax.rem(lax.broadcasted_iota(jnp.int32, (nrow, ncol), 1), DA_HEADS)
            == lax.broadcasted_iota(jnp.int32, (nrow, ncol), 0) // 2)
    scores = [jnp.where(keep, _dot_nt(qm, ck[0].reshape(ncol, DA_VDIM)), NEG_INF) for ck in ck_refs]
    m_old = m_s[...]
    m_new = m_old
    for s in scores:
        m_new = jnp.maximum(m_new, jnp.max(s, axis=-1, keepdims=True))
    alpha = jnp.exp(m_old - m_new)
    l_new = l_s[...] * alpha
    a_new = acc[...] * alpha
    for s, cv in zip(scores, cv_refs):
        pr = jnp.exp(s - m_new)
        l_new = l_new + jnp.sum(pr, axis=-1, keepdims=True)
        a_new = a_new + _dot(pr, cv[0].reshape(ncol, DA_VDIM))
    m_s[...] = m_new
    l_s[...] = l_new
    acc[...] = a_new

    @pl.when(j == nj - 1)
    def _():
        lam = _lambda(lq1[...], lk1[...], lq2[...], lk2[...], lam_init)
        d = (a_new / l_new) * jnp.where(jnp.bitwise_and(rid, 1) == 0, 1.0, -lam)
        pair = d + pltpu.roll(d, nrow - 1, axis=0)
        o_ref[0] = _rms(pair, sw_ref[...], DA_SUBLN_EPS) * (1.0 - lam_init)


def _decode(q, k_new, v_new, cache_k, cache_v, page_table, p, lam_init):
    b = q.shape[0]
    n_pages = page_table.shape[1]
    npg = DEC_PAGES if n_pages % DEC_PAGES == 0 else 1
    nrow = 2 * DA_HEADS
    rows16 = lambda x: jnp.repeat(x.reshape(b, DA_HEADS, DA_VDIM), 2, axis=1)
    row = pl.BlockSpec((1, nrow, DA_VDIM), lambda bi, j, pt: (bi, 0, 0))
    pages = [pl.BlockSpec((1, PAGE_SIZE, DA_HEADS, DA_VDIM),
                          lambda bi, j, pt, i=i: (pt[bi, j * npg + i], 0, 0, 0)) for i in range(npg)]
    vec = lambda x: x.reshape(1, -1)
    small = lambda n: pl.BlockSpec((1, n), lambda bi, j, pt: (0, 0))
    o = pl.pallas_call(
        functools.partial(_decode_body, lam_init, npg),
        grid_spec=pltpu.PrefetchScalarGridSpec(
            num_scalar_prefetch=1, grid=(b, n_pages // npg),
            in_specs=[row, row, row] + pages + pages + [small(DA_HEAD)] * 4 + [small(DA_VDIM)],
            out_specs=row,
            scratch_shapes=[pltpu.VMEM((nrow, DA_VDIM), F32), pltpu.VMEM((nrow, 1), F32),
                            pltpu.VMEM((nrow, 1), F32), pltpu.VMEM((nrow, DA_VDIM), F32)]),
        out_shape=jax.ShapeDtypeStruct((b, nrow, DA_VDIM), F32),
        compiler_params=_cparams("parallel", "arbitrary"))(
            page_table, rows16(q), rows16(k_new), rows16(v_new), *([cache_k] * npg), *([cache_v] * npg),
            vec(p['da_lq1']), vec(p['da_lk1']), vec(p['da_lq2']), vec(p['da_lk2']), vec(p['da_subln_w']))
    return o[:, ::2, :].reshape(b, DA_WIDTH)


def _cf_epilogue(ys, rows, fulls):
    y = ys[0]
    u = y[:, :CF_WIDTH] * _sigmoid(y[:, CF_WIDTH:2 * CF_WIDTH])
    return u, y[:, 2 * CF_WIDTH:]


def _cf_conv_body(tc, u_ref, up_ref, cw_ref, cb_ref, lw_ref, lb_ref, o_ref, buf, sh):
    c = pl.program_id(1)
    buf[0:CF_HIST, :] = jnp.where(c == 0, 0.0, up_ref[0])
    buf[CF_HIST:CF_HIST + tc, :] = u_ref[0]
    off = CF_HIST - (CF_KERNEL - 1)
    span = tc + CF_HIST - SUBLANES
    for r in range(1, SUBLANES):
        sh[r - 1, 0:span, :] = buf[r:r + span, :]
    rb = CF_ROWS
    for i in range(tc // rb):
        acc = cb_ref[...]
        for j in range(CF_KERNEL):
            r, base = (off + j) % SUBLANES, (off + j) // SUBLANES * SUBLANES
            rows = slice(base + i * rb, base + (i + 1) * rb)
            acc = acc + cw_ref[j:j + 1, :] * (buf[rows, :] if r == 0 else sh[r - 1, rows, :])
        mu = jnp.mean(acc, axis=-1, keepdims=True)
        xc = acc - mu
        var = jnp.mean(xc * xc, axis=-1, keepdims=True)
        o_ref[0, i * rb:(i + 1) * rb, :] = _silu(xc * lax.rsqrt(var + CF_LN_EPS) * lw_ref[...] + lb_ref[...])


def _cf_conv(u, b, t, p):
    tc = min(t, 256)
    per = tc // CF_HIST
    blk = pl.BlockSpec((1, tc, CF_WIDTH), lambda bi, ci: (bi, ci, 0))
    prev = pl.BlockSpec((1, CF_HIST, CF_WIDTH), lambda bi, ci: (bi, jnp.maximum(ci * per - 1, 0), 0))
    vec = lambda x: x.reshape(1, CF_WIDTH)
    u3 = u.reshape(b, t, CF_WIDTH)
    o = pl.pallas_call(
        functools.partial(_cf_conv_body, tc),
        grid=(b, t // tc),
        in_specs=[blk, prev, _full((CF_KERNEL, CF_WIDTH))] + [_full((1, CF_WIDTH))] * 3,
        out_specs=blk,
        out_shape=jax.ShapeDtypeStruct((b, t, CF_WIDTH), F32),
        scratch_shapes=[pltpu.VMEM((CF_HIST + tc, CF_WIDTH), F32),
                        pltpu.VMEM((SUBLANES - 1, CF_HIST + tc - SUBLANES, CF_WIDTH), F32)],
        compiler_params=_cparams("parallel", "parallel"))(
            u3, u3, p['cf_conv_w'], vec(p['cf_conv_b']), vec(p['cf_ln_w']), vec(p['cf_ln_b']))
    return o.reshape(b * t, CF_WIDTH)


def _cf_step_body(u_ref, st_ref, cw_ref, cb_ref, lw_ref, lb_ref, o_ref):
    nb = u_ref.shape[0]
    cw = cw_ref[...]
    hist = jnp.concatenate([jnp.sum(st_ref[i] * cw[:CF_KERNEL - 1, :], axis=0, keepdims=True)
                            for i in range(nb)], axis=0)
    acc = cb_ref[...] + hist + cw[CF_KERNEL - 1:CF_KERNEL, :] * u_ref[...]
    mu = jnp.mean(acc, axis=-1, keepdims=True)
    xc = acc - mu
    var = jnp.mean(xc * xc, axis=-1, keepdims=True)
    o_ref[...] = _silu(xc * lax.rsqrt(var + CF_LN_EPS) * lw_ref[...] + lb_ref[...])


def _cf_step(u, state, p):
    b = u.shape[0]
    tb = SUBLANES
    vec = lambda x: x.reshape(1, CF_WIDTH)
    return pl.pallas_call(
        _cf_step_body,
        grid=(b // tb,),
        in_specs=[pl.BlockSpec((tb, CF_WIDTH), lambda i: (i, 0)),
                  pl.BlockSpec((tb, CF_KERNEL - 1, CF_WIDTH), lambda i: (i, 0, 0)),
                  _full((CF_KERNEL, CF_WIDTH))] + [_full((1, CF_WIDTH))] * 3,
        out_specs=pl.BlockSpec((tb, CF_WIDTH), lambda i: (i, 0)),
        out_shape=jax.ShapeDtypeStruct((b, CF_WIDTH), F32),
        compiler_params=_cparams("parallel"))(
            u, state, p['cf_conv_w'], vec(p['cf_conv_b']), vec(p['cf_ln_w']), vec(p['cf_ln_b']))


def _forward(x_prompt, x_sample, state_rwkv_shift, state_rwkv_wkv, state_mamba_conv, state_mamba_ssm,
             cache_k, cache_v, page_table, state_conformer_conv, p):
    bp, tp, d = x_prompt.shape
    bs = x_sample.shape[0]
    hp = x_prompt.reshape(bp * tp, d)
    hs = x_sample.reshape(bs, d)
    npre, npost = p['norm_pre'], p['norm_post']

    (r, k, v, g, a, lw), rw_shift_p = _rwkv_proj_seq(hp, bp, tp, npre[0], p)
    o, rw_wkv_p = _rwkv_chunk(r, k, v, a, lw, bp, tp, p)
    hp = _out_proj(o, g, hp, p['rw_w_o'], npost[0])
    (r, k, v, g, a, lw), rw_shift_s = _rwkv_proj_step(hs, state_rwkv_shift, npre[0], p)
    o, rw_wkv_s = _rwkv_step(r, k, v, a, lw, state_rwkv_wkv, p)
    hs = _out_proj(o, g, hs, p['rw_w_o'], npost[0])

    mw = _mamba_weights(p)
    mv = _mamba_vecs(p)
    ident = lambda ys, rows, fulls: ys
    widths = (MB_INNER, MB_CONV_DIM, LANES)
    z, xbc, dtr = _in_proj(hp, npre[1], mw, ident, widths, tm=256)
    yg, mb_ssm_p = _mamba_chunk(z, xbc, dtr, bp, tp, mv)
    mb_conv_p = xbc.reshape(bp, tp, MB_CONV_DIM)[:, tp - (MB_CONV - 1):]
    hp = _out_proj(yg, None, hp, p['mb_w_out'], npost[1])
    z, xbc, dtr = _in_proj(hs, npre[1], mw, ident, widths)
    yg, mb_ssm_s = _mamba_step(z, xbc, dtr, state_mamba_conv, state_mamba_ssm, mv)
    mb_conv_s = jnp.concatenate([state_mamba_conv[:, 1:], xbc[:, None, :]], axis=1)
    hs = _out_proj(yg, None, hs, p['mb_w_out'], npost[1])

    lam_init = 0.8 - 0.6 * math.exp(-0.3 * 2)
    tabs_p = _rope_tables(jnp.arange(tp, dtype=jnp.int32))
    tm = min(_row_tile(tp), 256)
    tps = tp // tm
    rows_p = [(tb, lambda i: (i % tps, 0), LANES) for tb in tabs_p]
    widths = (DA_WIDTH,) * 3 + ((DA_WIDTH, BF16, True), (DA_WIDTH, BF16, False), (DA_WIDTH, BF16, True))
    k, v, g, qt, kb, vt = _in_proj(hp, npre[2], [p['da_w_qkvg']], _da_epilogue_seq, widths,
                                   row_extras=rows_p, tm=tm)
    k_rows_p = k.reshape(bp, tp, DA_HEADS, 2 * DA_HEAD)
    v_rows_p = v.reshape(bp, tp, DA_HEADS, DA_VDIM)
    o = _flash(qt, kb, vt, bp, tp, p, lam_init)
    hp = _out_proj(o, g, hp, p['da_w_o'], npost[2])
    tabs_s = _rope_tables(jnp.full((bs,), PAST_LEN, dtype=jnp.int32))
    rows_s = [(tb, lambda i: (i, 0), LANES) for tb in tabs_s]
    q, k, v, g = _in_proj(hs, npre[2], [p['da_w_qkvg']], _da_epilogue, (DA_WIDTH,) * 4, row_extras=rows_s)
    k_rows_s = k.reshape(bs, 1, DA_HEADS, 2 * DA_HEAD)
    v_rows_s = v.reshape(bs, 1, DA_HEADS, DA_VDIM)
    o = _decode(q, k, v, cache_k, cache_v, page_table, p, lam_init)
    hs = _out_proj(o, g, hs, p['da_w_o'], npost[2])

    u, g = _in_proj(hp, npre[3], [p['cf_w_in']], _cf_epilogue, (CF_WIDTH,) * 2)
    cf_conv_p = u.reshape(bp, tp, CF_WIDTH)[:, tp - (CF_KERNEL - 1):]
    c = _cf_conv(u, bp, tp, p)
    hp = _out_proj(c, g, hp, p['cf_w_out'], npost[3])
    u, g = _in_proj(hs, npre[3], [p['cf_w_in']], _cf_epilogue, (CF_WIDTH,) * 2)
    cf_conv_s = jnp.concatenate([state_conformer_conv[:, 1:], u[:, None, :]], axis=1)
    c = _cf_step(u, state_conformer_conv, p)
    hs = _out_proj(c, g, hs, p['cf_w_out'], npost[3])

    return (hp.reshape(bp, tp, d), hs.reshape(bs, 1, d), rw_shift_p, rw_shift_s, rw_wkv_p, rw_wkv_s,
            mb_conv_p, mb_conv_s, mb_ssm_p, mb_ssm_s, k_rows_p, k_rows_s, v_rows_p, v_rows_s,
            cf_conv_p, cf_conv_s)


def kernel(x_prompt, x_sample, state_rwkv_shift, state_rwkv_wkv, state_mamba_conv, state_mamba_ssm, cache_k, cache_v, page_table, state_conformer_conv, norm_pre, norm_post, rw_mu, rw_w_r, rw_w_k, rw_w_v, rw_w_g, rw_w0, rw_w_w1, rw_w_w2, rw_a0, rw_w_a1, rw_w_a2, rw_k_k, rw_k_a, rw_r_k, rw_gn_w, rw_gn_b, rw_w_o, mb_w_in, mb_conv_w, mb_conv_b, mb_dt_bias, mb_a_log, mb_d, mb_norm_w, mb_w_out, da_w_qkvg, da_lq1, da_lk1, da_lq2, da_lk2, da_subln_w, da_w_o, cf_w_in, cf_conv_w, cf_conv_b, cf_ln_w, cf_ln_b, cf_w_out):
    p = dict(norm_pre=norm_pre, norm_post=norm_post, rw_mu=rw_mu, rw_w_r=rw_w_r, rw_w_k=rw_w_k, rw_w_v=rw_w_v,
             rw_w_g=rw_w_g, rw_w0=rw_w0, rw_w_w1=rw_w_w1, rw_w_w2=rw_w_w2, rw_a0=rw_a0, rw_w_a1=rw_w_a1,
             rw_w_a2=rw_w_a2, rw_k_k=rw_k_k, rw_k_a=rw_k_a, rw_r_k=rw_r_k, rw_gn_w=rw_gn_w, rw_gn_b=rw_gn_b,
             rw_w_o=rw_w_o, mb_w_in=mb_w_in, mb_conv_w=mb_conv_w, mb_conv_b=mb_conv_b, mb_dt_bias=mb_dt_bias,
             mb_a_log=mb_a_log, mb_d=mb_d, mb_norm_w=mb_norm_w, mb_w_out=mb_w_out, da_w_qkvg=da_w_qkvg,
             da_lq1=da_lq1, da_lk1=da_lk1, da_lq2=da_lq2, da_lk2=da_lk2, da_subln_w=da_subln_w, da_w_o=da_w_o,
             cf_w_in=cf_w_in, cf_conv_w=cf_conv_w, cf_conv_b=cf_conv_b, cf_ln_w=cf_ln_w, cf_ln_b=cf_ln_b,
             cf_w_out=cf_w_out)
    return _forward(x_prompt, x_sample, state_rwkv_shift, state_rwkv_wkv, state_mamba_conv, state_mamba_ssm,
                    cache_k, cache_v, page_table, state_conformer_conv, p)
```

```python
import functools
import math

import jax
import jax.numpy as jnp
from jax import lax
from jax.experimental import pallas as pl
from jax.experimental.pallas import tpu as pltpu

F32 = jnp.float32
BF16 = jnp.bfloat16

D_MODEL = 1024
PAST_LEN = 8192
PAGE_SIZE = 128
RMS_EPS = 1e-6

RW_HEAD = 64
RW_HEADS = D_MODEL // RW_HEAD
RW_LORA = 64
RW_GN_EPS = 64e-5
RW_CHUNK = 64
RW_GROUP = 4
RW_PAIRS = 2

MB_INNER = 2 * D_MODEL
MB_HEADDIM = 64
MB_HEADS = MB_INNER // MB_HEADDIM
MB_GROUPS = 4
MB_STATE = 128
MB_CONV = 4
MB_CHUNK = 128
MB_CONV_DIM = MB_INNER + 2 * MB_GROUPS * MB_STATE
MB_NORM_EPS = 1e-5
MB_GROUP_WIDTH = MB_INNER // MB_GROUPS

DA_HEADS = 8
DA_HEAD = 64
DA_VDIM = 2 * DA_HEAD
DA_WIDTH = DA_HEADS * DA_VDIM
DA_SCALE = DA_HEAD ** -0.5
DA_SUBLN_EPS = 1e-5
ROPE_THETA = 500000.0
ROPE_DIM = DA_HEAD // 4
DEC_PAGES = 8

CF_WIDTH = D_MODEL
CF_KERNEL = 31
CF_LN_EPS = 1e-5
CF_HIST = 32
CF_ROWS = 32

LANES = 128
SUBLANES = 8
VMEM_LIMIT = 56 * 1024 * 1024

NEG_INF = float("-inf")


def _cparams(*sem):
    return pltpu.CompilerParams(dimension_semantics=sem, vmem_limit_bytes=VMEM_LIMIT)


def _dot(a, b):
    return jnp.dot(a.astype(BF16), b.astype(BF16), preferred_element_type=F32)


def _dot_nt(a, b):
    return lax.dot_general(a.astype(BF16), b.astype(BF16), (((1,), (1,)), ((), ())),
                           preferred_element_type=F32)


def _dot_tn(a, b):
    return lax.dot_general(a.astype(BF16), b.astype(BF16), (((0,), (0,)), ((), ())),
                           preferred_element_type=F32)


def _split2(a):
    hi = a.astype(BF16)
    lo = (a - hi.astype(F32)).astype(BF16)
    return hi, lo


def _split3(a):
    hi = a.astype(BF16)
    r1 = a - hi.astype(F32)
    mid = r1.astype(BF16)
    lo = (r1 - mid.astype(F32)).astype(BF16)
    return hi, mid, lo


def _dotx(a, b, dn=(((1,), (0,)), ((), ()))):
    ah, al = _split2(a)
    bh, bl = _split2(b)
    f = functools.partial(lax.dot_general, dimension_numbers=dn, preferred_element_type=F32)
    return f(ah, bh) + (f(ah, bl) + f(al, bh))


_NT = (((1,), (1,)), ((), ()))
_TN = (((0,), (0,)), ((), ()))


def _cumsum_rows(tri_bf16, x):
    hi, mid, lo = _split3(x)
    f = functools.partial(jnp.dot, preferred_element_type=F32)
    return f(tri_bf16, hi) + (f(tri_bf16, mid) + f(tri_bf16, lo))


def _sigmoid(x):
    return 1.0 / (1.0 + jnp.exp(-x))


def _silu(x):
    return x * _sigmoid(x)


def _softplus(x):
    return jnp.maximum(x, 0.0) + jnp.log(1.0 + jnp.exp(-jnp.abs(x)))


def _rms(x, w, eps):
    return x * lax.rsqrt(jnp.mean(x * x, axis=-1, keepdims=True) + eps) * w


def _row_tile(m):
    for t in (512, 256, 128, 64, 32, 16, 8):
        if m % t == 0:
            return t
    return m


def _full(shape):
    nd = len(shape)
    return pl.BlockSpec(shape, lambda *_: (0,) * nd)


def _out_proj_body(has_gate, *refs):
    if has_gate:
        y_ref, g_ref, h_ref, w_ref, pw_ref, o_ref = refs
        y = y_ref[...] * _silu(g_ref[...])
    else:
        y_ref, h_ref, w_ref, pw_ref, o_ref = refs
        y = y_ref[...]
    z = _dot(y, w_ref[...])
    o_ref[...] = h_ref[...] + _rms(z, pw_ref[...], RMS_EPS)


def _out_proj(y, gate, h, w, post_w):
    m, kdim = y.shape
    tm = _row_tile(m)
    row = lambda width: pl.BlockSpec((tm, width), lambda i: (i, 0))
    args, specs = [y], [row(kdim)]
    if gate is not None:
        args.append(gate)
        specs.append(row(D_MODEL))
    args += [h, w.astype(BF16), post_w.reshape(1, D_MODEL)]
    specs += [row(D_MODEL), _full((kdim, D_MODEL)), _full((1, D_MODEL))]
    return pl.pallas_call(
        functools.partial(_out_proj_body, gate is not None),
        grid=(m // tm,), in_specs=specs, out_specs=row(D_MODEL),
        out_shape=jax.ShapeDtypeStruct((m, D_MODEL), F32),
        compiler_params=_cparams("parallel"))(*args)


def _in_proj_body(n_w, n_row, n_full, epilogue, *refs):
    h_ref, nw_ref = refs[0], refs[1]
    w_refs = refs[2:2 + n_w]
    row_refs = refs[2 + n_w:2 + n_w + n_row]
    full_refs = refs[2 + n_w + n_row:2 + n_w + n_row + n_full]
    out_refs = refs[2 + n_w + n_row + n_full:]
    xn = _rms(h_ref[...], nw_ref[...], RMS_EPS).astype(BF16)
    ys = [jnp.dot(xn, w[...], preferred_element_type=F32) for w in w_refs]
    outs = epilogue(ys, [r[...] for r in row_refs], [r[...] for r in full_refs])
    for o_ref, o in zip(out_refs, outs):
        o_ref[...] = o.astype(o_ref.dtype).reshape(o_ref.shape)


def _in_proj(h, norm_w, weights, epilogue, out_widths, row_extras=(), full_extras=(), tm=None):
    m = h.shape[0]
    tm = tm or _row_tile(m)
    row = lambda width: pl.BlockSpec((tm, width), lambda i: (i, 0))
    args = [h, norm_w.reshape(1, D_MODEL)] + [w.astype(BF16) for w in weights]
    specs = [row(D_MODEL), _full((1, D_MODEL))] + [_full(w.shape) for w in weights]
    for arr, imap, width in row_extras:
        args.append(arr)
        specs.append(pl.BlockSpec((tm, width), imap))
    for arr in full_extras:
        args.append(arr)
        specs.append(_full(arr.shape))
    out_specs, out_shape = [], []
    for wd in out_widths:
        width, dtype, transposed = wd if isinstance(wd, tuple) else (wd, F32, False)
        if transposed == 'heads':
            shape3 = (width // LANES, LANES)
            out_specs.append(pl.BlockSpec((tm,) + shape3, lambda i: (i, 0, 0)))
            out_shape.append(jax.ShapeDtypeStruct((m,) + shape3, dtype))
        elif transposed:
            out_specs.append(pl.BlockSpec((width, tm), lambda i: (0, i)))
            out_shape.append(jax.ShapeDtypeStruct((width, m), dtype))
        else:
            out_specs.append(row(width))
            out_shape.append(jax.ShapeDtypeStruct((m, width), dtype))
    return pl.pallas_call(
        functools.partial(_in_proj_body, len(weights), len(row_extras), len(full_extras), epilogue),
        grid=(m // tm,), in_specs=specs, out_specs=out_specs, out_shape=out_shape,
        compiler_params=_cparams("parallel"))(*args)


def _rwkv_mix(xn, xprev, mu, wr, wk, wv, wg, w0, ww1, ww2, a0, wa1, wa2):
    xx = xprev - xn
    xm = [xn + xx * mu[i:i + 1, :] for i in range(6)]
    r = _dot(xm[0], wr)
    k = _dot(xm[2], wk)
    v = _dot(xm[3], wv)
    g = _dot(xm[5], wg)
    zw = w0 + _dot(jnp.tanh(_dot(xm[1], ww1)), ww2)
    w_log = -_softplus(-zw) - 0.5
    lw = -jnp.exp(w_log)
    a = _sigmoid(a0 + _dot(_dot(xm[4], wa1), wa2))
    return r, k, v, g, a, lw


def _rwkv_proj_seq_body(tiles_per_seq, h_ref, hp_ref, nw_ref, mu_ref, wr, wk, wv, wg, w0, ww1, ww2,
                        a0, wa1, wa2, r_ref, k_ref, v_ref, g_ref, a_ref, lw_ref, last_ref):
    i = pl.program_id(0)
    nw = nw_ref[...]
    xn = _rms(h_ref[...], nw, RMS_EPS)
    tm = xn.shape[0]
    pn = _rms(hp_ref[...], nw, RMS_EPS)[SUBLANES - 1:SUBLANES, :]
    first = (i % tiles_per_seq) == 0
    prow = jnp.where(first, 0.0, pn)
    rows = lax.broadcasted_iota(jnp.int32, xn.shape, 0)
    xprev = jnp.where(rows == 0, prow, pltpu.roll(xn, 1, axis=0))
    outs = _rwkv_mix(xn, xprev, mu_ref[...], wr[...], wk[...], wv[...], wg[...], w0[...], ww1[...],
                     ww2[...], a0[...], wa1[...], wa2[...])
    for o_ref, o in zip((r_ref, k_ref, v_ref, g_ref, a_ref, lw_ref), outs):
        o_ref[...] = o
    last_ref[0] = xn[tm - 1:tm, :]


def _rwkv_proj_step_body(h_ref, sh_ref, nw_ref, mu_ref, wr, wk, wv, wg, w0, ww1, ww2,
                         a0, wa1, wa2, r_ref, k_ref, v_ref, g_ref, a_ref, lw_ref, last_ref):
    xn = _rms(h_ref[...], nw_ref[...], RMS_EPS)
    outs = _rwkv_mix(xn, sh_ref[...], mu_ref[...], wr[...], wk[...], wv[...], wg[...], w0[...], ww1[...],
                     ww2[...], a0[...], wa1[...], wa2[...])
    for o_ref, o in zip((r_ref, k_ref, v_ref, g_ref, a_ref, lw_ref), outs):
        o_ref[...] = o
    last_ref[...] = xn


def _rwkv_weights(p):
    d = D_MODEL
    vec = lambda x: x.reshape(1, d)
    args = [p['rw_mu'], p['rw_w_r'].astype(BF16), p['rw_w_k'].astype(BF16), p['rw_w_v'].astype(BF16),
            p['rw_w_g'].astype(BF16), vec(p['rw_w0']), p['rw_w_w1'].astype(BF16), p['rw_w_w2'].astype(BF16),
            vec(p['rw_a0']), p['rw_w_a1'].astype(BF16), p['rw_w_a2'].astype(BF16)]
    return args, [_full(a.shape) for a in args]


def _rwkv_proj_seq(h2, b, t, norm_w, p):
    m = b * t
    tm = min(_row_tile(t), 256)
    tps = t // tm
    row = pl.BlockSpec((tm, D_MODEL), lambda i: (i, 0))
    prev = pl.BlockSpec((SUBLANES, D_MODEL), lambda i: (jnp.maximum(i * (tm // SUBLANES) - 1, 0), 0))
    wargs, wspecs = _rwkv_weights(p)
    outs = pl.pallas_call(
        functools.partial(_rwkv_proj_seq_body, tps),
        grid=(m // tm,),
        in_specs=[row, prev, _full((1, D_MODEL))] + wspecs,
        out_specs=[row] * 6 + [pl.BlockSpec((1, 1, D_MODEL), lambda i: (i // tps, 0, 0))],
        out_shape=[jax.ShapeDtypeStruct((m, D_MODEL), F32)] * 6
        + [jax.ShapeDtypeStruct((b, 1, D_MODEL), F32)],
        compiler_params=_cparams("arbitrary"))(h2, h2, norm_w.reshape(1, D_MODEL), *wargs)
    return outs[:6], outs[6].reshape(b, D_MODEL)


def _rwkv_proj_step(h2, shift, norm_w, p):
    m = h2.shape[0]
    wargs, wspecs = _rwkv_weights(p)
    outs = pl.pallas_call(
        _rwkv_proj_step_body,
        grid=(1,),
        in_specs=[_full((m, D_MODEL)), _full((m, D_MODEL)), _full((1, D_MODEL))] + wspecs,
        out_specs=[_full((m, D_MODEL))] * 7,
        out_shape=[jax.ShapeDtypeStruct((m, D_MODEL), F32)] * 7,
        compiler_params=_cparams("arbitrary"))(h2, shift, norm_w.reshape(1, D_MODEL), *wargs)
    return outs[:6], outs[6]


def _pair_sum(x, m0):
    s0 = jnp.sum(jnp.where(m0, x, 0.0), axis=-1, keepdims=True)
    s1 = jnp.sum(jnp.where(m0, 0.0, x), axis=-1, keepdims=True)
    return jnp.where(m0, s0, s1)


def _rwkv_chunk_body(gsz, npair, r_ref, k_ref, v_ref, a_ref, lw_ref, kk_ref, ka_ref, rk_ref, gw_ref, gb_ref,
                     o_ref, s_ref, bd_ref):
    cg = pl.program_id(2)
    ncg = pl.num_programs(2)
    cl = RW_CHUNK
    two = 2 * cl

    @pl.when(cg == 0)
    def _():
        bd_ref[...] = jnp.zeros_like(bd_ref)

    lane = lax.broadcasted_iota(jnp.int32, (cl, LANES), 1)
    m0 = lane < RW_HEAD
    ti = lax.broadcasted_iota(jnp.int32, (cl, cl), 0)
    tj = lax.broadcasted_iota(jnp.int32, (cl, cl), 1)
    tri = jnp.where(ti >= tj, 1.0, 0.0).astype(BF16)
    row2 = lax.broadcasted_iota(jnp.int32, (two, LANES), 0)
    col2 = lax.broadcasted_iota(jnp.int32, (two, LANES), 1)
    same = (row2 >= cl) == (col2 >= RW_HEAD)
    step_r = jnp.bitwise_and(row2, cl - 1)
    step_c = jnp.bitwise_and(col2, cl - 1)
    bd_strict = jnp.logical_and(same, step_r > step_c)
    bd_incl = jnp.logical_and(same, step_r >= step_c)
    eye = row2 == col2

    def stack(x):
        return jnp.where(same, jnp.concatenate([x, x], axis=0), 0.0)

    chunks = range(gsz)
    st = []
    for pi in range(npair):
        cols = slice(pi * LANES, (pi + 1) * LANES)
        kk_w, ka_w, rk_w = kk_ref[:, cols], ka_ref[:, cols], rk_ref[:, cols]
        for gi in chunks:
            sl = slice(gi * cl, (gi + 1) * cl)
            r = r_ref[0, sl, cols]
            k = k_ref[0, sl, cols]
            v = v_ref[0, sl, cols]
            a = a_ref[0, sl, cols]
            lw = lw_ref[0, sl, cols]
            kkv = k * kk_w
            kap = kkv * lax.rsqrt(jnp.maximum(_pair_sum(kkv * kkv, m0), 1e-24))
            kmod = k * (1.0 + (a - 1.0) * ka_w)
            st.append(dict(r=r, v=v, lw=lw, kap=kap, kmod=kmod, bvec=kap * a,
                           bonus=_pair_sum(r * kmod * rk_w, m0) * v))
    for s in st:
        s['cum'] = _cumsum_rows(tri, s['lw'])
    for s in st:
        cum = s['cum']
        cum_last = cum[cl - 1:cl, :]
        g_inv = jnp.exp(-cum)
        g_end = jnp.exp(cum_last - cum)
        s['g_all'] = jnp.exp(cum_last)
        s['kapm'] = stack(s['kap'] * jnp.exp(cum - s['lw']))
        s['rm'] = stack(s['r'] * jnp.exp(cum))
        s['v2m'] = stack(s['v'])
        s['b_h'] = s['bvec'] * g_inv
        s['k_h'] = s['kmod'] * g_inv
        s['b_e'] = stack(s['bvec'] * g_end)
        s['k_e'] = stack(s['kmod'] * g_end)
    for s in st:
        lhs = jnp.concatenate([s['kapm'], s['rm']], axis=0)
        gb = _dot_nt(lhs, jnp.concatenate([s['b_h'], s['b_h']], axis=0))
        gk = _dot_nt(lhs, jnp.concatenate([s['k_h'], s['k_h']], axis=0))
        s['a_ab'] = jnp.where(bd_strict, gb[:two], 0.0)
        s['a_rb'] = jnp.where(bd_incl, gb[two:], 0.0)
        s['a_ak'] = jnp.where(bd_strict, gk[:two], 0.0)
        s['a_rk'] = jnp.where(bd_incl, gk[two:], 0.0)
    for s in st:
        s['tinv'] = jnp.where(eye, 1.0, 0.0) - s['a_ab']
        s['lp'] = _dot(s['a_ab'], s['a_ab'])
        s['x2m'] = _dot(s['a_ak'], s['v2m'])
    for _ in range(4):
        for s in st:
            both = _dot(jnp.concatenate([s['tinv'], s['lp']], axis=0), s['lp'])
            s['tinv'] = s['tinv'] + both[:two]
            s['lp'] = both[two:]
    for s in st:
        s['tinv'] = s['tinv'] + _dot(s['tinv'], s['lp'])
    for s in st:
        pq = _dot(s['tinv'], jnp.concatenate([s['kapm'], s['x2m']], axis=1))
        s['pm'] = -pq[:, :LANES]
        s['qm'] = -pq[:, LANES:]
    for s in st:
        ro = _dot(s['a_rb'], jnp.concatenate([s['pm'], s['qm']], axis=1))
        rtm = s['rm'] + ro[:, :LANES]
        oim = ro[:, LANES:] + _dot(s['a_rk'], s['v2m'])
        s['rt'] = rtm[:cl] + rtm[cl:]
        s['oi'] = oim[:cl] + oim[cl:]
    for s in st:
        s['corr'] = _dotx(s['pm'], s['b_e'], _TN)
        s['nmat'] = _dot_tn(jnp.concatenate([s['qm'], s['v2m']], axis=0),
                            jnp.concatenate([s['b_e'], s['k_e']], axis=0))
    bds = [bd_ref[pi] for pi in range(npair)]
    for gi in chunks:
        for pi in range(npair):
            s = st[pi * gsz + gi]
            cols = slice(pi * LANES, (pi + 1) * LANES)
            o = _dot_nt(s['rt'], bds[pi]) + s['oi']
            bds[pi] = bds[pi] * s['g_all'] + _dotx(bds[pi], s['corr']) + s['nmat']
            mean = _pair_sum(o, m0) * (1.0 / RW_HEAD)
            oc = o - mean
            var = _pair_sum(oc * oc, m0) * (1.0 / RW_HEAD)
            o_ref[0, gi * cl:(gi + 1) * cl, cols] = (
                oc * lax.rsqrt(var + RW_GN_EPS) * gw_ref[:, cols] + gb_ref[:, cols] + s['bonus'])
    for pi in range(npair):
        bd_ref[pi] = bds[pi]

    @pl.when(cg == ncg - 1)
    def _():
        for pi in range(npair):
            s_ref[0, 2 * pi] = bds[pi][:RW_HEAD, :RW_HEAD]
            s_ref[0, 2 * pi + 1] = bds[pi][RW_HEAD:, RW_HEAD:]


def _rwkv_chunk(r, k, v, a, lw, b, t, p):
    npair = RW_PAIRS
    hp = RW_HEADS // (2 * npair)
    gsz = RW_GROUP if t % (RW_GROUP * RW_CHUNK) == 0 else 1
    rows = gsz * RW_CHUNK
    shp = lambda x: x.reshape(b, t, D_MODEL)
    blk = pl.BlockSpec((1, rows, npair * LANES), lambda bi, hi, ci: (bi, ci, hi))
    vec = pl.BlockSpec((1, npair * LANES), lambda bi, hi, ci: (0, hi))
    vrow = lambda x: x.reshape(1, D_MODEL)
    o, s = pl.pallas_call(
        functools.partial(_rwkv_chunk_body, gsz, npair),
        grid=(b, hp, t // rows),
        in_specs=[blk] * 5 + [vec] * 5,
        out_specs=[blk, pl.BlockSpec((1, 2 * npair, RW_HEAD, RW_HEAD), lambda bi, hi, ci: (bi, hi, 0, 0))],
        out_shape=[jax.ShapeDtypeStruct((b, t, D_MODEL), F32),
                   jax.ShapeDtypeStruct((b, RW_HEADS, RW_HEAD, RW_HEAD), F32)],
        scratch_shapes=[pltpu.VMEM((npair, LANES, LANES), F32)],
        compiler_params=_cparams("parallel", "parallel", "arbitrary"))(
            shp(r), shp(k), shp(v), shp(a), shp(lw), vrow(p['rw_k_k']), vrow(p['rw_k_a']),
            vrow(p['rw_r_k']), vrow(p['rw_gn_w']), vrow(p['rw_gn_b']))
    return o.reshape(b * t, D_MODEL), s


def _rwkv_step_body(r_ref, k_ref, vc_ref, a_ref, lw_ref, s_ref, kk_ref, ka_ref, rk_ref, gw_ref, gb_ref,
                    o_ref, so_ref):
    r = r_ref[0]
    k = k_ref[0]
    a = a_ref[0]
    lw = lw_ref[0]
    vc = vc_ref[0]
    s = s_ref[0]
    kkv = k * kk_ref[...]
    kap = kkv * lax.rsqrt(jnp.maximum(jnp.sum(kkv * kkv, axis=-1, keepdims=True), 1e-24))
    kmod = k * (1.0 + (a - 1.0) * ka_ref[...])
    bvec = kap * a
    dec = jnp.exp(lw)
    sa = jnp.sum(s * kap, axis=-1, keepdims=True)
    s_new = s * dec - sa * bvec + vc * kmod
    so_ref[0] = s_new
    o = jnp.sum(s_new * r, axis=-1, keepdims=True)
    mean = jnp.mean(o, axis=1, keepdims=True)
    oc = o - mean
    var = jnp.mean(oc * oc, axis=1, keepdims=True)
    on = oc * lax.rsqrt(var + RW_GN_EPS) * gw_ref[...] + gb_ref[...]
    bonus = jnp.sum(r * kmod * rk_ref[...], axis=-1, keepdims=True)
    o_ref[0] = on + bonus * vc


def _rwkv_step(r, k, v, a, lw, state, p):
    b = r.shape[0]
    hrow = (RW_HEADS, 1, RW_HEAD)
    hcol = (RW_HEADS, RW_HEAD, 1)
    rows = lambda x: x.reshape((b,) + hrow)
    rblk = pl.BlockSpec((1,) + hrow, lambda i: (i, 0, 0, 0))
    cblk = pl.BlockSpec((1,) + hcol, lambda i: (i, 0, 0, 0))
    sblk = pl.BlockSpec((1, RW_HEADS, RW_HEAD, RW_HEAD), lambda i: (i, 0, 0, 0))
    o, s = pl.pallas_call(
        _rwkv_step_body,
        grid=(b,),
        in_specs=[rblk, rblk, cblk, rblk, rblk, sblk] + [_full(hrow)] * 3 + [_full(hcol)] * 2,
        out_specs=[cblk, sblk],
        out_shape=[jax.ShapeDtypeStruct((b,) + hcol, F32),
                   jax.ShapeDtypeStruct((b, RW_HEADS, RW_HEAD, RW_HEAD), F32)],
        compiler_params=_cparams("parallel"))(
            rows(r), rows(k), v.reshape((b,) + hcol), rows(a), rows(lw), state,
            p['rw_k_k'].reshape(hrow), p['rw_k_a'].reshape(hrow), p['rw_r_k'].reshape(hrow),
            p['rw_gn_w'].reshape(hcol), p['rw_gn_b'].reshape(hcol))
    return o.reshape(b, D_MODEL), s


def _mamba_weights(p):
    w = p['mb_w_in']
    wz = w[:, :MB_INNER]
    wx = w[:, MB_INNER:MB_INNER + MB_CONV_DIM]
    wdt = jnp.pad(w[:, MB_INNER + MB_CONV_DIM:], ((0, 0), (0, LANES - MB_HEADS)))
    return [wz, wx, wdt]


def _mamba_vecs(p):
    pad = lambda x: jnp.pad(x.reshape(1, MB_HEADS), ((0, 0), (0, LANES - MB_HEADS)))
    return dict(conv_w=p['mb_conv_w'], conv_b=p['mb_conv_b'].reshape(1, MB_CONV_DIM),
                dt_bias=pad(p['mb_dt_bias']), a_log=pad(p['mb_a_log']),
                d_skip=jnp.repeat(p['mb_d'], MB_HEADDIM).reshape(1, MB_INNER),
                norm_w=p['mb_norm_w'].reshape(1, MB_INNER))


def _mamba_chunk_body(x_ref, xp_ref, z_ref, dt_ref, cw_ref, cb_ref, dtb_ref, alog_ref, dsk_ref, nw_ref,
                      y_ref, fs_ref, st_ref):
    c = pl.program_id(1)
    nc = pl.num_programs(1)
    cl = MB_CHUNK

    @pl.when(c == 0)
    def _():
        st_ref[...] = jnp.zeros_like(st_ref)

    hist = jnp.where(c == 0, 0.0, xp_ref[0])
    xcat = jnp.concatenate([hist, x_ref[0]], axis=0)
    cw = cw_ref[...]
    conv = cb_ref[...] + cw[0:1, :] * xcat[5:5 + cl]
    for j in range(1, MB_CONV):
        conv = conv + cw[j:j + 1, :] * xcat[5 + j:5 + j + cl]
    conv = _silu(conv)

    dt = _softplus(dt_ref[0] + dtb_ref[...])
    da = dt * (-jnp.exp(alog_ref[...]))
    ti = lax.broadcasted_iota(jnp.int32, (cl, cl), 0)
    tj = lax.broadcasted_iota(jnp.int32, (cl, cl), 1)
    causal = ti >= tj
    tri = jnp.where(causal, 1.0, 0.0).astype(BF16)
    a_cs = _cumsum_rows(tri, da)
    a_cs_t = a_cs.T
    a_last = a_cs[cl - 1:cl, :]
    dec_to_end = jnp.exp(a_last - a_cs)
    e_cs = jnp.exp(a_cs)
    e_last = jnp.exp(a_last)

    lane = lax.broadcasted_iota(jnp.int32, (cl, LANES), 1)
    m0 = lane < MB_HEADDIM
    heads_per_group = MB_HEADS // MB_GROUPS
    pairs_per_group = heads_per_group // 2
    for g in range(MB_GROUPS):
        bm = conv[:, MB_INNER + g * MB_STATE:MB_INNER + (g + 1) * MB_STATE]
        cm = conv[:, MB_INNER + (MB_GROUPS + g) * MB_STATE:MB_INNER + (MB_GROUPS + g + 1) * MB_STATE]
        cb = _dot_nt(cm, bm)
        bm_t = bm.T.astype(BF16)
        blocks = []
        for j in range(pairs_per_group):
            pi = g * pairs_per_group + j
            h0, h1 = 2 * pi, 2 * pi + 1
            lo, hi = pi * LANES, (pi + 1) * LANES
            xs = conv[:, lo:hi]
            dt2 = jnp.where(m0, dt[:, h0:h0 + 1], dt[:, h1:h1 + 1])
            xdt = xs * dt2
            l0 = jnp.exp(jnp.where(causal, a_cs[:, h0:h0 + 1] - a_cs_t[h0:h0 + 1, :], NEG_INF))
            l1 = jnp.exp(jnp.where(causal, a_cs[:, h1:h1 + 1] - a_cs_t[h1:h1 + 1, :], NEG_INF))
            lhs = jnp.concatenate([(cb * l0).astype(BF16), (cb * l1).astype(BF16)], axis=1)
            rhs = jnp.concatenate([jnp.where(m0, xdt, 0.0).astype(BF16),
                                   jnp.where(m0, 0.0, xdt).astype(BF16)], axis=0)
            y = jnp.dot(lhs, rhs, preferred_element_type=F32)
            st = st_ref[pi]
            e2 = jnp.where(m0, e_cs[:, h0:h0 + 1], e_cs[:, h1:h1 + 1])
            y = y + _dot(cm, st) * e2
            d2 = jnp.where(m0, dec_to_end[:, h0:h0 + 1], dec_to_end[:, h1:h1 + 1])
            sc = jnp.where(m0, e_last[:, h0:h0 + 1], e_last[:, h1:h1 + 1])
            st_ref[pi] = st * sc + jnp.dot(bm_t, (xdt * d2).astype(BF16), preferred_element_type=F32)
            y = y + dsk_ref[:, lo:hi] * xs
            blocks.append(y * _silu(z_ref[0, :, lo:hi]))
        ssq = blocks[0] * blocks[0]
        for blk in blocks[1:]:
            ssq = ssq + blk * blk
        scale = lax.rsqrt(jnp.sum(ssq, axis=-1, keepdims=True) * (1.0 / MB_GROUP_WIDTH) + MB_NORM_EPS)
        for j, blk in enumerate(blocks):
            lo = (g * pairs_per_group + j) * LANES
            y_ref[0, :, lo:lo + LANES] = blk * scale * nw_ref[:, lo:lo + LANES]

    @pl.when(c == nc - 1)
    def _():
        for pi in range(MB_HEADS // 2):
            fs_ref[0, pi] = st_ref[pi].T


def _mamba_chunk(z, xbc, dtr, b, t, vecs):
    nc = t // MB_CHUNK
    npairs = MB_HEADS // 2
    per = MB_CHUNK // SUBLANES
    y, fs = pl.pallas_call(
        _mamba_chunk_body,
        grid=(b, nc),
        in_specs=[pl.BlockSpec((1, MB_CHUNK, MB_CONV_DIM), lambda bi, ci: (bi, ci, 0)),
                  pl.BlockSpec((1, SUBLANES, MB_CONV_DIM),
                               lambda bi, ci: (bi, jnp.maximum(ci * per - 1, 0), 0)),
                  pl.BlockSpec((1, MB_CHUNK, MB_INNER), lambda bi, ci: (bi, ci, 0)),
                  pl.BlockSpec((1, MB_CHUNK, LANES), lambda bi, ci: (bi, ci, 0)),
                  _full((MB_CONV, MB_CONV_DIM)), _full((1, MB_CONV_DIM)), _full((1, LANES)),
                  _full((1, LANES)), _full((1, MB_INNER)), _full((1, MB_INNER))],
        out_specs=[pl.BlockSpec((1, MB_CHUNK, MB_INNER), lambda bi, ci: (bi, ci, 0)),
                   pl.BlockSpec((1, npairs, LANES, MB_STATE), lambda bi, ci: (bi, 0, 0, 0))],
        out_shape=[jax.ShapeDtypeStruct((b, t, MB_INNER), F32),
                   jax.ShapeDtypeStruct((b, npairs, LANES, MB_STATE), F32)],
        scratch_shapes=[pltpu.VMEM((npairs, LANES, MB_STATE), F32)],
        compiler_params=_cparams("parallel", "arbitrary"))(
            xbc.reshape(b, t, MB_CONV_DIM), xbc.reshape(b, t, MB_CONV_DIM), z.reshape(b, t, MB_INNER),
            dtr.reshape(b, t, LANES), vecs['conv_w'], vecs['conv_b'], vecs['dt_bias'], vecs['a_log'],
            vecs['d_skip'], vecs['norm_w'])
    return y.reshape(b * t, MB_INNER), fs.reshape(b, MB_HEADS, MB_HEADDIM, MB_STATE)


def _mamba_step_conv_body(x_ref, cs_ref, dt_ref, cw_ref, cb_ref, dtb_ref, alog_ref, conv_ref, dt_o, ed_o):
    cw = cw_ref[...]
    conv = cb_ref[...] + cw[MB_CONV - 1:MB_CONV, :] * x_ref[...]
    for j in range(MB_CONV - 1):
        conv = conv + cw[j:j + 1, :] * cs_ref[:, j, :]
    conv_ref[...] = _silu(conv)
    dt = _softplus(dt_ref[...] + dtb_ref[...])
    dt_o[...] = dt
    ed_o[...] = jnp.exp(dt * (-jnp.exp(alog_ref[...])))


def _mamba_step_state_body(s_ref, xc_ref, dt_ref, ed_ref, bm_ref, cm_ref, so_ref, y_ref):
    hpg = MB_HEADS // MB_GROUPS
    for g in range(MB_GROUPS):
        hs = slice(g * hpg, (g + 1) * hpg)
        s = s_ref[0, hs]
        xdt = xc_ref[0, hs] * dt_ref[0, hs]
        fin = s * ed_ref[0, hs] + xdt * bm_ref[0, g:g + 1]
        so_ref[0, hs] = fin
        y_ref[0, hs] = jnp.sum(fin * cm_ref[0, g:g + 1], axis=-1, keepdims=True)


def _mamba_step_out_body(y_ref, x_ref, z_ref, dsk_ref, nw_ref, o_ref):
    y = (y_ref[...] + dsk_ref[...] * x_ref[...]) * _silu(z_ref[...])
    for g in range(MB_GROUPS):
        lo, hi = g * MB_GROUP_WIDTH, (g + 1) * MB_GROUP_WIDTH
        o_ref[:, lo:hi] = _rms(y[:, lo:hi], nw_ref[:, lo:hi], MB_NORM_EPS)


def _mamba_step(z, xbc, dtr, conv_state, ssm_state, vecs):
    b = z.shape[0]
    conv, dt, ed = pl.pallas_call(
        _mamba_step_conv_body,
        grid=(1,),
        in_specs=[_full((b, MB_CONV_DIM)), _full((b, MB_CONV - 1, MB_CONV_DIM)), _full((b, LANES)),
                  _full((MB_CONV, MB_CONV_DIM)), _full((1, MB_CONV_DIM)), _full((1, LANES)), _full((1, LANES))],
        out_specs=[_full((b, MB_CONV_DIM)), _full((b, LANES)), _full((b, LANES))],
        out_shape=[jax.ShapeDtypeStruct((b, MB_CONV_DIM), F32), jax.ShapeDtypeStruct((b, LANES), F32),
                   jax.ShapeDtypeStruct((b, LANES), F32)],
        compiler_params=_cparams("arbitrary"))(
            xbc, conv_state, dtr, vecs['conv_w'], vecs['conv_b'], vecs['dt_bias'], vecs['a_log'])
    xs = conv[:, :MB_INNER]
    bm = conv[:, MB_INNER:MB_INNER + MB_GROUPS * MB_STATE].reshape(b, MB_GROUPS, MB_STATE)
    cm = conv[:, MB_INNER + MB_GROUPS * MB_STATE:].reshape(b, MB_GROUPS, MB_STATE)
    hp1 = (MB_HEADS, MB_HEADDIM, 1)
    h11 = (MB_HEADS, 1, 1)
    per = lambda shape: pl.BlockSpec((1,) + shape, lambda i: (i,) + (0,) * len(shape))
    st_shape = (MB_HEADS, MB_HEADDIM, MB_STATE)
    fin, y = pl.pallas_call(
        _mamba_step_state_body,
        grid=(b,),
        in_specs=[per(st_shape), per(hp1), per(h11), per(h11), per((MB_GROUPS, MB_STATE)),
                  per((MB_GROUPS, MB_STATE))],
        out_specs=[per(st_shape), per(hp1)],
        out_shape=[jax.ShapeDtypeStruct((b,) + st_shape, F32), jax.ShapeDtypeStruct((b,) + hp1, F32)],
        compiler_params=_cparams("parallel"))(
            ssm_state, xs.reshape((b,) + hp1), dt[:, :MB_HEADS].reshape((b,) + h11),
            ed[:, :MB_HEADS].reshape((b,) + h11), bm, cm)
    yg = pl.pallas_call(
        _mamba_step_out_body,
        grid=(1,),
        in_specs=[_full((b, MB_INNER))] * 3 + [_full((1, MB_INNER))] * 2,
        out_specs=_full((b, MB_INNER)),
        out_shape=jax.ShapeDtypeStruct((b, MB_INNER), F32),
        compiler_params=_cparams("arbitrary"))(
            y.reshape(b, MB_INNER), xs, z, vecs['d_skip'], vecs['norm_w'])
    return yg, fin


def _rope_tables(pos):
    half = ROPE_DIM // 2
    inv_freq = ROPE_THETA ** (-jnp.arange(0, ROPE_DIM, 2, dtype=F32) / ROPE_DIM)
    ang = pos.astype(F32)[:, None] * inv_freq[None, :]
    cos, sin = jnp.cos(ang), jnp.sin(ang)
    n = pos.shape[0]
    ones = jnp.ones((n, DA_HEAD - ROPE_DIM), F32)
    zeros = jnp.zeros((n, DA_HEAD - ROPE_DIM), F32)
    zh = jnp.zeros((n, half), F32)
    ct = jnp.concatenate([cos, cos, ones], axis=1)
    s_up = jnp.concatenate([-sin, zh, zeros], axis=1)
    s_dn = jnp.concatenate([zh, sin, zeros], axis=1)
    two = lambda x: jnp.concatenate([x, x], axis=1)
    return two(ct), two(s_up), two(s_dn)


def _rope(x, ct, s_up, s_dn):
    half = ROPE_DIM // 2
    cols = []
    for j in range(x.shape[1] // LANES):
        xj = x[:, j * LANES:(j + 1) * LANES]
        cols.append(xj * ct + pltpu.roll(xj, LANES - half, axis=1) * s_up + pltpu.roll(xj, half, axis=1) * s_dn)
    return jnp.concatenate(cols, axis=1)


def _da_epilogue(ys, rows, fulls):
    y = ys[0]
    ct, s_up, s_dn = rows
    q = _rope(y[:, :DA_WIDTH], ct, s_up, s_dn)
    k = _rope(y[:, DA_WIDTH:2 * DA_WIDTH], ct, s_up, s_dn)
    return q, k, y[:, 2 * DA_WIDTH:3 * DA_WIDTH], y[:, 3 * DA_WIDTH:]


def _lambda(lq1, lk1, lq2, lk2, lam_init):
    return (jnp.exp(jnp.sum(lq1 * lk1, axis=-1, keepdims=True))
            - jnp.exp(jnp.sum(lq2 * lk2, axis=-1, keepdims=True)) + lam_init)


def _da_epilogue_seq(ys, rows, fulls):
    q, k, v, g = _da_epilogue(ys, rows, fulls)
    return k, v, g, (q * DA_SCALE).T, k, v.T


def _flash_body(lam_init, qi_ref, kj_ref, qt_ref, k_ref, vt_ref, lq1, lk1, lq2, lk2, sw_ref, o_ref,
                m1, l1, m2, l2, acc1, acc2):
    step = pl.program_id(2)
    qi = qi_ref[step]
    kj = kj_ref[step]

    @pl.when(kj == 0)
    def _():
        for m in (m1, m2):
            m[...] = jnp.full_like(m, NEG_INF)
        for z in (l1, l2, acc1, acc2):
            z[...] = jnp.zeros_like(z)

    def update(masked):
        qt = qt_ref[...]
        k = k_ref[...]
        vt = vt_ref[...]
        tk, tq = k.shape[0], qt.shape[1]
        frow = lax.broadcasted_iota(jnp.int32, qt.shape, 0)
        zero = jnp.zeros_like(qt)
        if masked:
            keep = (lax.broadcasted_iota(jnp.int32, (tk, tq), 0)
                    <= lax.broadcasted_iota(jnp.int32, (tk, tq), 1))
        scores = [jnp.dot(k, jnp.where(frow < DA_HEAD, qt, zero), preferred_element_type=F32),
                  jnp.dot(k, jnp.where(frow < DA_HEAD, zero, qt), preferred_element_type=F32)]
        for s, (m_ref, l_ref, acc_ref) in zip(scores, ((m1, l1, acc1), (m2, l2, acc2))):
            if masked:
                s = jnp.where(keep, s, NEG_INF)
            m_old = m_ref[...]
            m_new = jnp.maximum(m_old, jnp.max(s, axis=0, keepdims=True))
            alpha = jnp.exp(m_old - m_new)
            pr = jnp.exp(s - m_new)
            l_ref[...] = l_ref[...] * alpha + jnp.sum(pr, axis=0, keepdims=True)
            acc_ref[...] = acc_ref[...] * alpha + jnp.dot(vt, pr.astype(BF16), preferred_element_type=F32)
            m_ref[...] = m_new

    @pl.when(kj < qi)
    def _():
        update(False)

    @pl.when(kj == qi)
    def _():
        update(True)
        lam = _lambda(lq1[...], lk1[...], lq2[...], lk2[...], lam_init)
        ot = acc1[...] / l1[...] - lam * (acc2[...] / l2[...])
        scale = lax.rsqrt(jnp.mean(ot * ot, axis=0, keepdims=True) + DA_SUBLN_EPS)
        o_ref[...] = (ot * scale * (sw_ref[...] * (1.0 - lam_init))).T


def _flash(qt, kb, vt, b, t, p, lam_init):
    tq = min(t, 512)
    nq = t // tq
    pairs = [(qi, kj) for qi in range(nq) for kj in range(qi + 1)]
    qi_tab = jnp.asarray([pr[0] for pr in pairs], jnp.int32)
    kj_tab = jnp.asarray([pr[1] for pr in pairs], jnp.int32)
    qblk = pl.BlockSpec((DA_VDIM, tq), lambda bi, hi, s, qt_, kt_: (hi, bi * nq + qt_[s]))
    vblk = pl.BlockSpec((DA_VDIM, tq), lambda bi, hi, s, qt_, kt_: (hi, bi * nq + kt_[s]))
    kblk = pl.BlockSpec((tq, DA_VDIM), lambda bi, hi, s, qt_, kt_: (bi * nq + kt_[s], hi))
    oblk = pl.BlockSpec((tq, DA_VDIM), lambda bi, hi, s, qt_, kt_: (bi * nq + qt_[s], hi))
    small = lambda shape: pl.BlockSpec(shape, lambda bi, hi, s, qt_, kt_: (0, 0))
    vec = lambda x: x.reshape(1, -1)
    stat = pltpu.VMEM((1, tq), F32)
    acc = pltpu.VMEM((DA_VDIM, tq), F32)
    return pl.pallas_call(
        functools.partial(_flash_body, lam_init),
        grid_spec=pltpu.PrefetchScalarGridSpec(
            num_scalar_prefetch=2, grid=(b, DA_HEADS, len(pairs)),
            in_specs=[qblk, kblk, vblk] + [small((1, DA_HEAD))] * 4 + [small((DA_VDIM, 1))],
            out_specs=oblk,
            scratch_shapes=[stat, stat, stat, stat, acc, acc]),
        out_shape=jax.ShapeDtypeStruct((b * t, DA_WIDTH), F32),
        compiler_params=_cparams("parallel", "parallel", "arbitrary"))(
            qi_tab, kj_tab, qt, kb, vt, vec(p['da_lq1']), vec(p['da_lk1']), vec(p['da_lq2']),
            vec(p['da_lk2']), p['da_subln_w'].reshape(DA_VDIM, 1))


def _decode_body(lam_init, npg, pt_ref, q_ref, kn_ref, vn_ref, *refs):
    ck_refs = refs[:npg]
    cv_refs = refs[npg:2 * npg]
    lq1, lk1, lq2, lk2, sw_ref, o_ref, qm_s, m_s, l_s, acc = refs[2 * npg:]
    j = pl.program_id(1)
    nj = pl.num_programs(1)
    nrow = 2 * DA_HEADS
    ncol = PAGE_SIZE * DA_HEADS
    rid = lax.broadcasted_iota(jnp.int32, (nrow, LANES), 0)

    @pl.when(j == 0)
    def _():
        comp = lax.broadcasted_iota(jnp.int32, (nrow, LANES), 1) // DA_HEAD
        qm = jnp.where(jnp.bitwise_and(rid, 1) == comp, q_ref[0] * DA_SCALE, 0.0)
        qm_s[...] = qm
        m_s[...] = jnp.sum(qm * kn_ref[0], axis=-1, keepdims=True)
        l_s[...] = jnp.ones_like(l_s)
        acc[...] = vn_ref[0]

    qm = qm_s[...]
    keep = (lax.rem(lax.broadcasted_iota(jnp.int32, (nrow, ncol), 1), DA_HEADS)
            == lax.broadcasted_iota(jnp.int32, (nrow, ncol), 0) // 2)
    scores = [jnp.where(keep, _dot_nt(qm, ck[0].reshape(ncol, DA_VDIM)), NEG_INF) for ck in ck_refs]
    m_old = m_s[...]
    m_new = m_old
    for s in scores:
        m_new = jnp.maximum(m_new, jnp.max(s, axis=-1, keepdims=True))
    alpha = jnp.exp(m_old - m_new)
    l_new = l_s[...] * alpha
    a_new = acc[...] * alpha
    for s, cv in zip(scores, cv_refs):
        pr = jnp.exp(s - m_new)
        l_new = l_new + jnp.sum(pr, axis=-1, keepdims=True)
        a_new = a_new + _dot(pr, cv[0].reshape(ncol, DA_VDIM))
    m_s[...] = m_new
    l_s[...] = l_new
    acc[...] = a_new

    @pl.when(j == nj - 1)
    def _():
        lam = _lambda(lq1[...], lk1[...], lq2[...], lk2[...], lam_init)
        d = (a_new / l_new) * jnp.where(jnp.bitwise_and(rid, 1) == 0, 1.0, -lam)
        pair = d + pltpu.roll(d, nrow - 1, axis=0)
        o_ref[0] = _rms(pair, sw_ref[...], DA_SUBLN_EPS) * (1.0 - lam_init)


def _decode(q, k_new, v_new, cache_k, cache_v, page_table, p, lam_init):
    b = q.shape[0]
    n_pages = page_table.shape[1]
    npg = DEC_PAGES if n_pages % DEC_PAGES == 0 else 1
    nrow = 2 * DA_HEADS
    rows16 = lambda x: jnp.repeat(x.reshape(b, DA_HEADS, DA_VDIM), 2, axis=1)
    row = pl.BlockSpec((1, nrow, DA_VDIM), lambda bi, j, pt: (bi, 0, 0))
    pages = [pl.BlockSpec((1, PAGE_SIZE, DA_HEADS, DA_VDIM),
                          lambda bi, j, pt, i=i: (pt[bi, j * npg + i], 0, 0, 0)) for i in range(npg)]
    vec = lambda x: x.reshape(1, -1)
    small = lambda n: pl.BlockSpec((1, n), lambda bi, j, pt: (0, 0))
    o = pl.pallas_call(
        functools.partial(_decode_body, lam_init, npg),
        grid_spec=pltpu.PrefetchScalarGridSpec(
            num_scalar_prefetch=1, grid=(b, n_pages // npg),
            in_specs=[row, row, row] + pages + pages + [small(DA_HEAD)] * 4 + [small(DA_VDIM)],
            out_specs=row,
            scratch_shapes=[pltpu.VMEM((nrow, DA_VDIM), F32), pltpu.VMEM((nrow, 1), F32),
                            pltpu.VMEM((nrow, 1), F32), pltpu.VMEM((nrow, DA_VDIM), F32)]),
        out_shape=jax.ShapeDtypeStruct((b, nrow, DA_VDIM), F32),
        compiler_params=_cparams("parallel", "arbitrary"))(
            page_table, rows16(q), rows16(k_new), rows16(v_new), *([cache_k] * npg), *([cache_v] * npg),
            vec(p['da_lq1']), vec(p['da_lk1']), vec(p['da_lq2']), vec(p['da_lk2']), vec(p['da_subln_w']))
    return o[:, ::2, :].reshape(b, DA_WIDTH)


def _cf_epilogue(ys, rows, fulls):
    y = ys[0]
    u = y[:, :CF_WIDTH] * _sigmoid(y[:, CF_WIDTH:2 * CF_WIDTH])
    return u, y[:, 2 * CF_WIDTH:]


def _cf_conv_body(tc, u_ref, up_ref, cw_ref, cb_ref, lw_ref, lb_ref, o_ref, buf, sh):
    c = pl.program_id(1)
    buf[0:CF_HIST, :] = jnp.where(c == 0, 0.0, up_ref[0])
    buf[CF_HIST:CF_HIST + tc, :] = u_ref[0]
    off = CF_HIST - (CF_KERNEL - 1)
    span = tc + CF_HIST - SUBLANES
    for r in range(1, SUBLANES):
        sh[r - 1, 0:span, :] = buf[r:r + span, :]
    rb = CF_ROWS
    for i in range(tc // rb):
        acc = cb_ref[...]
        for j in range(CF_KERNEL):
            r, base = (off + j) % SUBLANES, (off + j) // SUBLANES * SUBLANES
            rows = slice(base + i * rb, base + (i + 1) * rb)
            acc = acc + cw_ref[j:j + 1, :] * (buf[rows, :] if r == 0 else sh[r - 1, rows, :])
        mu = jnp.mean(acc, axis=-1, keepdims=True)
        xc = acc - mu
        var = jnp.mean(xc * xc, axis=-1, keepdims=True)
        o_ref[0, i * rb:(i + 1) * rb, :] = _silu(xc * lax.rsqrt(var + CF_LN_EPS) * lw_ref[...] + lb_ref[...])


def _cf_conv(u, b, t, p):
    tc = min(t, 256)
    per = tc // CF_HIST
    blk = pl.BlockSpec((1, tc, CF_WIDTH), lambda bi, ci: (bi, ci, 0))
    prev = pl.BlockSpec((1, CF_HIST, CF_WIDTH), lambda bi, ci: (bi, jnp.maximum(ci * per - 1, 0), 0))
    vec = lambda x: x.reshape(1, CF_WIDTH)
    u3 = u.reshape(b, t, CF_WIDTH)
    o = pl.pallas_call(
        functools.partial(_cf_conv_body, tc),
        grid=(b, t // tc),
        in_specs=[blk, prev, _full((CF_KERNEL, CF_WIDTH))] + [_full((1, CF_WIDTH))] * 3,
        out_specs=blk,
        out_shape=jax.ShapeDtypeStruct((b, t, CF_WIDTH), F32),
        scratch_shapes=[pltpu.VMEM((CF_HIST + tc, CF_WIDTH), F32),
                        pltpu.VMEM((SUBLANES - 1, CF_HIST + tc - SUBLANES, CF_WIDTH), F32)],
        compiler_params=_cparams("parallel", "parallel"))(
            u3, u3, p['cf_conv_w'], vec(p['cf_conv_b']), vec(p['cf_ln_w']), vec(p['cf_ln_b']))
    return o.reshape(b * t, CF_WIDTH)


def _cf_step_body(u_ref, st_ref, cw_ref, cb_ref, lw_ref, lb_ref, o_ref):
    nb = u_ref.shape[0]
    cw = cw_ref[...]
    hist = jnp.concatenate([jnp.sum(st_ref[i] * cw[:CF_KERNEL - 1, :], axis=0, keepdims=True)
                            for i in range(nb)], axis=0)
    acc = cb_ref[...] + hist + cw[CF_KERNEL - 1:CF_KERNEL, :] * u_ref[...]
    mu = jnp.mean(acc, axis=-1, keepdims=True)
    xc = acc - mu
    var = jnp.mean(xc * xc, axis=-1, keepdims=True)
    o_ref[...] = _silu(xc * lax.rsqrt(var + CF_LN_EPS) * lw_ref[...] + lb_ref[...])


def _cf_step(u, state, p):
    b = u.shape[0]
    tb = SUBLANES
    vec = lambda x: x.reshape(1, CF_WIDTH)
    return pl.pallas_call(
        _cf_step_body,
        grid=(b // tb,),
        in_specs=[pl.BlockSpec((tb, CF_WIDTH), lambda i: (i, 0)),
                  pl.BlockSpec((tb, CF_KERNEL - 1, CF_WIDTH), lambda i: (i, 0, 0)),
                  _full((CF_KERNEL, CF_WIDTH))] + [_full((1, CF_WIDTH))] * 3,
        out_specs=pl.BlockSpec((tb, CF_WIDTH), lambda i: (i, 0)),
        out_shape=jax.ShapeDtypeStruct((b, CF_WIDTH), F32),
        compiler_params=_cparams("parallel"))(
            u, state, p['cf_conv_w'], vec(p['cf_conv_b']), vec(p['cf_ln_w']), vec(p['cf_ln_b']))


def _forward(x_prompt, x_sample, state_rwkv_shift, state_rwkv_wkv, state_mamba_conv, state_mamba_ssm,
             cache_k, cache_v, page_table, state_conformer_conv, p):
    bp, tp, d = x_prompt.shape
    bs = x_sample.shape[0]
    hp = x_prompt.reshape(bp * tp, d)
    hs = x_sample.reshape(bs, d)
    npre, npost = p['norm_pre'], p['norm_post']

    (r, k, v, g, a, lw), rw_shift_p = _rwkv_proj_seq(hp, bp, tp, npre[0], p)
    o, rw_wkv_p = _rwkv_chunk(r, k, v, a, lw, bp, tp, p)
    hp = _out_proj(o, g, hp, p['rw_w_o'], npost[0])
    (r, k, v, g, a, lw), rw_shift_s = _rwkv_proj_step(hs, state_rwkv_shift, npre[0], p)
    o, rw_wkv_s = _rwkv_step(r, k, v, a, lw, state_rwkv_wkv, p)
    hs = _out_proj(o, g, hs, p['rw_w_o'], npost[0])

    mw = _mamba_weights(p)
    mv = _mamba_vecs(p)
    ident = lambda ys, rows, fulls: ys
    widths = (MB_INNER, MB_CONV_DIM, LANES)
    z, xbc, dtr = _in_proj(hp, npre[1], mw, ident, widths, tm=256)
    yg, mb_ssm_p = _mamba_chunk(z, xbc, dtr, bp, tp, mv)
    mb_conv_p = xbc.reshape(bp, tp, MB_CONV_DIM)[:, tp - (MB_CONV - 1):]
    hp = _out_proj(yg, None, hp, p['mb_w_out'], npost[1])
    z, xbc, dtr = _in_proj(hs, npre[1], mw, ident, widths)
    yg, mb_ssm_s = _mamba_step(z, xbc, dtr, state_mamba_conv, state_mamba_ssm, mv)
    mb_conv_s = jnp.concatenate([state_mamba_conv[:, 1:], xbc[:, None, :]], axis=1)
    hs = _out_proj(yg, None, hs, p['mb_w_out'], npost[1])

    lam_init = 0.8 - 0.6 * math.exp(-0.3 * 2)
    tabs_p = _rope_tables(jnp.arange(tp, dtype=jnp.int32))
    tm = min(_row_tile(tp), 256)
    tps = tp // tm
    rows_p = [(tb, lambda i: (i % tps, 0), LANES) for tb in tabs_p]
    widths = ((DA_WIDTH, F32, 'heads'), (DA_WIDTH, F32, 'heads'), DA_WIDTH,
              (DA_WIDTH, BF16, True), (DA_WIDTH, BF16, False), (DA_WIDTH, BF16, True))
    k, v, g, qt, kb, vt = _in_proj(hp, npre[2], [p['da_w_qkvg']], _da_epilogue_seq, widths,
                                   row_extras=rows_p, tm=tm)
    k_rows_p = k.reshape(bp, tp, DA_HEADS, 2 * DA_HEAD)
    v_rows_p = v.reshape(bp, tp, DA_HEADS, DA_VDIM)
    o = _flash(qt, kb, vt, bp, tp, p, lam_init)
    hp = _out_proj(o, g, hp, p['da_w_o'], npost[2])
    tabs_s = _rope_tables(jnp.full((bs,), PAST_LEN, dtype=jnp.int32))
    rows_s = [(tb, lambda i: (i, 0), LANES) for tb in tabs_s]
    q, k, v, g = _in_proj(hs, npre[2], [p['da_w_qkvg']], _da_epilogue, (DA_WIDTH,) * 4, row_extras=rows_s)
    k_rows_s = k.reshape(bs, 1, DA_HEADS, 2 * DA_HEAD)
    v_rows_s = v.reshape(bs, 1, DA_HEADS, DA_VDIM)
    o = _decode(q, k, v, cache_k, cache_v, page_table, p, lam_init)
    hs = _out_proj(o, g, hs, p['da_w_o'], npost[2])

    u, g = _in_proj(hp, npre[3], [p['cf_w_in']], _cf_epilogue, (CF_WIDTH,) * 2)
    cf_conv_p = u.reshape(bp, tp, CF_WIDTH)[:, tp - (CF_KERNEL - 1):]
    c = _cf_conv(u, bp, tp, p)
    hp = _out_proj(c, g, hp, p['cf_w_out'], npost[3])
    u, g = _in_proj(hs, npre[3], [p['cf_w_in']], _cf_epilogue, (CF_WIDTH,) * 2)
    cf_conv_s = jnp.concatenate([state_conformer_conv[:, 1:], u[:, None, :]], axis=1)
    c = _cf_step(u, state_conformer_conv, p)
    hs = _out_proj(c, g, hs, p['cf_w_out'], npost[3])

    return (hp.reshape(bp, tp, d), hs.reshape(bs, 1, d), rw_shift_p, rw_shift_s, rw_wkv_p, rw_wkv_s,
            mb_conv_p, mb_conv_s, mb_ssm_p, mb_ssm_s, k_rows_p, k_rows_s, v_rows_p, v_rows_s,
            cf_conv_p, cf_conv_s)


def kernel(x_prompt, x_sample, state_rwkv_shift, state_rwkv_wkv, state_mamba_conv, state_mamba_ssm, cache_k, cache_v, page_table, state_conformer_conv, norm_pre, norm_post, rw_mu, rw_w_r, rw_w_k, rw_w_v, rw_w_g, rw_w0, rw_w_w1, rw_w_w2, rw_a0, rw_w_a1, rw_w_a2, rw_k_k, rw_k_a, rw_r_k, rw_gn_w, rw_gn_b, rw_w_o, mb_w_in, mb_conv_w, mb_conv_b, mb_dt_bias, mb_a_log, mb_d, mb_norm_w, mb_w_out, da_w_qkvg, da_lq1, da_lk1, da_lq2, da_lk2, da_subln_w, da_w_o, cf_w_in, cf_conv_w, cf_conv_b, cf_ln_w, cf_ln_b, cf_w_out):
    p = dict(norm_pre=norm_pre, norm_post=norm_post, rw_mu=rw_mu, rw_w_r=rw_w_r, rw_w_k=rw_w_k, rw_w_v=rw_w_v,
             rw_w_g=rw_w_g, rw_w0=rw_w0, rw_w_w1=rw_w_w1, rw_w_w2=rw_w_w2, rw_a0=rw_a0, rw_w_a1=rw_w_a1,
             rw_w_a2=rw_w_a2, rw_k_k=rw_k_k, rw_k_a=rw_k_a, rw_r_k=rw_r_k, rw_gn_w=rw_gn_w, rw_gn_b=rw_gn_b,
             rw_w_o=rw_w_o, mb_w_in=mb_w_in, mb_conv_w=mb_conv_w, mb_conv_b=mb_conv_b, mb_dt_bias=mb_dt_bias,
             mb_a_log=mb_a_log, mb_d=mb_d, mb_norm_w=mb_norm_w, mb_w_out=mb_w_out, da_w_qkvg=da_w_qkvg,
             da_lq1=da_lq1, da_lk1=da_lk1, da_lq2=da_lq2, da_lk2=da_lk2, da_subln_w=da_subln_w, da_w_o=da_w_o,
             cf_w_in=cf_w_in, cf_conv_w=cf_conv_w, cf_conv_b=cf_conv_b, cf_ln_w=cf_ln_w, cf_ln_b=cf_ln_b,
             cf_w_out=cf_w_out)
    return _forward(x_prompt, x_sample, state_rwkv_shift, state_rwkv_wkv, state_mamba_conv, state_mamba_ssm,
                    cache_k, cache_v, page_table, state_conformer_conv, p)
```

```python
import functools
import math

import jax
import jax.numpy as jnp
from jax import lax
from jax.experimental import pallas as pl
from jax.experimental.pallas import tpu as pltpu

F32 = jnp.float32
BF16 = jnp.bfloat16

D_MODEL = 1024
PAST_LEN = 8192
PAGE_SIZE = 128
RMS_EPS = 1e-6

RW_HEAD = 64
RW_HEADS = D_MODEL // RW_HEAD
RW_LORA = 64
RW_GN_EPS = 64e-5
RW_CHUNK = 64
RW_GROUP = 8
RW_PAIRS = 2

MB_INNER = 2 * D_MODEL
MB_HEADDIM = 64
MB_HEADS = MB_INNER // MB_HEADDIM
MB_GROUPS = 4
MB_STATE = 128
MB_CONV = 4
MB_CHUNK = 128
MB_CONV_DIM = MB_INNER + 2 * MB_GROUPS * MB_STATE
MB_NORM_EPS = 1e-5
MB_GROUP_WIDTH = MB_INNER // MB_GROUPS

DA_HEADS = 8
DA_HEAD = 64
DA_VDIM = 2 * DA_HEAD
DA_WIDTH = DA_HEADS * DA_VDIM
DA_SCALE = DA_HEAD ** -0.5
LOG2_E = math.log2(math.e)
DA_SUBLN_EPS = 1e-5
ROPE_THETA = 500000.0
ROPE_DIM = DA_HEAD // 4
DEC_PAGES = 16

CF_WIDTH = D_MODEL
CF_KERNEL = 31
CF_LN_EPS = 1e-5
CF_HIST = 32
CF_ROWS = 32

LANES = 128
SUBLANES = 8
VMEM_LIMIT = 56 * 1024 * 1024

NEG_INF = float("-inf")


def _cparams(*sem):
    return pltpu.CompilerParams(dimension_semantics=sem, vmem_limit_bytes=VMEM_LIMIT)


def _dot(a, b):
    return jnp.dot(a.astype(BF16), b.astype(BF16), preferred_element_type=F32)


def _dot_nt(a, b):
    return lax.dot_general(a.astype(BF16), b.astype(BF16), (((1,), (1,)), ((), ())),
                           preferred_element_type=F32)


def _dot_tn(a, b):
    return lax.dot_general(a.astype(BF16), b.astype(BF16), (((0,), (0,)), ((), ())),
                           preferred_element_type=F32)


def _split3(a):
    hi = a.astype(BF16)
    r1 = a - hi.astype(F32)
    mid = r1.astype(BF16)
    lo = (r1 - mid.astype(F32)).astype(BF16)
    return hi, mid, lo


def _cumsum_rows(tri_bf16, x):
    hi, mid, lo = _split3(x)
    f = functools.partial(jnp.dot, preferred_element_type=F32)
    return f(tri_bf16, hi) + (f(tri_bf16, mid) + f(tri_bf16, lo))


def _sigmoid(x):
    return 1.0 / (1.0 + jnp.exp(-x))


def _silu(x):
    return x * _sigmoid(x)


def _softplus(x):
    return jnp.maximum(x, 0.0) + jnp.log(1.0 + jnp.exp(-jnp.abs(x)))


def _rms(x, w, eps):
    return x * lax.rsqrt(jnp.mean(x * x, axis=-1, keepdims=True) + eps) * w


def _row_tile(m):
    for t in (512, 256, 128, 64, 32, 16, 8):
        if m % t == 0:
            return t
    return m


def _full(shape):
    nd = len(shape)
    return pl.BlockSpec(shape, lambda *_: (0,) * nd)


def _out_proj_body(has_gate, *refs):
    if has_gate:
        y_ref, g_ref, h_ref, w_ref, pw_ref, o_ref = refs
        y = y_ref[...] * _silu(g_ref[...])
    else:
        y_ref, h_ref, w_ref, pw_ref, o_ref = refs
        y = y_ref[...]
    z = _dot(y, w_ref[...])
    o_ref[...] = h_ref[...] + _rms(z, pw_ref[...], RMS_EPS)


def _out_proj(y, gate, h, w, post_w):
    m, kdim = y.shape
    tm = _row_tile(m)
    row = lambda width: pl.BlockSpec((tm, width), lambda i: (i, 0))
    args, specs = [y], [row(kdim)]
    if gate is not None:
        args.append(gate)
        specs.append(row(D_MODEL))
    args += [h, w.astype(BF16), post_w.reshape(1, D_MODEL)]
    specs += [row(D_MODEL), _full((kdim, D_MODEL)), _full((1, D_MODEL))]
    return pl.pallas_call(
        functools.partial(_out_proj_body, gate is not None),
        grid=(m // tm,), in_specs=specs, out_specs=row(D_MODEL),
        out_shape=jax.ShapeDtypeStruct((m, D_MODEL), F32),
        compiler_params=_cparams("parallel"))(*args)


def _in_proj_body(n_w, n_row, n_full, epilogue, *refs):
    h_ref, nw_ref = refs[0], refs[1]
    w_refs = refs[2:2 + n_w]
    row_refs = refs[2 + n_w:2 + n_w + n_row]
    full_refs = refs[2 + n_w + n_row:2 + n_w + n_row + n_full]
    out_refs = refs[2 + n_w + n_row + n_full:]
    xn = _rms(h_ref[...], nw_ref[...], RMS_EPS).astype(BF16)
    ys = [jnp.dot(xn, w[...], preferred_element_type=F32) for w in w_refs]
    outs = epilogue(ys, [r[...] for r in row_refs], [r[...] for r in full_refs])
    for o_ref, o in zip(out_refs, outs):
        o_ref[...] = o.astype(o_ref.dtype).reshape(o_ref.shape)


def _in_proj(h, norm_w, weights, epilogue, out_widths, row_extras=(), full_extras=(), tm=None):
    m = h.shape[0]
    tm = tm or _row_tile(m)
    row = lambda width: pl.BlockSpec((tm, width), lambda i: (i, 0))
    args = [h, norm_w.reshape(1, D_MODEL)] + [w.astype(BF16) for w in weights]
    specs = [row(D_MODEL), _full((1, D_MODEL))] + [_full(w.shape) for w in weights]
    for arr, imap, width in row_extras:
        args.append(arr)
        specs.append(pl.BlockSpec((tm, width), imap))
    for arr in full_extras:
        args.append(arr)
        specs.append(_full(arr.shape))
    out_specs, out_shape = [], []
    for wd in out_widths:
        width, dtype, transposed = wd if isinstance(wd, tuple) else (wd, F32, False)
        if transposed == 'heads':
            shape3 = (width // LANES, LANES)
            out_specs.append(pl.BlockSpec((tm,) + shape3, lambda i: (i, 0, 0)))
            out_shape.append(jax.ShapeDtypeStruct((m,) + shape3, dtype))
        elif transposed:
            out_specs.append(pl.BlockSpec((width, tm), lambda i: (0, i)))
            out_shape.append(jax.ShapeDtypeStruct((width, m), dtype))
        else:
            out_specs.append(row(width))
            out_shape.append(jax.ShapeDtypeStruct((m, width), dtype))
    return pl.pallas_call(
        functools.partial(_in_proj_body, len(weights), len(row_extras), len(full_extras), epilogue),
        grid=(m // tm,), in_specs=specs, out_specs=out_specs, out_shape=out_shape,
        compiler_params=_cparams("parallel"))(*args)


def _rwkv_mix(xn, xprev, mu, wr, wk, wv, wg, w0, ww1, ww2, a0, wa1, wa2):
    xx = xprev - xn
    xm = [xn + xx * mu[i:i + 1, :] for i in range(6)]
    r = _dot(xm[0], wr)
    k = _dot(xm[2], wk)
    v = _dot(xm[3], wv)
    g = _dot(xm[5], wg)
    zw = w0 + _dot(jnp.tanh(_dot(xm[1], ww1)), ww2)
    w_log = -_softplus(-zw) - 0.5
    lw = -jnp.exp(w_log)
    a = _sigmoid(a0 + _dot(_dot(xm[4], wa1), wa2))
    return r, k, v, g, a, lw


def _rwkv_proj_seq_body(tiles_per_seq, h_ref, hp_ref, nw_ref, mu_ref, wr, wk, wv, wg, w0, ww1, ww2,
                        a0, wa1, wa2, r_ref, k_ref, v_ref, g_ref, a_ref, lw_ref, last_ref):
    i = pl.program_id(0)
    nw = nw_ref[...]
    xn = _rms(h_ref[...], nw, RMS_EPS)
    tm = xn.shape[0]
    pn = _rms(hp_ref[...], nw, RMS_EPS)[SUBLANES - 1:SUBLANES, :]
    first = (i % tiles_per_seq) == 0
    prow = jnp.where(first, 0.0, pn)
    rows = lax.broadcasted_iota(jnp.int32, xn.shape, 0)
    xprev = jnp.where(rows == 0, prow, pltpu.roll(xn, 1, axis=0))
    outs = _rwkv_mix(xn, xprev, mu_ref[...], wr[...], wk[...], wv[...], wg[...], w0[...], ww1[...],
                     ww2[...], a0[...], wa1[...], wa2[...])
    for o_ref, o in zip((r_ref, k_ref, v_ref, g_ref, a_ref, lw_ref), outs):
        o_ref[...] = o
    last_ref[0] = xn[tm - 1:tm, :]


def _rwkv_proj_step_body(h_ref, sh_ref, nw_ref, mu_ref, wr, wk, wv, wg, w0, ww1, ww2,
                         a0, wa1, wa2, r_ref, k_ref, v_ref, g_ref, a_ref, lw_ref, last_ref):
    xn = _rms(h_ref[...], nw_ref[...], RMS_EPS)
    outs = _rwkv_mix(xn, sh_ref[...], mu_ref[...], wr[...], wk[...], wv[...], wg[...], w0[...], ww1[...],
                     ww2[...], a0[...], wa1[...], wa2[...])
    for o_ref, o in zip((r_ref, k_ref, v_ref, g_ref, a_ref, lw_ref), outs):
        o_ref[...] = o
    last_ref[...] = xn


def _rwkv_weights(p):
    d = D_MODEL
    vec = lambda x: x.reshape(1, d)
    args = [p['rw_mu'], p['rw_w_r'].astype(BF16), p['rw_w_k'].astype(BF16), p['rw_w_v'].astype(BF16),
            p['rw_w_g'].astype(BF16), vec(p['rw_w0']), p['rw_w_w1'].astype(BF16), p['rw_w_w2'].astype(BF16),
            vec(p['rw_a0']), p['rw_w_a1'].astype(BF16), p['rw_w_a2'].astype(BF16)]
    return args, [_full(a.shape) for a in args]


def _rwkv_proj_seq(h2, b, t, norm_w, p):
    m = b * t
    tm = min(_row_tile(t), 256)
    tps = t // tm
    row = pl.BlockSpec((tm, D_MODEL), lambda i: (i, 0))
    prev = pl.BlockSpec((SUBLANES, D_MODEL), lambda i: (jnp.maximum(i * (tm // SUBLANES) - 1, 0), 0))
    wargs, wspecs = _rwkv_weights(p)
    outs = pl.pallas_call(
        functools.partial(_rwkv_proj_seq_body, tps),
        grid=(m // tm,),
        in_specs=[row, prev, _full((1, D_MODEL))] + wspecs,
        out_specs=[row] * 6 + [pl.BlockSpec((1, 1, D_MODEL), lambda i: (i // tps, 0, 0))],
        out_shape=[jax.ShapeDtypeStruct((m, D_MODEL), F32)] * 6
        + [jax.ShapeDtypeStruct((b, 1, D_MODEL), F32)],
        compiler_params=_cparams("arbitrary"))(h2, h2, norm_w.reshape(1, D_MODEL), *wargs)
    return outs[:6], outs[6].reshape(b, D_MODEL)


def _rwkv_proj_step(h2, shift, norm_w, p):
    m = h2.shape[0]
    wargs, wspecs = _rwkv_weights(p)
    outs = pl.pallas_call(
        _rwkv_proj_step_body,
        grid=(1,),
        in_specs=[_full((m, D_MODEL)), _full((m, D_MODEL)), _full((1, D_MODEL))] + wspecs,
        out_specs=[_full((m, D_MODEL))] * 7,
        out_shape=[jax.ShapeDtypeStruct((m, D_MODEL), F32)] * 7,
        compiler_params=_cparams("arbitrary"))(h2, shift, norm_w.reshape(1, D_MODEL), *wargs)
    return outs[:6], outs[6]


def _pair_sum(x, m0):
    s0 = jnp.sum(jnp.where(m0, x, 0.0), axis=-1, keepdims=True)
    s1 = jnp.sum(jnp.where(m0, 0.0, x), axis=-1, keepdims=True)
    return jnp.where(m0, s0, s1)


def _rwkv_chunk_body(gsz, npair, r_ref, k_ref, v_ref, a_ref, lw_ref, kk_ref, ka_ref, rk_ref, gw_ref, gb_ref,
                     o_ref, s_ref, bd_ref):
    cg = pl.program_id(2)
    ncg = pl.num_programs(2)
    cl = RW_CHUNK
    two = 2 * cl

    @pl.when(cg == 0)
    def _():
        bd_ref[...] = jnp.zeros_like(bd_ref)

    lane = lax.broadcasted_iota(jnp.int32, (cl, LANES), 1)
    m0 = lane < RW_HEAD
    ti = lax.broadcasted_iota(jnp.int32, (cl, cl), 0)
    tj = lax.broadcasted_iota(jnp.int32, (cl, cl), 1)
    tri = jnp.where(ti >= tj, 1.0, 0.0).astype(BF16)
    row2 = lax.broadcasted_iota(jnp.int32, (two, LANES), 0)
    col2 = lax.broadcasted_iota(jnp.int32, (two, LANES), 1)
    same = (row2 >= cl) == (col2 >= RW_HEAD)
    step_r = jnp.bitwise_and(row2, cl - 1)
    step_c = jnp.bitwise_and(col2, cl - 1)
    bd_strict = jnp.logical_and(same, step_r > step_c)
    bd_incl = jnp.logical_and(same, step_r >= step_c)
    eye = row2 == col2

    def stack(x):
        return jnp.where(same, jnp.concatenate([x, x], axis=0), 0.0)

    chunks = range(gsz)
    st = []
    for pi in range(npair):
        cols = slice(pi * LANES, (pi + 1) * LANES)
        kk_w, ka_w, rk_w = kk_ref[:, cols], ka_ref[:, cols], rk_ref[:, cols]
        for gi in chunks:
            sl = slice(gi * cl, (gi + 1) * cl)
            r = r_ref[0, sl, cols]
            k = k_ref[0, sl, cols]
            v = v_ref[0, sl, cols]
            a = a_ref[0, sl, cols]
            lw = lw_ref[0, sl, cols]
            kkv = k * kk_w
            kap = kkv * lax.rsqrt(jnp.maximum(_pair_sum(kkv * kkv, m0), 1e-24))
            kmod = k * (1.0 + (a - 1.0) * ka_w)
            st.append(dict(r=r, v=v, lw=lw, kap=kap, kmod=kmod, bvec=kap * a,
                           bonus=_pair_sum(r * kmod * rk_w, m0) * v))
    for s in st:
        s['cum'] = _cumsum_rows(tri, s['lw'])
    for s in st:
        cum = s['cum']
        cum_last = cum[cl - 1:cl, :]
        g_inv = jnp.exp(-cum)
        g_end = jnp.exp(cum_last - cum)
        s['g_all'] = jnp.exp(cum_last)
        s['kapm'] = stack(s['kap'] * jnp.exp(cum - s['lw']))
        s['rm'] = stack(s['r'] * jnp.exp(cum))
        s['v2m'] = stack(s['v'])
        s['b_h'] = s['bvec'] * g_inv
        s['k_h'] = s['kmod'] * g_inv
        s['b_e'] = stack(s['bvec'] * g_end)
        s['k_e'] = stack(s['kmod'] * g_end)
    for s in st:
        lhs = jnp.concatenate([s['kapm'], s['rm']], axis=0)
        gb = _dot_nt(lhs, jnp.concatenate([s['b_h'], s['b_h']], axis=0))
        gk = _dot_nt(lhs, jnp.concatenate([s['k_h'], s['k_h']], axis=0))
        s['a_ab'] = jnp.where(bd_strict, gb[:two], 0.0)
        s['a_rb'] = jnp.where(bd_incl, gb[two:], 0.0)
        s['a_ak'] = jnp.where(bd_strict, gk[:two], 0.0)
        s['a_rk'] = jnp.where(bd_incl, gk[two:], 0.0)
    for s in st:
        s['tinv'] = jnp.where(eye, 1.0, 0.0) - s['a_ab']
        s['lp'] = _dot(s['a_ab'], s['a_ab'])
        s['x2m'] = _dot(s['a_ak'], s['v2m'])
    for _ in range(4):
        for s in st:
            both = _dot(jnp.concatenate([s['tinv'], s['lp']], axis=0), s['lp'])
            s['tinv'] = s['tinv'] + both[:two]
            s['lp'] = both[two:]
    for s in st:
        s['tinv'] = s['tinv'] + _dot(s['tinv'], s['lp'])
    for s in st:
        pq = _dot(s['tinv'], jnp.concatenate([s['kapm'], s['x2m']], axis=1))
        s['pm'] = -pq[:, :LANES]
        s['qm'] = -pq[:, LANES:]
    for s in st:
        ro = _dot(s['a_rb'], jnp.concatenate([s['pm'], s['qm']], axis=1))
        rtm = s['rm'] + ro[:, :LANES]
        oim = ro[:, LANES:] + _dot(s['a_rk'], s['v2m'])
        s['rt'] = rtm[:cl] + rtm[cl:]
        s['oi'] = oim[:cl] + oim[cl:]
    for s in st:
        s['corr'] = _dot_tn(s['pm'], s['b_e'])
        s['nmat'] = _dot_tn(jnp.concatenate([s['qm'], s['v2m']], axis=0),
                            jnp.concatenate([s['b_e'], s['k_e']], axis=0))
    bds = [bd_ref[pi] for pi in range(npair)]
    for gi in chunks:
        for pi in range(npair):
            s = st[pi * gsz + gi]
            cols = slice(pi * LANES, (pi + 1) * LANES)
            o = _dot_nt(s['rt'], bds[pi]) + s['oi']
            bds[pi] = bds[pi] * s['g_all'] + _dot(bds[pi], s['corr']) + s['nmat']
            mean = _pair_sum(o, m0) * (1.0 / RW_HEAD)
            oc = o - mean
            var = _pair_sum(oc * oc, m0) * (1.0 / RW_HEAD)
            o_ref[0, gi * cl:(gi + 1) * cl, cols] = (
                oc * lax.rsqrt(var + RW_GN_EPS) * gw_ref[:, cols] + gb_ref[:, cols] + s['bonus'])
    for pi in range(npair):
        bd_ref[pi] = bds[pi]

    @pl.when(cg == ncg - 1)
    def _():
        for pi in range(npair):
            s_ref[0, 2 * pi] = bds[pi][:RW_HEAD, :RW_HEAD]
            s_ref[0, 2 * pi + 1] = bds[pi][RW_HEAD:, RW_HEAD:]


def _rwkv_chunk(r, k, v, a, lw, b, t, p):
    npair = RW_PAIRS
    hp = RW_HEADS // (2 * npair)
    gsz = RW_GROUP if t % (RW_GROUP * RW_CHUNK) == 0 else 1
    rows = gsz * RW_CHUNK
    shp = lambda x: x.reshape(b, t, D_MODEL)
    blk = pl.BlockSpec((1, rows, npair * LANES), lambda bi, hi, ci: (bi, ci, hi))
    vec = pl.BlockSpec((1, npair * LANES), lambda bi, hi, ci: (0, hi))
    vrow = lambda x: x.reshape(1, D_MODEL)
    o, s = pl.pallas_call(
        functools.partial(_rwkv_chunk_body, gsz, npair),
        grid=(b, hp, t // rows),
        in_specs=[blk] * 5 + [vec] * 5,
        out_specs=[blk, pl.BlockSpec((1, 2 * npair, RW_HEAD, RW_HEAD), lambda bi, hi, ci: (bi, hi, 0, 0))],
        out_shape=[jax.ShapeDtypeStruct((b, t, D_MODEL), F32),
                   jax.ShapeDtypeStruct((b, RW_HEADS, RW_HEAD, RW_HEAD), F32)],
        scratch_shapes=[pltpu.VMEM((npair, LANES, LANES), F32)],
        compiler_params=_cparams("parallel", "parallel", "arbitrary"))(
            shp(r), shp(k), shp(v), shp(a), shp(lw), vrow(p['rw_k_k']), vrow(p['rw_k_a']),
            vrow(p['rw_r_k']), vrow(p['rw_gn_w']), vrow(p['rw_gn_b']))
    return o.reshape(b * t, D_MODEL), s


def _rwkv_step_body(r_ref, k_ref, vc_ref, a_ref, lw_ref, s_ref, kk_ref, ka_ref, rk_ref, gw_ref, gb_ref,
                    o_ref, so_ref):
    r = r_ref[0]
    k = k_ref[0]
    a = a_ref[0]
    lw = lw_ref[0]
    vc = vc_ref[0]
    s = s_ref[0]
    kkv = k * kk_ref[...]
    kap = kkv * lax.rsqrt(jnp.maximum(jnp.sum(kkv * kkv, axis=-1, keepdims=True), 1e-24))
    kmod = k * (1.0 + (a - 1.0) * ka_ref[...])
    bvec = kap * a
    dec = jnp.exp(lw)
    sa = jnp.sum(s * kap, axis=-1, keepdims=True)
    s_new = s * dec - sa * bvec + vc * kmod
    so_ref[0] = s_new
    o = jnp.sum(s_new * r, axis=-1, keepdims=True)
    mean = jnp.mean(o, axis=1, keepdims=True)
    oc = o - mean
    var = jnp.mean(oc * oc, axis=1, keepdims=True)
    on = oc * lax.rsqrt(var + RW_GN_EPS) * gw_ref[...] + gb_ref[...]
    bonus = jnp.sum(r * kmod * rk_ref[...], axis=-1, keepdims=True)
    o_ref[0] = on + bonus * vc


def _rwkv_step(r, k, v, a, lw, state, p):
    b = r.shape[0]
    hrow = (RW_HEADS, 1, RW_HEAD)
    hcol = (RW_HEADS, RW_HEAD, 1)
    rows = lambda x: x.reshape((b,) + hrow)
    rblk = pl.BlockSpec((1,) + hrow, lambda i: (i, 0, 0, 0))
    cblk = pl.BlockSpec((1,) + hcol, lambda i: (i, 0, 0, 0))
    sblk = pl.BlockSpec((1, RW_HEADS, RW_HEAD, RW_HEAD), lambda i: (i, 0, 0, 0))
    o, s = pl.pallas_call(
        _rwkv_step_body,
        grid=(b,),
        in_specs=[rblk, rblk, cblk, rblk, rblk, sblk] + [_full(hrow)] * 3 + [_full(hcol)] * 2,
        out_specs=[cblk, sblk],
        out_shape=[jax.ShapeDtypeStruct((b,) + hcol, F32),
                   jax.ShapeDtypeStruct((b, RW_HEADS, RW_HEAD, RW_HEAD), F32)],
        compiler_params=_cparams("parallel"))(
            rows(r), rows(k), v.reshape((b,) + hcol), rows(a), rows(lw), state,
            p['rw_k_k'].reshape(hrow), p['rw_k_a'].reshape(hrow), p['rw_r_k'].reshape(hrow),
            p['rw_gn_w'].reshape(hcol), p['rw_gn_b'].reshape(hcol))
    return o.reshape(b, D_MODEL), s


def _mamba_weights(p):
    w = p['mb_w_in']
    wz = w[:, :MB_INNER]
    wx = w[:, MB_INNER:MB_INNER + MB_CONV_DIM]
    wdt = jnp.pad(w[:, MB_INNER + MB_CONV_DIM:], ((0, 0), (0, LANES - MB_HEADS)))
    return [wz, wx, wdt]


def _mamba_vecs(p):
    pad = lambda x: jnp.pad(x.reshape(1, MB_HEADS), ((0, 0), (0, LANES - MB_HEADS)))
    return dict(conv_w=p['mb_conv_w'], conv_b=p['mb_conv_b'].reshape(1, MB_CONV_DIM),
                dt_bias=pad(p['mb_dt_bias']), a_log=pad(p['mb_a_log']),
                d_skip=jnp.repeat(p['mb_d'], MB_HEADDIM).reshape(1, MB_INNER),
                norm_w=p['mb_norm_w'].reshape(1, MB_INNER))


def _mamba_chunk_body(x_ref, xp_ref, z_ref, dt_ref, cw_ref, cb_ref, dtb_ref, alog_ref, dsk_ref, nw_ref,
                      y_ref, fs_ref, st_ref):
    c = pl.program_id(1)
    nc = pl.num_programs(1)
    cl = MB_CHUNK

    @pl.when(c == 0)
    def _():
        st_ref[...] = jnp.zeros_like(st_ref)

    hist = jnp.where(c == 0, 0.0, xp_ref[0])
    xcat = jnp.concatenate([hist, x_ref[0]], axis=0)
    cw = cw_ref[...]
    conv = cb_ref[...] + cw[0:1, :] * xcat[5:5 + cl]
    for j in range(1, MB_CONV):
        conv = conv + cw[j:j + 1, :] * xcat[5 + j:5 + j + cl]
    conv = _silu(conv)

    dt = _softplus(dt_ref[0] + dtb_ref[...])
    da = dt * (-jnp.exp(alog_ref[...]))
    ti = lax.broadcasted_iota(jnp.int32, (cl, cl), 0)
    tj = lax.broadcasted_iota(jnp.int32, (cl, cl), 1)
    causal = ti >= tj
    tri = jnp.where(causal, 1.0, 0.0).astype(BF16)
    a_cs = _cumsum_rows(tri, da)
    a_cs_t = a_cs.T
    a_last = a_cs[cl - 1:cl, :]
    dec_to_end = jnp.exp(a_last - a_cs)
    e_cs = jnp.exp(a_cs)
    e_last = jnp.exp(a_last)

    lane = lax.broadcasted_iota(jnp.int32, (cl, LANES), 1)
    m0 = lane < MB_HEADDIM
    heads_per_group = MB_HEADS // MB_GROUPS
    pairs_per_group = heads_per_group // 2
    for g in range(MB_GROUPS):
        bm = conv[:, MB_INNER + g * MB_STATE:MB_INNER + (g + 1) * MB_STATE]
        cm = conv[:, MB_INNER + (MB_GROUPS + g) * MB_STATE:MB_INNER + (MB_GROUPS + g + 1) * MB_STATE]
        cb = _dot_nt(cm, bm)
        bm_t = bm.T.astype(BF16)
        blocks = []
        for j in range(pairs_per_group):
            pi = g * pairs_per_group + j
            h0, h1 = 2 * pi, 2 * pi + 1
            lo, hi = pi * LANES, (pi + 1) * LANES
            xs = conv[:, lo:hi]
            dt2 = jnp.where(m0, dt[:, h0:h0 + 1], dt[:, h1:h1 + 1])
            xdt = xs * dt2
            l0 = jnp.exp(jnp.where(causal, a_cs[:, h0:h0 + 1] - a_cs_t[h0:h0 + 1, :], NEG_INF))
            l1 = jnp.exp(jnp.where(causal, a_cs[:, h1:h1 + 1] - a_cs_t[h1:h1 + 1, :], NEG_INF))
            lhs = jnp.concatenate([(cb * l0).astype(BF16), (cb * l1).astype(BF16)], axis=1)
            rhs = jnp.concatenate([jnp.where(m0, xdt, 0.0).astype(BF16),
                                   jnp.where(m0, 0.0, xdt).astype(BF16)], axis=0)
            y = jnp.dot(lhs, rhs, preferred_element_type=F32)
            st = st_ref[pi]
            e2 = jnp.where(m0, e_cs[:, h0:h0 + 1], e_cs[:, h1:h1 + 1])
            y = y + _dot(cm, st) * e2
            d2 = jnp.where(m0, dec_to_end[:, h0:h0 + 1], dec_to_end[:, h1:h1 + 1])
            sc = jnp.where(m0, e_last[:, h0:h0 + 1], e_last[:, h1:h1 + 1])
            st_ref[pi] = st * sc + jnp.dot(bm_t, (xdt * d2).astype(BF16), preferred_element_type=F32)
            y = y + dsk_ref[:, lo:hi] * xs
            blocks.append(y * _silu(z_ref[0, :, lo:hi]))
        ssq = blocks[0] * blocks[0]
        for blk in blocks[1:]:
            ssq = ssq + blk * blk
        scale = lax.rsqrt(jnp.sum(ssq, axis=-1, keepdims=True) * (1.0 / MB_GROUP_WIDTH) + MB_NORM_EPS)
        for j, blk in enumerate(blocks):
            lo = (g * pairs_per_group + j) * LANES
            y_ref[0, :, lo:lo + LANES] = blk * scale * nw_ref[:, lo:lo + LANES]

    @pl.when(c == nc - 1)
    def _():
        for pi in range(MB_HEADS // 2):
            fs_ref[0, pi] = st_ref[pi].T


def _mamba_chunk(z, xbc, dtr, b, t, vecs):
    nc = t // MB_CHUNK
    npairs = MB_HEADS // 2
    per = MB_CHUNK // SUBLANES
    y, fs = pl.pallas_call(
        _mamba_chunk_body,
        grid=(b, nc),
        in_specs=[pl.BlockSpec((1, MB_CHUNK, MB_CONV_DIM), lambda bi, ci: (bi, ci, 0)),
                  pl.BlockSpec((1, SUBLANES, MB_CONV_DIM),
                               lambda bi, ci: (bi, jnp.maximum(ci * per - 1, 0), 0)),
                  pl.BlockSpec((1, MB_CHUNK, MB_INNER), lambda bi, ci: (bi, ci, 0)),
                  pl.BlockSpec((1, MB_CHUNK, LANES), lambda bi, ci: (bi, ci, 0)),
                  _full((MB_CONV, MB_CONV_DIM)), _full((1, MB_CONV_DIM)), _full((1, LANES)),
                  _full((1, LANES)), _full((1, MB_INNER)), _full((1, MB_INNER))],
        out_specs=[pl.BlockSpec((1, MB_CHUNK, MB_INNER), lambda bi, ci: (bi, ci, 0)),
                   pl.BlockSpec((1, npairs, LANES, MB_STATE), lambda bi, ci: (bi, 0, 0, 0))],
        out_shape=[jax.ShapeDtypeStruct((b, t, MB_INNER), F32),
                   jax.ShapeDtypeStruct((b, npairs, LANES, MB_STATE), F32)],
        scratch_shapes=[pltpu.VMEM((npairs, LANES, MB_STATE), F32)],
        compiler_params=_cparams("parallel", "arbitrary"))(
            xbc.reshape(b, t, MB_CONV_DIM), xbc.reshape(b, t, MB_CONV_DIM), z.reshape(b, t, MB_INNER),
            dtr.reshape(b, t, LANES), vecs['conv_w'], vecs['conv_b'], vecs['dt_bias'], vecs['a_log'],
            vecs['d_skip'], vecs['norm_w'])
    return y.reshape(b * t, MB_INNER), fs.reshape(b, MB_HEADS, MB_HEADDIM, MB_STATE)


def _mamba_step_conv_body(x_ref, cs_ref, dt_ref, cw_ref, cb_ref, dtb_ref, alog_ref, conv_ref, dt_o, ed_o):
    cw = cw_ref[...]
    conv = cb_ref[...] + cw[MB_CONV - 1:MB_CONV, :] * x_ref[...]
    for j in range(MB_CONV - 1):
        conv = conv + cw[j:j + 1, :] * cs_ref[:, j, :]
    conv_ref[...] = _silu(conv)
    dt = _softplus(dt_ref[...] + dtb_ref[...])
    dt_o[...] = dt
    ed_o[...] = jnp.exp(dt * (-jnp.exp(alog_ref[...])))


def _mamba_step_state_body(s_ref, xc_ref, dt_ref, ed_ref, bm_ref, cm_ref, so_ref, y_ref):
    hpg = MB_HEADS // MB_GROUPS
    for g in range(MB_GROUPS):
        hs = slice(g * hpg, (g + 1) * hpg)
        s = s_ref[0, hs]
        xdt = xc_ref[0, hs] * dt_ref[0, hs]
        fin = s * ed_ref[0, hs] + xdt * bm_ref[0, g:g + 1]
        so_ref[0, hs] = fin
        y_ref[0, hs] = jnp.sum(fin * cm_ref[0, g:g + 1], axis=-1, keepdims=True)


def _mamba_step_out_body(y_ref, x_ref, z_ref, dsk_ref, nw_ref, o_ref):
    y = (y_ref[...] + dsk_ref[...] * x_ref[...]) * _silu(z_ref[...])
    for g in range(MB_GROUPS):
        lo, hi = g * MB_GROUP_WIDTH, (g + 1) * MB_GROUP_WIDTH
        o_ref[:, lo:hi] = _rms(y[:, lo:hi], nw_ref[:, lo:hi], MB_NORM_EPS)


def _mamba_step(z, xbc, dtr, conv_state, ssm_state, vecs):
    b = z.shape[0]
    conv, dt, ed = pl.pallas_call(
        _mamba_step_conv_body,
        grid=(1,),
        in_specs=[_full((b, MB_CONV_DIM)), _full((b, MB_CONV - 1, MB_CONV_DIM)), _full((b, LANES)),
                  _full((MB_CONV, MB_CONV_DIM)), _full((1, MB_CONV_DIM)), _full((1, LANES)), _full((1, LANES))],
        out_specs=[_full((b, MB_CONV_DIM)), _full((b, LANES)), _full((b, LANES))],
        out_shape=[jax.ShapeDtypeStruct((b, MB_CONV_DIM), F32), jax.ShapeDtypeStruct((b, LANES), F32),
                   jax.ShapeDtypeStruct((b, LANES), F32)],
        compiler_params=_cparams("arbitrary"))(
            xbc, conv_state, dtr, vecs['conv_w'], vecs['conv_b'], vecs['dt_bias'], vecs['a_log'])
    xs = conv[:, :MB_INNER]
    bm = conv[:, MB_INNER:MB_INNER + MB_GROUPS * MB_STATE].reshape(b, MB_GROUPS, MB_STATE)
    cm = conv[:, MB_INNER + MB_GROUPS * MB_STATE:].reshape(b, MB_GROUPS, MB_STATE)
    hp1 = (MB_HEADS, MB_HEADDIM, 1)
    h11 = (MB_HEADS, 1, 1)
    per = lambda shape: pl.BlockSpec((1,) + shape, lambda i: (i,) + (0,) * len(shape))
    st_shape = (MB_HEADS, MB_HEADDIM, MB_STATE)
    fin, y = pl.pallas_call(
        _mamba_step_state_body,
        grid=(b,),
        in_specs=[per(st_shape), per(hp1), per(h11), per(h11), per((MB_GROUPS, MB_STATE)),
                  per((MB_GROUPS, MB_STATE))],
        out_specs=[per(st_shape), per(hp1)],
        out_shape=[jax.ShapeDtypeStruct((b,) + st_shape, F32), jax.ShapeDtypeStruct((b,) + hp1, F32)],
        compiler_params=_cparams("parallel"))(
            ssm_state, xs.reshape((b,) + hp1), dt[:, :MB_HEADS].reshape((b,) + h11),
            ed[:, :MB_HEADS].reshape((b,) + h11), bm, cm)
    yg = pl.pallas_call(
        _mamba_step_out_body,
        grid=(1,),
        in_specs=[_full((b, MB_INNER))] * 3 + [_full((1, MB_INNER))] * 2,
        out_specs=_full((b, MB_INNER)),
        out_shape=jax.ShapeDtypeStruct((b, MB_INNER), F32),
        compiler_params=_cparams("arbitrary"))(
            y.reshape(b, MB_INNER), xs, z, vecs['d_skip'], vecs['norm_w'])
    return yg, fin


def _rope_tables(pos):
    half = ROPE_DIM // 2
    inv_freq = ROPE_THETA ** (-jnp.arange(0, ROPE_DIM, 2, dtype=F32) / ROPE_DIM)
    ang = pos.astype(F32)[:, None] * inv_freq[None, :]
    cos, sin = jnp.cos(ang), jnp.sin(ang)
    n = pos.shape[0]
    ones = jnp.ones((n, DA_HEAD - ROPE_DIM), F32)
    zeros = jnp.zeros((n, DA_HEAD - ROPE_DIM), F32)
    zh = jnp.zeros((n, half), F32)
    ct = jnp.concatenate([cos, cos, ones], axis=1)
    s_up = jnp.concatenate([-sin, zh, zeros], axis=1)
    s_dn = jnp.concatenate([zh, sin, zeros], axis=1)
    two = lambda x: jnp.concatenate([x, x], axis=1)
    return two(ct), two(s_up), two(s_dn)


def _rope(x, ct, s_up, s_dn):
    half = ROPE_DIM // 2
    cols = []
    for j in range(x.shape[1] // LANES):
        xj = x[:, j * LANES:(j + 1) * LANES]
        cols.append(xj * ct + pltpu.roll(xj, LANES - half, axis=1) * s_up + pltpu.roll(xj, half, axis=1) * s_dn)
    return jnp.concatenate(cols, axis=1)


def _da_epilogue(ys, rows, fulls):
    y = ys[0]
    ct, s_up, s_dn = rows
    q = _rope(y[:, :DA_WIDTH], ct, s_up, s_dn)
    k = _rope(y[:, DA_WIDTH:2 * DA_WIDTH], ct, s_up, s_dn)
    return q, k, y[:, 2 * DA_WIDTH:3 * DA_WIDTH], y[:, 3 * DA_WIDTH:]


def _lambda(lq1, lk1, lq2, lk2, lam_init):
    return (jnp.exp(jnp.sum(lq1 * lk1, axis=-1, keepdims=True))
            - jnp.exp(jnp.sum(lq2 * lk2, axis=-1, keepdims=True)) + lam_init)


def _da_epilogue_seq(ys, rows, fulls):
    q, k, v, g = _da_epilogue(ys, rows, fulls)
    return k, v, g, (q * (DA_SCALE * LOG2_E)).T, k, v.T


def _flash_body(lam_init, qi_ref, kj_ref, qt_ref, k_ref, vt_ref, lq1, lk1, lq2, lk2, sw_ref, o_ref,
                m1, l1, m2, l2, acc1, acc2):
    step = pl.program_id(2)
    qi = qi_ref[step]
    kj = kj_ref[step]

    @pl.when(kj == 0)
    def _():
        for m in (m1, m2):
            m[...] = jnp.full_like(m, NEG_INF)
        for z in (l1, l2, acc1, acc2):
            z[...] = jnp.zeros_like(z)

    def update(masked):
        qt = qt_ref[...]
        k = k_ref[...]
        vt = vt_ref[...]
        tk, tq = k.shape[0], qt.shape[1]
        frow = lax.broadcasted_iota(jnp.int32, qt.shape, 0)
        zero = jnp.zeros_like(qt)
        if masked:
            keep = (lax.broadcasted_iota(jnp.int32, (tk, tq), 0)
                    <= lax.broadcasted_iota(jnp.int32, (tk, tq), 1))
        scores = [jnp.dot(k, jnp.where(frow < DA_HEAD, qt, zero), preferred_element_type=F32),
                  jnp.dot(k, jnp.where(frow < DA_HEAD, zero, qt), preferred_element_type=F32)]
        for s, (m_ref, l_ref, acc_ref) in zip(scores, ((m1, l1, acc1), (m2, l2, acc2))):
            if masked:
                s = jnp.where(keep, s, NEG_INF)
            m_old = m_ref[...]
            m_new = jnp.maximum(m_old, jnp.max(s, axis=0, keepdims=True))
            alpha = jnp.exp2(m_old - m_new)
            pr = jnp.exp2(s - m_new)
            l_ref[...] = l_ref[...] * alpha + jnp.sum(pr, axis=0, keepdims=True)
            acc_ref[...] = acc_ref[...] * alpha + jnp.dot(vt, pr.astype(BF16), preferred_element_type=F32)
            m_ref[...] = m_new

    @pl.when(kj < qi)
    def _():
        update(False)

    @pl.when(kj == qi)
    def _():
        update(True)
        lam = _lambda(lq1[...], lk1[...], lq2[...], lk2[...], lam_init)
        ot = acc1[...] / l1[...] - lam * (acc2[...] / l2[...])
        scale = lax.rsqrt(jnp.mean(ot * ot, axis=0, keepdims=True) + DA_SUBLN_EPS)
        o_ref[...] = (ot * scale * (sw_ref[...] * (1.0 - lam_init))).T


def _flash(qt, kb, vt, b, t, p, lam_init):
    tq = min(t, 512)
    nq = t // tq
    pairs = [(qi, kj) for qi in range(nq) for kj in range(qi + 1)]
    qi_tab = jnp.asarray([pr[0] for pr in pairs], jnp.int32)
    kj_tab = jnp.asarray([pr[1] for pr in pairs], jnp.int32)
    qblk = pl.BlockSpec((DA_VDIM, tq), lambda bi, hi, s, qt_, kt_: (hi, bi * nq + qt_[s]))
    vblk = pl.BlockSpec((DA_VDIM, tq), lambda bi, hi, s, qt_, kt_: (hi, bi * nq + kt_[s]))
    kblk = pl.BlockSpec((tq, DA_VDIM), lambda bi, hi, s, qt_, kt_: (bi * nq + kt_[s], hi))
    oblk = pl.BlockSpec((tq, DA_VDIM), lambda bi, hi, s, qt_, kt_: (bi * nq + qt_[s], hi))
    small = lambda shape: pl.BlockSpec(shape, lambda bi, hi, s, qt_, kt_: (0, 0))
    vec = lambda x: x.reshape(1, -1)
    stat = pltpu.VMEM((1, tq), F32)
    acc = pltpu.VMEM((DA_VDIM, tq), F32)
    return pl.pallas_call(
        functools.partial(_flash_body, lam_init),
        grid_spec=pltpu.PrefetchScalarGridSpec(
            num_scalar_prefetch=2, grid=(b, DA_HEADS, len(pairs)),
            in_specs=[qblk, kblk, vblk] + [small((1, DA_HEAD))] * 4 + [small((DA_VDIM, 1))],
            out_specs=oblk,
            scratch_shapes=[stat, stat, stat, stat, acc, acc]),
        out_shape=jax.ShapeDtypeStruct((b * t, DA_WIDTH), F32),
        compiler_params=_cparams("parallel", "parallel", "arbitrary"))(
            qi_tab, kj_tab, qt, kb, vt, vec(p['da_lq1']), vec(p['da_lk1']), vec(p['da_lq2']),
            vec(p['da_lk2']), p['da_subln_w'].reshape(DA_VDIM, 1))


def _decode_body(lam_init, npg, pt_ref, q_ref, kn_ref, vn_ref, *refs):
    ck_refs = refs[:npg]
    cv_refs = refs[npg:2 * npg]
    lq1, lk1, lq2, lk2, sw_ref, o_ref, qm_s, m_s, l_s, acc = refs[2 * npg:]
    j = pl.program_id(1)
    nj = pl.num_programs(1)
    nrow = 2 * DA_HEADS
    ncol = PAGE_SIZE * DA_HEADS
    rid = lax.broadcasted_iota(jnp.int32, (nrow, LANES), 0)

    @pl.when(j == 0)
    def _():
        comp = lax.broadcasted_iota(jnp.int32, (nrow, LANES), 1) // DA_HEAD
        qm = jnp.where(jnp.bitwise_and(rid, 1) == comp, q_ref[0] * DA_SCALE, 0.0)
        qm_s[...] = qm
        m_s[...] = jnp.sum(qm * kn_ref[0], axis=-1, keepdims=True)
        l_s[...] = jnp.ones_like(l_s)
        acc[...] = vn_ref[0]

    qm = qm_s[...]
    keep = (lax.rem(lax.broadcasted_iota(jnp.int32, (nrow, ncol), 1), DA_HEADS)
            == lax.broadcasted_iota(jnp.int32, (nrow, ncol), 0) // 2)
    scores = [jnp.where(keep, _dot_nt(qm, ck[0].reshape(ncol, DA_VDIM)), NEG_INF) for ck in ck_refs]
    m_old = m_s[...]
    m_new = m_old
    for s in scores:
        m_new = jnp.maximum(m_new, jnp.max(s, axis=-1, keepdims=True))
    alpha = jnp.exp(m_old - m_new)
    l_new = l_s[...] * alpha
    a_new = acc[...] * alpha
    for s, cv in zip(scores, cv_refs):
        pr = jnp.exp(s - m_new)
        l_new = l_new + jnp.sum(pr, axis=-1, keepdims=True)
        a_new = a_new + _dot(pr, cv[0].reshape(ncol, DA_VDIM))
    m_s[...] = m_new
    l_s[...] = l_new
    acc[...] = a_new

    @pl.when(j == nj - 1)
    def _():
        lam = _lambda(lq1[...], lk1[...], lq2[...], lk2[...], lam_init)
        d = (a_new / l_new) * jnp.where(jnp.bitwise_and(rid, 1) == 0, 1.0, -lam)
        pair = d + pltpu.roll(d, nrow - 1, axis=0)
        o_ref[0] = _rms(pair, sw_ref[...], DA_SUBLN_EPS) * (1.0 - lam_init)


def _decode(q, k_new, v_new, cache_k, cache_v, page_table, p, lam_init):
    b = q.shape[0]
    n_pages = page_table.shape[1]
    npg = DEC_PAGES if n_pages % DEC_PAGES == 0 else 1
    nrow = 2 * DA_HEADS
    rows16 = lambda x: jnp.repeat(x.reshape(b, DA_HEADS, DA_VDIM), 2, axis=1)
    row = pl.BlockSpec((1, nrow, DA_VDIM), lambda bi, j, pt: (bi, 0, 0))
    pages = [pl.BlockSpec((1, PAGE_SIZE, DA_HEADS, DA_VDIM),
                          lambda bi, j, pt, i=i: (pt[bi, j * npg + i], 0, 0, 0)) for i in range(npg)]
    vec = lambda x: x.reshape(1, -1)
    small = lambda n: pl.BlockSpec((1, n), lambda bi, j, pt: (0, 0))
    o = pl.pallas_call(
        functools.partial(_decode_body, lam_init, npg),
        grid_spec=pltpu.PrefetchScalarGridSpec(
            num_scalar_prefetch=1, grid=(b, n_pages // npg),
            in_specs=[row, row, row] + pages + pages + [small(DA_HEAD)] * 4 + [small(DA_VDIM)],
            out_specs=row,
            scratch_shapes=[pltpu.VMEM((nrow, DA_VDIM), F32), pltpu.VMEM((nrow, 1), F32),
                            pltpu.VMEM((nrow, 1), F32), pltpu.VMEM((nrow, DA_VDIM), F32)]),
        out_shape=jax.ShapeDtypeStruct((b, nrow, DA_VDIM), F32),
        compiler_params=_cparams("parallel", "arbitrary"))(
            page_table, rows16(q), rows16(k_new), rows16(v_new), *([cache_k] * npg), *([cache_v] * npg),
            vec(p['da_lq1']), vec(p['da_lk1']), vec(p['da_lq2']), vec(p['da_lk2']), vec(p['da_subln_w']))
    return o[:, ::2, :].reshape(b, DA_WIDTH)


def _cf_epilogue(ys, rows, fulls):
    y = ys[0]
    u = y[:, :CF_WIDTH] * _sigmoid(y[:, CF_WIDTH:2 * CF_WIDTH])
    return u, y[:, 2 * CF_WIDTH:]


def _cf_conv_body(tc, u_ref, up_ref, cw_ref, cb_ref, lw_ref, lb_ref, o_ref, buf, sh):
    c = pl.program_id(1)
    buf[0:CF_HIST, :] = jnp.where(c == 0, 0.0, up_ref[0])
    buf[CF_HIST:CF_HIST + tc, :] = u_ref[0]
    off = CF_HIST - (CF_KERNEL - 1)
    span = tc + CF_HIST - SUBLANES
    for r in range(1, SUBLANES):
        sh[r - 1, 0:span, :] = buf[r:r + span, :]
    rb = CF_ROWS
    for i in range(tc // rb):
        acc = cb_ref[...]
        for j in range(CF_KERNEL):
            r, base = (off + j) % SUBLANES, (off + j) // SUBLANES * SUBLANES
            rows = slice(base + i * rb, base + (i + 1) * rb)
            acc = acc + cw_ref[j:j + 1, :] * (buf[rows, :] if r == 0 else sh[r - 1, rows, :])
        mu = jnp.mean(acc, axis=-1, keepdims=True)
        xc = acc - mu
        var = jnp.mean(xc * xc, axis=-1, keepdims=True)
        o_ref[0, i * rb:(i + 1) * rb, :] = _silu(xc * lax.rsqrt(var + CF_LN_EPS) * lw_ref[...] + lb_ref[...])


def _cf_conv(u, b, t, p):
    tc = min(t, 256)
    per = tc // CF_HIST
    blk = pl.BlockSpec((1, tc, CF_WIDTH), lambda bi, ci: (bi, ci, 0))
    prev = pl.BlockSpec((1, CF_HIST, CF_WIDTH), lambda bi, ci: (bi, jnp.maximum(ci * per - 1, 0), 0))
    vec = lambda x: x.reshape(1, CF_WIDTH)
    u3 = u.reshape(b, t, CF_WIDTH)
    o = pl.pallas_call(
        functools.partial(_cf_conv_body, tc),
        grid=(b, t // tc),
        in_specs=[blk, prev, _full((CF_KERNEL, CF_WIDTH))] + [_full((1, CF_WIDTH))] * 3,
        out_specs=blk,
        out_shape=jax.ShapeDtypeStruct((b, t, CF_WIDTH), F32),
        scratch_shapes=[pltpu.VMEM((CF_HIST + tc, CF_WIDTH), F32),
                        pltpu.VMEM((SUBLANES - 1, CF_HIST + tc - SUBLANES, CF_WIDTH), F32)],
        compiler_params=_cparams("parallel", "parallel"))(
            u3, u3, p['cf_conv_w'], vec(p['cf_conv_b']), vec(p['cf_ln_w']), vec(p['cf_ln_b']))
    return o.reshape(b * t, CF_WIDTH)


def _cf_step_body(u_ref, st_ref, cw_ref, cb_ref, lw_ref, lb_ref, o_ref):
    nb = u_ref.shape[0]
    cw = cw_ref[...]
    hist = jnp.concatenate([jnp.sum(st_ref[i] * cw[:CF_KERNEL - 1, :], axis=0, keepdims=True)
                            for i in range(nb)], axis=0)
    acc = cb_ref[...] + hist + cw[CF_KERNEL - 1:CF_KERNEL, :] * u_ref[...]
    mu = jnp.mean(acc, axis=-1, keepdims=True)
    xc = acc - mu
    var = jnp.mean(xc * xc, axis=-1, keepdims=True)
    o_ref[...] = _silu(xc * lax.rsqrt(var + CF_LN_EPS) * lw_ref[...] + lb_ref[...])


def _cf_step(u, state, p):
    b = u.shape[0]
    tb = SUBLANES
    vec = lambda x: x.reshape(1, CF_WIDTH)
    return pl.pallas_call(
        _cf_step_body,
        grid=(b // tb,),
        in_specs=[pl.BlockSpec((tb, CF_WIDTH), lambda i: (i, 0)),
                  pl.BlockSpec((tb, CF_KERNEL - 1, CF_WIDTH), lambda i: (i, 0, 0)),
                  _full((CF_KERNEL, CF_WIDTH))] + [_full((1, CF_WIDTH))] * 3,
        out_specs=pl.BlockSpec((tb, CF_WIDTH), lambda i: (i, 0)),
        out_shape=jax.ShapeDtypeStruct((b, CF_WIDTH), F32),
        compiler_params=_cparams("parallel"))(
            u, state, p['cf_conv_w'], vec(p['cf_conv_b']), vec(p['cf_ln_w']), vec(p['cf_ln_b']))


def _forward(x_prompt, x_sample, state_rwkv_shift, state_rwkv_wkv, state_mamba_conv, state_mamba_ssm,
             cache_k, cache_v, page_table, state_conformer_conv, p):
    bp, tp, d = x_prompt.shape
    bs = x_sample.shape[0]
    hp = x_prompt.reshape(bp * tp, d)
    hs = x_sample.reshape(bs, d)
    npre, npost = p['norm_pre'], p['norm_post']

    (r, k, v, g, a, lw), rw_shift_p = _rwkv_proj_seq(hp, bp, tp, npre[0], p)
    o, rw_wkv_p = _rwkv_chunk(r, k, v, a, lw, bp, tp, p)
    hp = _out_proj(o, g, hp, p['rw_w_o'], npost[0])
    (r, k, v, g, a, lw), rw_shift_s = _rwkv_proj_step(hs, state_rwkv_shift, npre[0], p)
    o, rw_wkv_s = _rwkv_step(r, k, v, a, lw, state_rwkv_wkv, p)
    hs = _out_proj(o, g, hs, p['rw_w_o'], npost[0])

    mw = _mamba_weights(p)
    mv = _mamba_vecs(p)
    ident = lambda ys, rows, fulls: ys
    widths = (MB_INNER, MB_CONV_DIM, LANES)
    z, xbc, dtr = _in_proj(hp, npre[1], mw, ident, widths, tm=256)
    yg, mb_ssm_p = _mamba_chunk(z, xbc, dtr, bp, tp, mv)
    mb_conv_p = xbc.reshape(bp, tp, MB_CONV_DIM)[:, tp - (MB_CONV - 1):]
    hp = _out_proj(yg, None, hp, p['mb_w_out'], npost[1])
    z, xbc, dtr = _in_proj(hs, npre[1], mw, ident, widths)
    yg, mb_ssm_s = _mamba_step(z, xbc, dtr, state_mamba_conv, state_mamba_ssm, mv)
    mb_conv_s = jnp.concatenate([state_mamba_conv[:, 1:], xbc[:, None, :]], axis=1)
    hs = _out_proj(yg, None, hs, p['mb_w_out'], npost[1])

    lam_init = 0.8 - 0.6 * math.exp(-0.3 * 2)
    tabs_p = _rope_tables(jnp.arange(tp, dtype=jnp.int32))
    tm = min(_row_tile(tp), 256)
    tps = tp // tm
    rows_p = [(tb, lambda i: (i % tps, 0), LANES) for tb in tabs_p]
    widths = ((DA_WIDTH, F32, 'heads'), (DA_WIDTH, F32, 'heads'), DA_WIDTH,
              (DA_WIDTH, BF16, True), (DA_WIDTH, BF16, False), (DA_WIDTH, BF16, True))
    k, v, g, qt, kb, vt = _in_proj(hp, npre[2], [p['da_w_qkvg']], _da_epilogue_seq, widths,
                                   row_extras=rows_p, tm=tm)
    k_rows_p = k.reshape(bp, tp, DA_HEADS, 2 * DA_HEAD)
    v_rows_p = v.reshape(bp, tp, DA_HEADS, DA_VDIM)
    o = _flash(qt, kb, vt, bp, tp, p, lam_init)
    hp = _out_proj(o, g, hp, p['da_w_o'], npost[2])
    tabs_s = _rope_tables(jnp.full((bs,), PAST_LEN, dtype=jnp.int32))
    rows_s = [(tb, lambda i: (i, 0), LANES) for tb in tabs_s]
    q, k, v, g = _in_proj(hs, npre[2], [p['da_w_qkvg']], _da_epilogue, (DA_WIDTH,) * 4, row_extras=rows_s)
    k_rows_s = k.reshape(bs, 1, DA_HEADS, 2 * DA_HEAD)
    v_rows_s = v.reshape(bs, 1, DA_HEADS, DA_VDIM)
    o = _decode(q, k, v, cache_k, cache_v, page_table, p, lam_init)
    hs = _out_proj(o, g, hs, p['da_w_o'], npost[2])

    u, g = _in_proj(hp, npre[3], [p['cf_w_in']], _cf_epilogue, (CF_WIDTH,) * 2)
    cf_conv_p = u.reshape(bp, tp, CF_WIDTH)[:, tp - (CF_KERNEL - 1):]
    c = _cf_conv(u, bp, tp, p)
    hp = _out_proj(c, g, hp, p['cf_w_out'], npost[3])
    u, g = _in_proj(hs, npre[3], [p['cf_w_in']], _cf_epilogue, (CF_WIDTH,) * 2)
    cf_conv_s = jnp.concatenate([state_conformer_conv[:, 1:], u[:, None, :]], axis=1)
    c = _cf_step(u, state_conformer_conv, p)
    hs = _out_proj(c, g, hs, p['cf_w_out'], npost[3])

    return (hp.reshape(bp, tp, d), hs.reshape(bs, 1, d), rw_shift_p, rw_shift_s, rw_wkv_p, rw_wkv_s,
            mb_conv_p, mb_conv_s, mb_ssm_p, mb_ssm_s, k_rows_p, k_rows_s, v_rows_p, v_rows_s,
            cf_conv_p, cf_conv_s)


def kernel(x_prompt, x_sample, state_rwkv_shift, state_rwkv_wkv, state_mamba_conv, state_mamba_ssm, cache_k, cache_v, page_table, state_conformer_conv, norm_pre, norm_post, rw_mu, rw_w_r, rw_w_k, rw_w_v, rw_w_g, rw_w0, rw_w_w1, rw_w_w2, rw_a0, rw_w_a1, rw_w_a2, rw_k_k, rw_k_a, rw_r_k, rw_gn_w, rw_gn_b, rw_w_o, mb_w_in, mb_conv_w, mb_conv_b, mb_dt_bias, mb_a_log, mb_d, mb_norm_w, mb_w_out, da_w_qkvg, da_lq1, da_lk1, da_lq2, da_lk2, da_subln_w, da_w_o, cf_w_in, cf_conv_w, cf_conv_b, cf_ln_w, cf_ln_b, cf_w_out):
    p = dict(norm_pre=norm_pre, norm_post=norm_post, rw_mu=rw_mu, rw_w_r=rw_w_r, rw_w_k=rw_w_k, rw_w_v=rw_w_v,
             rw_w_g=rw_w_g, rw_w0=rw_w0, rw_w_w1=rw_w_w1, rw_w_w2=rw_w_w2, rw_a0=rw_a0, rw_w_a1=rw_w_a1,
             rw_w_a2=rw_w_a2, rw_k_k=rw_k_k, rw_k_a=rw_k_a, rw_r_k=rw_r_k, rw_gn_w=rw_gn_w, rw_gn_b=rw_gn_b,
             rw_w_o=rw_w_o, mb_w_in=mb_w_in, mb_conv_w=mb_conv_w, mb_conv_b=mb_conv_b, mb_dt_bias=mb_dt_bias,
             mb_a_log=mb_a_log, mb_d=mb_d, mb_norm_w=mb_norm_w, mb_w_out=mb_w_out, da_w_qkvg=da_w_qkvg,
             da_lq1=da_lq1, da_lk1=da_lk1, da_lq2=da_lq2, da_lk2=da_lk2, da_subln_w=da_subln_w, da_w_o=da_w_o,
             cf_w_in=cf_w_in, cf_conv_w=cf_conv_w, cf_conv_b=cf_conv_b, cf_ln_w=cf_ln_w, cf_ln_b=cf_ln_b,
             cf_w_out=cf_w_out)
    return _forward(x_prompt, x_sample, state_rwkv_shift, state_rwkv_wkv, state_mamba_conv, state_mamba_ssm,
                    cache_k, cache_v, page_table, state_conformer_conv, p)
```

```python
import functools
import math

import jax
import jax.numpy as jnp
from jax import lax
from jax.experimental import pallas as pl
from jax.experimental.pallas import tpu as pltpu

F32 = jnp.float32
BF16 = jnp.bfloat16

D_MODEL = 1024
PAST_LEN = 8192
PAGE_SIZE = 128
RMS_EPS = 1e-6

RW_HEAD = 64
RW_HEADS = D_MODEL // RW_HEAD
RW_LORA = 64
RW_GN_EPS = 64e-5
RW_CHUNK = 64
RW_GROUP = 8
RW_PAIRS = 2

MB_INNER = 2 * D_MODEL
MB_HEADDIM = 64
MB_HEADS = MB_INNER // MB_HEADDIM
MB_GROUPS = 4
MB_STATE = 128
MB_CONV = 4
MB_CHUNK = 128
MB_CONV_DIM = MB_INNER + 2 * MB_GROUPS * MB_STATE
MB_NORM_EPS = 1e-5
MB_GROUP_WIDTH = MB_INNER // MB_GROUPS

DA_HEADS = 8
DA_HEAD = 64
DA_VDIM = 2 * DA_HEAD
DA_WIDTH = DA_HEADS * DA_VDIM
DA_SCALE = DA_HEAD ** -0.5
LOG2_E = math.log2(math.e)
DA_HEADS_PER_STEP = 4
DA_SUBLN_EPS = 1e-5
ROPE_THETA = 500000.0
ROPE_DIM = DA_HEAD // 4
DEC_PAGES = 16

CF_WIDTH = D_MODEL
CF_KERNEL = 31
CF_LN_EPS = 1e-5
CF_HIST = 32
CF_ROWS = 32

LANES = 128
SUBLANES = 8
VMEM_LIMIT = 56 * 1024 * 1024

NEG_INF = float("-inf")


def _cparams(*sem):
    return pltpu.CompilerParams(dimension_semantics=sem, vmem_limit_bytes=VMEM_LIMIT)


def _dot(a, b):
    return jnp.dot(a.astype(BF16), b.astype(BF16), preferred_element_type=F32)


def _dot_nt(a, b):
    return lax.dot_general(a.astype(BF16), b.astype(BF16), (((1,), (1,)), ((), ())),
                           preferred_element_type=F32)


def _dot_tn(a, b):
    return lax.dot_general(a.astype(BF16), b.astype(BF16), (((0,), (0,)), ((), ())),
                           preferred_element_type=F32)


def _split3(a):
    hi = a.astype(BF16)
    r1 = a - hi.astype(F32)
    mid = r1.astype(BF16)
    lo = (r1 - mid.astype(F32)).astype(BF16)
    return hi, mid, lo


def _cumsum_rows(tri_bf16, x):
    hi, mid, lo = _split3(x)
    f = functools.partial(jnp.dot, preferred_element_type=F32)
    return f(tri_bf16, hi) + (f(tri_bf16, mid) + f(tri_bf16, lo))


def _sigmoid(x):
    return 1.0 / (1.0 + jnp.exp(-x))


def _silu(x):
    return x * _sigmoid(x)


def _softplus(x):
    return jnp.maximum(x, 0.0) + jnp.log(1.0 + jnp.exp(-jnp.abs(x)))


def _rms(x, w, eps):
    return x * lax.rsqrt(jnp.mean(x * x, axis=-1, keepdims=True) + eps) * w


def _row_tile(m):
    for t in (512, 256, 128, 64, 32, 16, 8):
        if m % t == 0:
            return t
    return m


def _full(shape):
    nd = len(shape)
    return pl.BlockSpec(shape, lambda *_: (0,) * nd)


def _out_proj_body(has_gate, *refs):
    if has_gate:
        y_ref, g_ref, h_ref, w_ref, pw_ref, o_ref = refs
        y = y_ref[...] * _silu(g_ref[...])
    else:
        y_ref, h_ref, w_ref, pw_ref, o_ref = refs
        y = y_ref[...]
    z = _dot(y, w_ref[...])
    o_ref[...] = h_ref[...] + _rms(z, pw_ref[...], RMS_EPS)


def _out_proj(y, gate, h, w, post_w):
    m, kdim = y.shape
    tm = _row_tile(m)
    row = lambda width: pl.BlockSpec((tm, width), lambda i: (i, 0))
    args, specs = [y], [row(kdim)]
    if gate is not None:
        args.append(gate)
        specs.append(row(D_MODEL))
    args += [h, w.astype(BF16), post_w.reshape(1, D_MODEL)]
    specs += [row(D_MODEL), _full((kdim, D_MODEL)), _full((1, D_MODEL))]
    return pl.pallas_call(
        functools.partial(_out_proj_body, gate is not None),
        grid=(m // tm,), in_specs=specs, out_specs=row(D_MODEL),
        out_shape=jax.ShapeDtypeStruct((m, D_MODEL), F32),
        compiler_params=_cparams("parallel"))(*args)


def _in_proj_body(n_w, n_row, n_full, epilogue, *refs):
    h_ref, nw_ref = refs[0], refs[1]
    w_refs = refs[2:2 + n_w]
    row_refs = refs[2 + n_w:2 + n_w + n_row]
    full_refs = refs[2 + n_w + n_row:2 + n_w + n_row + n_full]
    out_refs = refs[2 + n_w + n_row + n_full:]
    xn = _rms(h_ref[...], nw_ref[...], RMS_EPS).astype(BF16)
    ys = [jnp.dot(xn, w[...], preferred_element_type=F32) for w in w_refs]
    outs = epilogue(ys, [r[...] for r in row_refs], [r[...] for r in full_refs])
    for o_ref, o in zip(out_refs, outs):
        o_ref[...] = o.astype(o_ref.dtype).reshape(o_ref.shape)


def _in_proj(h, norm_w, weights, epilogue, out_widths, row_extras=(), full_extras=(), tm=None):
    m = h.shape[0]
    tm = tm or _row_tile(m)
    row = lambda width: pl.BlockSpec((tm, width), lambda i: (i, 0))
    args = [h, norm_w.reshape(1, D_MODEL)] + [w.astype(BF16) for w in weights]
    specs = [row(D_MODEL), _full((1, D_MODEL))] + [_full(w.shape) for w in weights]
    for arr, imap, width in row_extras:
        args.append(arr)
        specs.append(pl.BlockSpec((tm, width), imap))
    for arr in full_extras:
        args.append(arr)
        specs.append(_full(arr.shape))
    out_specs, out_shape = [], []
    for wd in out_widths:
        width, dtype, transposed = wd if isinstance(wd, tuple) else (wd, F32, False)
        if transposed == 'heads':
            shape3 = (width // LANES, LANES)
            out_specs.append(pl.BlockSpec((tm,) + shape3, lambda i: (i, 0, 0)))
            out_shape.append(jax.ShapeDtypeStruct((m,) + shape3, dtype))
        elif transposed:
            out_specs.append(pl.BlockSpec((width, tm), lambda i: (0, i)))
            out_shape.append(jax.ShapeDtypeStruct((width, m), dtype))
        else:
            out_specs.append(row(width))
            out_shape.append(jax.ShapeDtypeStruct((m, width), dtype))
    return pl.pallas_call(
        functools.partial(_in_proj_body, len(weights), len(row_extras), len(full_extras), epilogue),
        grid=(m // tm,), in_specs=specs, out_specs=out_specs, out_shape=out_shape,
        compiler_params=_cparams("parallel"))(*args)


def _rwkv_mix(xn, xprev, mu, wr, wk, wv, wg, w0, ww1, ww2, a0, wa1, wa2):
    xx = xprev - xn
    xm = [xn + xx * mu[i:i + 1, :] for i in range(6)]
    r = _dot(xm[0], wr)
    k = _dot(xm[2], wk)
    v = _dot(xm[3], wv)
    g = _dot(xm[5], wg)
    zw = w0 + _dot(jnp.tanh(_dot(xm[1], ww1)), ww2)
    w_log = -_softplus(-zw) - 0.5
    lw = -jnp.exp(w_log)
    a = _sigmoid(a0 + _dot(_dot(xm[4], wa1), wa2))
    return r, k, v, g, a, lw


def _rwkv_proj_seq_body(tiles_per_seq, h_ref, hp_ref, nw_ref, mu_ref, wr, wk, wv, wg, w0, ww1, ww2,
                        a0, wa1, wa2, r_ref, k_ref, v_ref, g_ref, a_ref, lw_ref, last_ref):
    i = pl.program_id(0)
    nw = nw_ref[...]
    xn = _rms(h_ref[...], nw, RMS_EPS)
    tm = xn.shape[0]
    pn = _rms(hp_ref[...], nw, RMS_EPS)[SUBLANES - 1:SUBLANES, :]
    first = (i % tiles_per_seq) == 0
    prow = jnp.where(first, 0.0, pn)
    rows = lax.broadcasted_iota(jnp.int32, xn.shape, 0)
    xprev = jnp.where(rows == 0, prow, pltpu.roll(xn, 1, axis=0))
    outs = _rwkv_mix(xn, xprev, mu_ref[...], wr[...], wk[...], wv[...], wg[...], w0[...], ww1[...],
                     ww2[...], a0[...], wa1[...], wa2[...])
    for o_ref, o in zip((r_ref, k_ref, v_ref, g_ref, a_ref, lw_ref), outs):
        o_ref[...] = o
    last_ref[0] = xn[tm - 1:tm, :]


def _rwkv_proj_step_body(h_ref, sh_ref, nw_ref, mu_ref, wr, wk, wv, wg, w0, ww1, ww2,
                         a0, wa1, wa2, r_ref, k_ref, v_ref, g_ref, a_ref, lw_ref, last_ref):
    xn = _rms(h_ref[...], nw_ref[...], RMS_EPS)
    outs = _rwkv_mix(xn, sh_ref[...], mu_ref[...], wr[...], wk[...], wv[...], wg[...], w0[...], ww1[...],
                     ww2[...], a0[...], wa1[...], wa2[...])
    for o_ref, o in zip((r_ref, k_ref, v_ref, g_ref, a_ref, lw_ref), outs):
        o_ref[...] = o
    last_ref[...] = xn


def _rwkv_weights(p):
    d = D_MODEL
    vec = lambda x: x.reshape(1, d)
    args = [p['rw_mu'], p['rw_w_r'].astype(BF16), p['rw_w_k'].astype(BF16), p['rw_w_v'].astype(BF16),
            p['rw_w_g'].astype(BF16), vec(p['rw_w0']), p['rw_w_w1'].astype(BF16), p['rw_w_w2'].astype(BF16),
            vec(p['rw_a0']), p['rw_w_a1'].astype(BF16), p['rw_w_a2'].astype(BF16)]
    return args, [_full(a.shape) for a in args]


def _rwkv_proj_seq(h2, b, t, norm_w, p):
    m = b * t
    tm = min(_row_tile(t), 256)
    tps = t // tm
    row = pl.BlockSpec((tm, D_MODEL), lambda i: (i, 0))
    prev = pl.BlockSpec((SUBLANES, D_MODEL), lambda i: (jnp.maximum(i * (tm // SUBLANES) - 1, 0), 0))
    wargs, wspecs = _rwkv_weights(p)
    outs = pl.pallas_call(
        functools.partial(_rwkv_proj_seq_body, tps),
        grid=(m // tm,),
        in_specs=[row, prev, _full((1, D_MODEL))] + wspecs,
        out_specs=[row] * 6 + [pl.BlockSpec((1, 1, D_MODEL), lambda i: (i // tps, 0, 0))],
        out_shape=[jax.ShapeDtypeStruct((m, D_MODEL), F32)] * 6
        + [jax.ShapeDtypeStruct((b, 1, D_MODEL), F32)],
        compiler_params=_cparams("arbitrary"))(h2, h2, norm_w.reshape(1, D_MODEL), *wargs)
    return outs[:6], outs[6].reshape(b, D_MODEL)


def _rwkv_proj_step(h2, shift, norm_w, p):
    m = h2.shape[0]
    wargs, wspecs = _rwkv_weights(p)
    outs = pl.pallas_call(
        _rwkv_proj_step_body,
        grid=(1,),
        in_specs=[_full((m, D_MODEL)), _full((m, D_MODEL)), _full((1, D_MODEL))] + wspecs,
        out_specs=[_full((m, D_MODEL))] * 7,
        out_shape=[jax.ShapeDtypeStruct((m, D_MODEL), F32)] * 7,
        compiler_params=_cparams("arbitrary"))(h2, shift, norm_w.reshape(1, D_MODEL), *wargs)
    return outs[:6], outs[6]


def _pair_sum(x, m0):
    s0 = jnp.sum(jnp.where(m0, x, 0.0), axis=-1, keepdims=True)
    s1 = jnp.sum(jnp.where(m0, 0.0, x), axis=-1, keepdims=True)
    return jnp.where(m0, s0, s1)


def _rwkv_chunk_body(gsz, npair, r_ref, k_ref, v_ref, a_ref, lw_ref, kk_ref, ka_ref, rk_ref, gw_ref, gb_ref,
                     o_ref, s_ref, bd_ref):
    cg = pl.program_id(2)
    ncg = pl.num_programs(2)
    cl = RW_CHUNK
    two = 2 * cl

    @pl.when(cg == 0)
    def _():
        bd_ref[...] = jnp.zeros_like(bd_ref)

    lane = lax.broadcasted_iota(jnp.int32, (cl, LANES), 1)
    m0 = lane < RW_HEAD
    ti = lax.broadcasted_iota(jnp.int32, (cl, cl), 0)
    tj = lax.broadcasted_iota(jnp.int32, (cl, cl), 1)
    tri = jnp.where(ti >= tj, 1.0, 0.0).astype(BF16)
    row2 = lax.broadcasted_iota(jnp.int32, (two, LANES), 0)
    col2 = lax.broadcasted_iota(jnp.int32, (two, LANES), 1)
    same = (row2 >= cl) == (col2 >= RW_HEAD)
    step_r = jnp.bitwise_and(row2, cl - 1)
    step_c = jnp.bitwise_and(col2, cl - 1)
    bd_strict = jnp.logical_and(same, step_r > step_c)
    bd_incl = jnp.logical_and(same, step_r >= step_c)
    eye = row2 == col2

    def stack(x):
        return jnp.where(same, jnp.concatenate([x, x], axis=0), 0.0)

    chunks = range(gsz)
    st = []
    for pi in range(npair):
        cols = slice(pi * LANES, (pi + 1) * LANES)
        kk_w, ka_w, rk_w = kk_ref[:, cols], ka_ref[:, cols], rk_ref[:, cols]
        for gi in chunks:
            sl = slice(gi * cl, (gi + 1) * cl)
            r = r_ref[0, sl, cols]
            k = k_ref[0, sl, cols]
            v = v_ref[0, sl, cols]
            a = a_ref[0, sl, cols]
            lw = lw_ref[0, sl, cols]
            kkv = k * kk_w
            kap = kkv * lax.rsqrt(jnp.maximum(_pair_sum(kkv * kkv, m0), 1e-24))
            kmod = k * (1.0 + (a - 1.0) * ka_w)
            st.append(dict(r=r, v=v, lw=lw, kap=kap, kmod=kmod, bvec=kap * a,
                           bonus=_pair_sum(r * kmod * rk_w, m0) * v))
    for s in st:
        s['cum'] = _cumsum_rows(tri, s['lw'])
    for s in st:
        cum = s['cum']
        cum_last = cum[cl - 1:cl, :]
        g_inv = jnp.exp(-cum)
        g_end = jnp.exp(cum_last - cum)
        s['g_all'] = jnp.exp(cum_last)
        s['kapm'] = stack(s['kap'] * jnp.exp(cum - s['lw']))
        s['rm'] = stack(s['r'] * jnp.exp(cum))
        s['v2m'] = stack(s['v'])
        s['b_h'] = s['bvec'] * g_inv
        s['k_h'] = s['kmod'] * g_inv
        s['b_e'] = stack(s['bvec'] * g_end)
        s['k_e'] = stack(s['kmod'] * g_end)
    for s in st:
        lhs = jnp.concatenate([s['kapm'], s['rm']], axis=0)
        gb = _dot_nt(lhs, jnp.concatenate([s['b_h'], s['b_h']], axis=0))
        gk = _dot_nt(lhs, jnp.concatenate([s['k_h'], s['k_h']], axis=0))
        s['a_ab'] = jnp.where(bd_strict, gb[:two], 0.0)
        s['a_rb'] = jnp.where(bd_incl, gb[two:], 0.0)
        s['a_ak'] = jnp.where(bd_strict, gk[:two], 0.0)
        s['a_rk'] = jnp.where(bd_incl, gk[two:], 0.0)
    for s in st:
        s['tinv'] = jnp.where(eye, 1.0, 0.0) - s['a_ab']
        s['lp'] = _dot(s['a_ab'], s['a_ab'])
        s['x2m'] = _dot(s['a_ak'], s['v2m'])
    for _ in range(4):
        for s in st:
            both = _dot(jnp.concatenate([s['tinv'], s['lp']], axis=0), s['lp'])
            s['tinv'] = s['tinv'] + both[:two]
            s['lp'] = both[two:]
    for s in st:
        s['tinv'] = s['tinv'] + _dot(s['tinv'], s['lp'])
    for s in st:
        pq = _dot(s['tinv'], jnp.concatenate([s['kapm'], s['x2m']], axis=1))
        s['pm'] = -pq[:, :LANES]
        s['qm'] = -pq[:, LANES:]
    for s in st:
        ro = _dot(s['a_rb'], jnp.concatenate([s['pm'], s['qm']], axis=1))
        rtm = s['rm'] + ro[:, :LANES]
        oim = ro[:, LANES:] + _dot(s['a_rk'], s['v2m'])
        s['rt'] = rtm[:cl] + rtm[cl:]
        s['oi'] = oim[:cl] + oim[cl:]
    for s in st:
        s['corr'] = _dot_tn(s['pm'], s['b_e'])
        s['nmat'] = _dot_tn(jnp.concatenate([s['qm'], s['v2m']], axis=0),
                            jnp.concatenate([s['b_e'], s['k_e']], axis=0))
    bds = [bd_ref[pi] for pi in range(npair)]
    for gi in chunks:
        for pi in range(npair):
            s = st[pi * gsz + gi]
            cols = slice(pi * LANES, (pi + 1) * LANES)
            o = _dot_nt(s['rt'], bds[pi]) + s['oi']
            bds[pi] = bds[pi] * s['g_all'] + _dot(bds[pi], s['corr']) + s['nmat']
            mean = _pair_sum(o, m0) * (1.0 / RW_HEAD)
            oc = o - mean
            var = _pair_sum(oc * oc, m0) * (1.0 / RW_HEAD)
            o_ref[0, gi * cl:(gi + 1) * cl, cols] = (
                oc * lax.rsqrt(var + RW_GN_EPS) * gw_ref[:, cols] + gb_ref[:, cols] + s['bonus'])
    for pi in range(npair):
        bd_ref[pi] = bds[pi]

    @pl.when(cg == ncg - 1)
    def _():
        for pi in range(npair):
            s_ref[0, 2 * pi] = bds[pi][:RW_HEAD, :RW_HEAD]
            s_ref[0, 2 * pi + 1] = bds[pi][RW_HEAD:, RW_HEAD:]


def _rwkv_chunk(r, k, v, a, lw, b, t, p):
    npair = RW_PAIRS
    hp = RW_HEADS // (2 * npair)
    gsz = RW_GROUP if t % (RW_GROUP * RW_CHUNK) == 0 else 1
    rows = gsz * RW_CHUNK
    shp = lambda x: x.reshape(b, t, D_MODEL)
    blk = pl.BlockSpec((1, rows, npair * LANES), lambda bi, hi, ci: (bi, ci, hi))
    vec = pl.BlockSpec((1, npair * LANES), lambda bi, hi, ci: (0, hi))
    vrow = lambda x: x.reshape(1, D_MODEL)
    o, s = pl.pallas_call(
        functools.partial(_rwkv_chunk_body, gsz, npair),
        grid=(b, hp, t // rows),
        in_specs=[blk] * 5 + [vec] * 5,
        out_specs=[blk, pl.BlockSpec((1, 2 * npair, RW_HEAD, RW_HEAD), lambda bi, hi, ci: (bi, hi, 0, 0))],
        out_shape=[jax.ShapeDtypeStruct((b, t, D_MODEL), F32),
                   jax.ShapeDtypeStruct((b, RW_HEADS, RW_HEAD, RW_HEAD), F32)],
        scratch_shapes=[pltpu.VMEM((npair, LANES, LANES), F32)],
        compiler_params=_cparams("parallel", "parallel", "arbitrary"))(
            shp(r), shp(k), shp(v), shp(a), shp(lw), vrow(p['rw_k_k']), vrow(p['rw_k_a']),
            vrow(p['rw_r_k']), vrow(p['rw_gn_w']), vrow(p['rw_gn_b']))
    return o.reshape(b * t, D_MODEL), s


def _rwkv_step_body(r_ref, k_ref, vc_ref, a_ref, lw_ref, s_ref, kk_ref, ka_ref, rk_ref, gw_ref, gb_ref,
                    o_ref, so_ref):
    r = r_ref[0]
    k = k_ref[0]
    a = a_ref[0]
    lw = lw_ref[0]
    vc = vc_ref[0]
    s = s_ref[0]
    kkv = k * kk_ref[...]
    kap = kkv * lax.rsqrt(jnp.maximum(jnp.sum(kkv * kkv, axis=-1, keepdims=True), 1e-24))
    kmod = k * (1.0 + (a - 1.0) * ka_ref[...])
    bvec = kap * a
    dec = jnp.exp(lw)
    sa = jnp.sum(s * kap, axis=-1, keepdims=True)
    s_new = s * dec - sa * bvec + vc * kmod
    so_ref[0] = s_new
    o = jnp.sum(s_new * r, axis=-1, keepdims=True)
    mean = jnp.mean(o, axis=1, keepdims=True)
    oc = o - mean
    var = jnp.mean(oc * oc, axis=1, keepdims=True)
    on = oc * lax.rsqrt(var + RW_GN_EPS) * gw_ref[...] + gb_ref[...]
    bonus = jnp.sum(r * kmod * rk_ref[...], axis=-1, keepdims=True)
    o_ref[0] = on + bonus * vc


def _rwkv_step(r, k, v, a, lw, state, p):
    b = r.shape[0]
    hrow = (RW_HEADS, 1, RW_HEAD)
    hcol = (RW_HEADS, RW_HEAD, 1)
    rows = lambda x: x.reshape((b,) + hrow)
    rblk = pl.BlockSpec((1,) + hrow, lambda i: (i, 0, 0, 0))
    cblk = pl.BlockSpec((1,) + hcol, lambda i: (i, 0, 0, 0))
    sblk = pl.BlockSpec((1, RW_HEADS, RW_HEAD, RW_HEAD), lambda i: (i, 0, 0, 0))
    o, s = pl.pallas_call(
        _rwkv_step_body,
        grid=(b,),
        in_specs=[rblk, rblk, cblk, rblk, rblk, sblk] + [_full(hrow)] * 3 + [_full(hcol)] * 2,
        out_specs=[cblk, sblk],
        out_shape=[jax.ShapeDtypeStruct((b,) + hcol, F32),
                   jax.ShapeDtypeStruct((b, RW_HEADS, RW_HEAD, RW_HEAD), F32)],
        compiler_params=_cparams("parallel"))(
            rows(r), rows(k), v.reshape((b,) + hcol), rows(a), rows(lw), state,
            p['rw_k_k'].reshape(hrow), p['rw_k_a'].reshape(hrow), p['rw_r_k'].reshape(hrow),
            p['rw_gn_w'].reshape(hcol), p['rw_gn_b'].reshape(hcol))
    return o.reshape(b, D_MODEL), s


def _mamba_weights(p):
    w = p['mb_w_in']
    wz = w[:, :MB_INNER]
    wx = w[:, MB_INNER:MB_INNER + MB_CONV_DIM]
    wdt = jnp.pad(w[:, MB_INNER + MB_CONV_DIM:], ((0, 0), (0, LANES - MB_HEADS)))
    return [wz, wx, wdt]


def _mamba_vecs(p):
    pad = lambda x: jnp.pad(x.reshape(1, MB_HEADS), ((0, 0), (0, LANES - MB_HEADS)))
    return dict(conv_w=p['mb_conv_w'], conv_b=p['mb_conv_b'].reshape(1, MB_CONV_DIM),
                dt_bias=pad(p['mb_dt_bias']), a_log=pad(p['mb_a_log']),
                d_skip=jnp.repeat(p['mb_d'], MB_HEADDIM).reshape(1, MB_INNER),
                norm_w=p['mb_norm_w'].reshape(1, MB_INNER))


def _mamba_chunk_body(x_ref, xp_ref, z_ref, dt_ref, cw_ref, cb_ref, dtb_ref, alog_ref, dsk_ref, nw_ref,
                      y_ref, fs_ref, st_ref):
    c = pl.program_id(1)
    nc = pl.num_programs(1)
    cl = MB_CHUNK

    @pl.when(c == 0)
    def _():
        st_ref[...] = jnp.zeros_like(st_ref)

    hist = jnp.where(c == 0, 0.0, xp_ref[0])
    xcat = jnp.concatenate([hist, x_ref[0]], axis=0)
    cw = cw_ref[...]
    conv = cb_ref[...] + cw[0:1, :] * xcat[5:5 + cl]
    for j in range(1, MB_CONV):
        conv = conv + cw[j:j + 1, :] * xcat[5 + j:5 + j + cl]
    conv = _silu(conv)

    dt = _softplus(dt_ref[0] + dtb_ref[...])
    da = dt * (-jnp.exp(alog_ref[...]))
    ti = lax.broadcasted_iota(jnp.int32, (cl, cl), 0)
    tj = lax.broadcasted_iota(jnp.int32, (cl, cl), 1)
    causal = ti >= tj
    tri = jnp.where(causal, 1.0, 0.0).astype(BF16)
    a_cs = _cumsum_rows(tri, da)
    a_cs_t = a_cs.T
    a_last = a_cs[cl - 1:cl, :]
    dec_to_end = jnp.exp(a_last - a_cs)
    e_cs = jnp.exp(a_cs)
    e_last = jnp.exp(a_last)

    lane = lax.broadcasted_iota(jnp.int32, (cl, LANES), 1)
    m0 = lane < MB_HEADDIM
    heads_per_group = MB_HEADS // MB_GROUPS
    pairs_per_group = heads_per_group // 2
    for g in range(MB_GROUPS):
        bm = conv[:, MB_INNER + g * MB_STATE:MB_INNER + (g + 1) * MB_STATE]
        cm = conv[:, MB_INNER + (MB_GROUPS + g) * MB_STATE:MB_INNER + (MB_GROUPS + g + 1) * MB_STATE]
        cb = _dot_nt(cm, bm)
        bm_t = bm.T.astype(BF16)
        blocks = []
        for j in range(pairs_per_group):
            pi = g * pairs_per_group + j
            h0, h1 = 2 * pi, 2 * pi + 1
            lo, hi = pi * LANES, (pi + 1) * LANES
            xs = conv[:, lo:hi]
            dt2 = jnp.where(m0, dt[:, h0:h0 + 1], dt[:, h1:h1 + 1])
            xdt = xs * dt2
            l0 = jnp.exp(jnp.where(causal, a_cs[:, h0:h0 + 1] - a_cs_t[h0:h0 + 1, :], NEG_INF))
            l1 = jnp.exp(jnp.where(causal, a_cs[:, h1:h1 + 1] - a_cs_t[h1:h1 + 1, :], NEG_INF))
            lhs = jnp.concatenate([(cb * l0).astype(BF16), (cb * l1).astype(BF16)], axis=1)
            rhs = jnp.concatenate([jnp.where(m0, xdt, 0.0).astype(BF16),
                                   jnp.where(m0, 0.0, xdt).astype(BF16)], axis=0)
            y = jnp.dot(lhs, rhs, preferred_element_type=F32)
            st = st_ref[pi]
            e2 = jnp.where(m0, e_cs[:, h0:h0 + 1], e_cs[:, h1:h1 + 1])
            y = y + _dot(cm, st) * e2
            d2 = jnp.where(m0, dec_to_end[:, h0:h0 + 1], dec_to_end[:, h1:h1 + 1])
            sc = jnp.where(m0, e_last[:, h0:h0 + 1], e_last[:, h1:h1 + 1])
            st_ref[pi] = st * sc + jnp.dot(bm_t, (xdt * d2).astype(BF16), preferred_element_type=F32)
            y = y + dsk_ref[:, lo:hi] * xs
            blocks.append(y * _silu(z_ref[0, :, lo:hi]))
        ssq = blocks[0] * blocks[0]
        for blk in blocks[1:]:
            ssq = ssq + blk * blk
        scale = lax.rsqrt(jnp.sum(ssq, axis=-1, keepdims=True) * (1.0 / MB_GROUP_WIDTH) + MB_NORM_EPS)
        for j, blk in enumerate(blocks):
            lo = (g * pairs_per_group + j) * LANES
            y_ref[0, :, lo:lo + LANES] = blk * scale * nw_ref[:, lo:lo + LANES]

    @pl.when(c == nc - 1)
    def _():
        for pi in range(MB_HEADS // 2):
            fs_ref[0, pi] = st_ref[pi].T


def _mamba_chunk(z, xbc, dtr, b, t, vecs):
    nc = t // MB_CHUNK
    npairs = MB_HEADS // 2
    per = MB_CHUNK // SUBLANES
    y, fs = pl.pallas_call(
        _mamba_chunk_body,
        grid=(b, nc),
        in_specs=[pl.BlockSpec((1, MB_CHUNK, MB_CONV_DIM), lambda bi, ci: (bi, ci, 0)),
                  pl.BlockSpec((1, SUBLANES, MB_CONV_DIM),
                               lambda bi, ci: (bi, jnp.maximum(ci * per - 1, 0), 0)),
                  pl.BlockSpec((1, MB_CHUNK, MB_INNER), lambda bi, ci: (bi, ci, 0)),
                  pl.BlockSpec((1, MB_CHUNK, LANES), lambda bi, ci: (bi, ci, 0)),
                  _full((MB_CONV, MB_CONV_DIM)), _full((1, MB_CONV_DIM)), _full((1, LANES)),
                  _full((1, LANES)), _full((1, MB_INNER)), _full((1, MB_INNER))],
        out_specs=[pl.BlockSpec((1, MB_CHUNK, MB_INNER), lambda bi, ci: (bi, ci, 0)),
                   pl.BlockSpec((1, npairs, LANES, MB_STATE), lambda bi, ci: (bi, 0, 0, 0))],
        out_shape=[jax.ShapeDtypeStruct((b, t, MB_INNER), F32),
                   jax.ShapeDtypeStruct((b, npairs, LANES, MB_STATE), F32)],
        scratch_shapes=[pltpu.VMEM((npairs, LANES, MB_STATE), F32)],
        compiler_params=_cparams("parallel", "arbitrary"))(
            xbc.reshape(b, t, MB_CONV_DIM), xbc.reshape(b, t, MB_CONV_DIM), z.reshape(b, t, MB_INNER),
            dtr.reshape(b, t, LANES), vecs['conv_w'], vecs['conv_b'], vecs['dt_bias'], vecs['a_log'],
            vecs['d_skip'], vecs['norm_w'])
    return y.reshape(b * t, MB_INNER), fs.reshape(b, MB_HEADS, MB_HEADDIM, MB_STATE)


def _mamba_step_conv_body(x_ref, cs_ref, dt_ref, cw_ref, cb_ref, dtb_ref, alog_ref, conv_ref, dt_o, ed_o):
    cw = cw_ref[...]
    conv = cb_ref[...] + cw[MB_CONV - 1:MB_CONV, :] * x_ref[...]
    for j in range(MB_CONV - 1):
        conv = conv + cw[j:j + 1, :] * cs_ref[:, j, :]
    conv_ref[...] = _silu(conv)
    dt = _softplus(dt_ref[...] + dtb_ref[...])
    dt_o[...] = dt
    ed_o[...] = jnp.exp(dt * (-jnp.exp(alog_ref[...])))


def _mamba_step_state_body(s_ref, xc_ref, dt_ref, ed_ref, bm_ref, cm_ref, so_ref, y_ref):
    hpg = MB_HEADS // MB_GROUPS
    for g in range(MB_GROUPS):
        hs = slice(g * hpg, (g + 1) * hpg)
        s = s_ref[0, hs]
        xdt = xc_ref[0, hs] * dt_ref[0, hs]
        fin = s * ed_ref[0, hs] + xdt * bm_ref[0, g:g + 1]
        so_ref[0, hs] = fin
        y_ref[0, hs] = jnp.sum(fin * cm_ref[0, g:g + 1], axis=-1, keepdims=True)


def _mamba_step_out_body(y_ref, x_ref, z_ref, dsk_ref, nw_ref, o_ref):
    y = (y_ref[...] + dsk_ref[...] * x_ref[...]) * _silu(z_ref[...])
    for g in range(MB_GROUPS):
        lo, hi = g * MB_GROUP_WIDTH, (g + 1) * MB_GROUP_WIDTH
        o_ref[:, lo:hi] = _rms(y[:, lo:hi], nw_ref[:, lo:hi], MB_NORM_EPS)


def _mamba_step(z, xbc, dtr, conv_state, ssm_state, vecs):
    b = z.shape[0]
    conv, dt, ed = pl.pallas_call(
        _mamba_step_conv_body,
        grid=(1,),
        in_specs=[_full((b, MB_CONV_DIM)), _full((b, MB_CONV - 1, MB_CONV_DIM)), _full((b, LANES)),
                  _full((MB_CONV, MB_CONV_DIM)), _full((1, MB_CONV_DIM)), _full((1, LANES)), _full((1, LANES))],
        out_specs=[_full((b, MB_CONV_DIM)), _full((b, LANES)), _full((b, LANES))],
        out_shape=[jax.ShapeDtypeStruct((b, MB_CONV_DIM), F32), jax.ShapeDtypeStruct((b, LANES), F32),
                   jax.ShapeDtypeStruct((b, LANES), F32)],
        compiler_params=_cparams("arbitrary"))(
            xbc, conv_state, dtr, vecs['conv_w'], vecs['conv_b'], vecs['dt_bias'], vecs['a_log'])
    xs = conv[:, :MB_INNER]
    bm = conv[:, MB_INNER:MB_INNER + MB_GROUPS * MB_STATE].reshape(b, MB_GROUPS, MB_STATE)
    cm = conv[:, MB_INNER + MB_GROUPS * MB_STATE:].reshape(b, MB_GROUPS, MB_STATE)
    hp1 = (MB_HEADS, MB_HEADDIM, 1)
    h11 = (MB_HEADS, 1, 1)
    per = lambda shape: pl.BlockSpec((1,) + shape, lambda i: (i,) + (0,) * len(shape))
    st_shape = (MB_HEADS, MB_HEADDIM, MB_STATE)
    fin, y = pl.pallas_call(
        _mamba_step_state_body,
        grid=(b,),
        in_specs=[per(st_shape), per(hp1), per(h11), per(h11), per((MB_GROUPS, MB_STATE)),
                  per((MB_GROUPS, MB_STATE))],
        out_specs=[per(st_shape), per(hp1)],
        out_shape=[jax.ShapeDtypeStruct((b,) + st_shape, F32), jax.ShapeDtypeStruct((b,) + hp1, F32)],
        compiler_params=_cparams("parallel"))(
            ssm_state, xs.reshape((b,) + hp1), dt[:, :MB_HEADS].reshape((b,) + h11),
            ed[:, :MB_HEADS].reshape((b,) + h11), bm, cm)
    yg = pl.pallas_call(
        _mamba_step_out_body,
        grid=(1,),
        in_specs=[_full((b, MB_INNER))] * 3 + [_full((1, MB_INNER))] * 2,
        out_specs=_full((b, MB_INNER)),
        out_shape=jax.ShapeDtypeStruct((b, MB_INNER), F32),
        compiler_params=_cparams("arbitrary"))(
            y.reshape(b, MB_INNER), xs, z, vecs['d_skip'], vecs['norm_w'])
    return yg, fin


def _rope_tables(pos):
    half = ROPE_DIM // 2
    inv_freq = ROPE_THETA ** (-jnp.arange(0, ROPE_DIM, 2, dtype=F32) / ROPE_DIM)
    ang = pos.astype(F32)[:, None] * inv_freq[None, :]
    cos, sin = jnp.cos(ang), jnp.sin(ang)
    n = pos.shape[0]
    ones = jnp.ones((n, DA_HEAD - ROPE_DIM), F32)
    zeros = jnp.zeros((n, DA_HEAD - ROPE_DIM), F32)
    zh = jnp.zeros((n, half), F32)
    ct = jnp.concatenate([cos, cos, ones], axis=1)
    s_up = jnp.concatenate([-sin, zh, zeros], axis=1)
    s_dn = jnp.concatenate([zh, sin, zeros], axis=1)
    two = lambda x: jnp.concatenate([x, x], axis=1)
    return two(ct), two(s_up), two(s_dn)


def _rope(x, ct, s_up, s_dn):
    half = ROPE_DIM // 2
    cols = []
    for j in range(x.shape[1] // LANES):
        xj = x[:, j * LANES:(j + 1) * LANES]
        cols.append(xj * ct + pltpu.roll(xj, LANES - half, axis=1) * s_up + pltpu.roll(xj, half, axis=1) * s_dn)
    return jnp.concatenate(cols, axis=1)


def _da_epilogue(ys, rows, fulls):
    y = ys[0]
    ct, s_up, s_dn = rows
    q = _rope(y[:, :DA_WIDTH], ct, s_up, s_dn)
    k = _rope(y[:, DA_WIDTH:2 * DA_WIDTH], ct, s_up, s_dn)
    return q, k, y[:, 2 * DA_WIDTH:3 * DA_WIDTH], y[:, 3 * DA_WIDTH:]


def _lambda(lq1, lk1, lq2, lk2, lam_init):
    return (jnp.exp(jnp.sum(lq1 * lk1, axis=-1, keepdims=True))
            - jnp.exp(jnp.sum(lq2 * lk2, axis=-1, keepdims=True)) + lam_init)


def _da_epilogue_seq(ys, rows, fulls):
    q, k, v, g = _da_epilogue(ys, rows, fulls)
    return k, v, g, (q * (DA_SCALE * LOG2_E)).T, k, v.T


def _flash_body(lam_init, nh, qi_ref, kj_ref, qt_ref, k_ref, vt_ref, lq1, lk1, lq2, lk2, sw_ref, o_ref,
                m_s, l_s, acc):
    step = pl.program_id(2)
    qi = qi_ref[step]
    kj = kj_ref[step]

    @pl.when(kj == 0)
    def _():
        m_s[...] = jnp.full_like(m_s, NEG_INF)
        l_s[...] = jnp.zeros_like(l_s)
        acc[...] = jnp.zeros_like(acc)

    def update(masked):
        tk, tq = k_ref.shape[0], qt_ref.shape[1]
        frow = lax.broadcasted_iota(jnp.int32, (DA_VDIM, tq), 0)
        zero = jnp.zeros((DA_VDIM, tq), BF16)
        if masked:
            keep = (lax.broadcasted_iota(jnp.int32, (tk, tq), 0)
                    <= lax.broadcasted_iota(jnp.int32, (tk, tq), 1))
        scores, vts = [], []
        for hd in range(nh):
            sl = slice(hd * DA_VDIM, (hd + 1) * DA_VDIM)
            qt = qt_ref[sl, :]
            k = k_ref[:, sl]
            scores.append(jnp.dot(k, jnp.where(frow < DA_HEAD, qt, zero), preferred_element_type=F32))
            scores.append(jnp.dot(k, jnp.where(frow < DA_HEAD, zero, qt), preferred_element_type=F32))
            vts.append(vt_ref[sl, :])
        for i, s in enumerate(scores):
            if masked:
                s = jnp.where(keep, s, NEG_INF)
            m_old = m_s[i]
            m_new = jnp.maximum(m_old, jnp.max(s, axis=0, keepdims=True))
            alpha = jnp.exp2(m_old - m_new)
            pr = jnp.exp2(s - m_new)
            l_s[i] = l_s[i] * alpha + jnp.sum(pr, axis=0, keepdims=True)
            acc[i] = acc[i] * alpha + jnp.dot(vts[i // 2], pr.astype(BF16), preferred_element_type=F32)
            m_s[i] = m_new

    @pl.when(kj < qi)
    def _():
        update(False)

    @pl.when(kj == qi)
    def _():
        update(True)
        lam = _lambda(lq1[...], lk1[...], lq2[...], lk2[...], lam_init)
        for hd in range(nh):
            i = 2 * hd
            ot = acc[i] / l_s[i] - lam * (acc[i + 1] / l_s[i + 1])
            scale = lax.rsqrt(jnp.mean(ot * ot, axis=0, keepdims=True) + DA_SUBLN_EPS)
            o_ref[:, hd * DA_VDIM:(hd + 1) * DA_VDIM] = (ot * scale * (sw_ref[...] * (1.0 - lam_init))).T


def _flash(qt, kb, vt, b, t, p, lam_init):
    tq = min(t, 512)
    nq = t // tq
    pairs = [(qi, kj) for qi in range(nq) for kj in range(qi + 1)]
    qi_tab = jnp.asarray([pr[0] for pr in pairs], jnp.int32)
    kj_tab = jnp.asarray([pr[1] for pr in pairs], jnp.int32)
    nh = DA_HEADS_PER_STEP
    wd = nh * DA_VDIM
    qblk = pl.BlockSpec((wd, tq), lambda bi, hi, s, qt_, kt_: (hi, bi * nq + qt_[s]))
    vblk = pl.BlockSpec((wd, tq), lambda bi, hi, s, qt_, kt_: (hi, bi * nq + kt_[s]))
    kblk = pl.BlockSpec((tq, wd), lambda bi, hi, s, qt_, kt_: (bi * nq + kt_[s], hi))
    oblk = pl.BlockSpec((tq, wd), lambda bi, hi, s, qt_, kt_: (bi * nq + qt_[s], hi))
    small = lambda shape: pl.BlockSpec(shape, lambda bi, hi, s, qt_, kt_: (0, 0))
    vec = lambda x: x.reshape(1, -1)
    stat = pltpu.VMEM((2 * nh, 1, tq), F32)
    acc = pltpu.VMEM((2 * nh, DA_VDIM, tq), F32)
    return pl.pallas_call(
        functools.partial(_flash_body, lam_init, nh),
        grid_spec=pltpu.PrefetchScalarGridSpec(
            num_scalar_prefetch=2, grid=(b, DA_HEADS // nh, len(pairs)),
            in_specs=[qblk, kblk, vblk] + [small((1, DA_HEAD))] * 4 + [small((DA_VDIM, 1))],
            out_specs=oblk,
            scratch_shapes=[stat, stat, acc]),
        out_shape=jax.ShapeDtypeStruct((b * t, DA_WIDTH), F32),
        compiler_params=_cparams("parallel", "parallel", "arbitrary"))(
            qi_tab, kj_tab, qt, kb, vt, vec(p['da_lq1']), vec(p['da_lk1']), vec(p['da_lq2']),
            vec(p['da_lk2']), p['da_subln_w'].reshape(DA_VDIM, 1))


def _decode_body(lam_init, npg, pt_ref, q_ref, kn_ref, vn_ref, *refs):
    ck_refs = refs[:npg]
    cv_refs = refs[npg:2 * npg]
    lq1, lk1, lq2, lk2, sw_ref, o_ref, qm_s, m_s, l_s, acc = refs[2 * npg:]
    j = pl.program_id(1)
    nj = pl.num_programs(1)
    nrow = 2 * DA_HEADS
    ncol = PAGE_SIZE * DA_HEADS
    rid = lax.broadcasted_iota(jnp.int32, (nrow, LANES), 0)

    @pl.when(j == 0)
    def _():
        comp = lax.broadcasted_iota(jnp.int32, (nrow, LANES), 1) // DA_HEAD
        qm = jnp.where(jnp.bitwise_and(rid, 1) == comp, q_ref[0] * DA_SCALE, 0.0)
        qm_s[...] = qm
        m_s[...] = jnp.sum(qm * kn_ref[0], axis=-1, keepdims=True)
        l_s[...] = jnp.ones_like(l_s)
        acc[...] = vn_ref[0]

    qm = qm_s[...]
    keep = (lax.rem(lax.broadcasted_iota(jnp.int32, (nrow, ncol), 1), DA_HEADS)
            == lax.broadcasted_iota(jnp.int32, (nrow, ncol), 0) // 2)
    scores = [jnp.where(keep, _dot_nt(qm, ck[0].reshape(ncol, DA_VDIM)), NEG_INF) for ck in ck_refs]
    m_old = m_s[...]
    m_new = m_old
    for s in scores:
        m_new = jnp.maximum(m_new, jnp.max(s, axis=-1, keepdims=True))
    alpha = jnp.exp(m_old - m_new)
    l_new = l_s[...] * alpha
    a_new = acc[...] * alpha
    for s, cv in zip(scores, cv_refs):
        pr = jnp.exp(s - m_new)
        l_new = l_new + jnp.sum(pr, axis=-1, keepdims=True)
        a_new = a_new + _dot(pr, cv[0].reshape(ncol, DA_VDIM))
    m_s[...] = m_new
    l_s[...] = l_new
    acc[...] = a_new

    @pl.when(j == nj - 1)
    def _():
        lam = _lambda(lq1[...], lk1[...], lq2[...], lk2[...], lam_init)
        d = (a_new / l_new) * jnp.where(jnp.bitwise_and(rid, 1) == 0, 1.0, -lam)
        pair = d + pltpu.roll(d, nrow - 1, axis=0)
        o_ref[0] = _rms(pair, sw_ref[...], DA_SUBLN_EPS) * (1.0 - lam_init)


def _decode(q, k_new, v_new, cache_k, cache_v, page_table, p, lam_init):
    b = q.shape[0]
    n_pages = page_table.shape[1]
    npg = DEC_PAGES if n_pages % DEC_PAGES == 0 else 1
    nrow = 2 * DA_HEADS
    rows16 = lambda x: jnp.repeat(x.reshape(b, DA_HEADS, DA_VDIM), 2, axis=1)
    row = pl.BlockSpec((1, nrow, DA_VDIM), lambda bi, j, pt: (bi, 0, 0))
    pages = [pl.BlockSpec((1, PAGE_SIZE, DA_HEADS, DA_VDIM),
                          lambda bi, j, pt, i=i: (pt[bi, j * npg + i], 0, 0, 0)) for i in range(npg)]
    vec = lambda x: x.reshape(1, -1)
    small = lambda n: pl.BlockSpec((1, n), lambda bi, j, pt: (0, 0))
    o = pl.pallas_call(
        functools.partial(_decode_body, lam_init, npg),
        grid_spec=pltpu.PrefetchScalarGridSpec(
            num_scalar_prefetch=1, grid=(b, n_pages // npg),
            in_specs=[row, row, row] + pages + pages + [small(DA_HEAD)] * 4 + [small(DA_VDIM)],
            out_specs=row,
            scratch_shapes=[pltpu.VMEM((nrow, DA_VDIM), F32), pltpu.VMEM((nrow, 1), F32),
                            pltpu.VMEM((nrow, 1), F32), pltpu.VMEM((nrow, DA_VDIM), F32)]),
        out_shape=jax.ShapeDtypeStruct((b, nrow, DA_VDIM), F32),
        compiler_params=_cparams("parallel", "arbitrary"))(
            page_table, rows16(q), rows16(k_new), rows16(v_new), *([cache_k] * npg), *([cache_v] * npg),
            vec(p['da_lq1']), vec(p['da_lk1']), vec(p['da_lq2']), vec(p['da_lk2']), vec(p['da_subln_w']))
    return o[:, ::2, :].reshape(b, DA_WIDTH)


def _cf_epilogue(ys, rows, fulls):
    y = ys[0]
    u = y[:, :CF_WIDTH] * _sigmoid(y[:, CF_WIDTH:2 * CF_WIDTH])
    return u, y[:, 2 * CF_WIDTH:]


def _cf_conv_body(tc, u_ref, up_ref, cw_ref, cb_ref, lw_ref, lb_ref, o_ref, buf, sh):
    c = pl.program_id(1)
    buf[0:CF_HIST, :] = jnp.where(c == 0, 0.0, up_ref[0])
    buf[CF_HIST:CF_HIST + tc, :] = u_ref[0]
    off = CF_HIST - (CF_KERNEL - 1)
    span = tc + CF_HIST - SUBLANES
    for r in range(1, SUBLANES):
        sh[r - 1, 0:span, :] = buf[r:r + span, :]
    rb = CF_ROWS
    for i in range(tc // rb):
        acc = cb_ref[...]
        for j in range(CF_KERNEL):
            r, base = (off + j) % SUBLANES, (off + j) // SUBLANES * SUBLANES
            rows = slice(base + i * rb, base + (i + 1) * rb)
            acc = acc + cw_ref[j:j + 1, :] * (buf[rows, :] if r == 0 else sh[r - 1, rows, :])
        mu = jnp.mean(acc, axis=-1, keepdims=True)
        xc = acc - mu
        var = jnp.mean(xc * xc, axis=-1, keepdims=True)
        o_ref[0, i * rb:(i + 1) * rb, :] = _silu(xc * lax.rsqrt(var + CF_LN_EPS) * lw_ref[...] + lb_ref[...])


def _cf_conv(u, b, t, p):
    tc = min(t, 256)
    per = tc // CF_HIST
    blk = pl.BlockSpec((1, tc, CF_WIDTH), lambda bi, ci: (bi, ci, 0))
    prev = pl.BlockSpec((1, CF_HIST, CF_WIDTH), lambda bi, ci: (bi, jnp.maximum(ci * per - 1, 0), 0))
    vec = lambda x: x.reshape(1, CF_WIDTH)
    u3 = u.reshape(b, t, CF_WIDTH)
    o = pl.pallas_call(
        functools.partial(_cf_conv_body, tc),
        grid=(b, t // tc),
        in_specs=[blk, prev, _full((CF_KERNEL, CF_WIDTH))] + [_full((1, CF_WIDTH))] * 3,
        out_specs=blk,
        out_shape=jax.ShapeDtypeStruct((b, t, CF_WIDTH), F32),
        scratch_shapes=[pltpu.VMEM((CF_HIST + tc, CF_WIDTH), F32),
                        pltpu.VMEM((SUBLANES - 1, CF_HIST + tc - SUBLANES, CF_WIDTH), F32)],
        compiler_params=_cparams("parallel", "parallel"))(
            u3, u3, p['cf_conv_w'], vec(p['cf_conv_b']), vec(p['cf_ln_w']), vec(p['cf_ln_b']))
    return o.reshape(b * t, CF_WIDTH)


def _cf_step_body(u_ref, st_ref, cw_ref, cb_ref, lw_ref, lb_ref, o_ref):
    nb = u_ref.shape[0]
    cw = cw_ref[...]
    hist = jnp.concatenate([jnp.sum(st_ref[i] * cw[:CF_KERNEL - 1, :], axis=0, keepdims=True)
                            for i in range(nb)], axis=0)
    acc = cb_ref[...] + hist + cw[CF_KERNEL - 1:CF_KERNEL, :] * u_ref[...]
    mu = jnp.mean(acc, axis=-1, keepdims=True)
    xc = acc - mu
    var = jnp.mean(xc * xc, axis=-1, keepdims=True)
    o_ref[...] = _silu(xc * lax.rsqrt(var + CF_LN_EPS) * lw_ref[...] + lb_ref[...])


def _cf_step(u, state, p):
    b = u.shape[0]
    tb = SUBLANES
    vec = lambda x: x.reshape(1, CF_WIDTH)
    return pl.pallas_call(
        _cf_step_body,
        grid=(b // tb,),
        in_specs=[pl.BlockSpec((tb, CF_WIDTH), lambda i: (i, 0)),
                  pl.BlockSpec((tb, CF_KERNEL - 1, CF_WIDTH), lambda i: (i, 0, 0)),
                  _full((CF_KERNEL, CF_WIDTH))] + [_full((1, CF_WIDTH))] * 3,
        out_specs=pl.BlockSpec((tb, CF_WIDTH), lambda i: (i, 0)),
        out_shape=jax.ShapeDtypeStruct((b, CF_WIDTH), F32),
        compiler_params=_cparams("parallel"))(
            u, state, p['cf_conv_w'], vec(p['cf_conv_b']), vec(p['cf_ln_w']), vec(p['cf_ln_b']))


def _forward(x_prompt, x_sample, state_rwkv_shift, state_rwkv_wkv, state_mamba_conv, state_mamba_ssm,
             cache_k, cache_v, page_table, state_conformer_conv, p):
    bp, tp, d = x_prompt.shape
    bs = x_sample.shape[0]
    hp = x_prompt.reshape(bp * tp, d)
    hs = x_sample.reshape(bs, d)
    npre, npost = p['norm_pre'], p['norm_post']

    (r, k, v, g, a, lw), rw_shift_p = _rwkv_proj_seq(hp, bp, tp, npre[0], p)
    o, rw_wkv_p = _rwkv_chunk(r, k, v, a, lw, bp, tp, p)
    hp = _out_proj(o, g, hp, p['rw_w_o'], npost[0])
    (r, k, v, g, a, lw), rw_shift_s = _rwkv_proj_step(hs, state_rwkv_shift, npre[0], p)
    o, rw_wkv_s = _rwkv_step(r, k, v, a, lw, state_rwkv_wkv, p)
    hs = _out_proj(o, g, hs, p['rw_w_o'], npost[0])

    mw = _mamba_weights(p)
    mv = _mamba_vecs(p)
    ident = lambda ys, rows, fulls: ys
    widths = (MB_INNER, MB_CONV_DIM, LANES)
    z, xbc, dtr = _in_proj(hp, npre[1], mw, ident, widths, tm=256)
    yg, mb_ssm_p = _mamba_chunk(z, xbc, dtr, bp, tp, mv)
    mb_conv_p = xbc.reshape(bp, tp, MB_CONV_DIM)[:, tp - (MB_CONV - 1):]
    hp = _out_proj(yg, None, hp, p['mb_w_out'], npost[1])
    z, xbc, dtr = _in_proj(hs, npre[1], mw, ident, widths)
    yg, mb_ssm_s = _mamba_step(z, xbc, dtr, state_mamba_conv, state_mamba_ssm, mv)
    mb_conv_s = jnp.concatenate([state_mamba_conv[:, 1:], xbc[:, None, :]], axis=1)
    hs = _out_proj(yg, None, hs, p['mb_w_out'], npost[1])

    lam_init = 0.8 - 0.6 * math.exp(-0.3 * 2)
    tabs_p = _rope_tables(jnp.arange(tp, dtype=jnp.int32))
    tm = min(_row_tile(tp), 256)
    tps = tp // tm
    rows_p = [(tb, lambda i: (i % tps, 0), LANES) for tb in tabs_p]
    widths = ((DA_WIDTH, F32, 'heads'), (DA_WIDTH, F32, 'heads'), DA_WIDTH,
              (DA_WIDTH, BF16, True), (DA_WIDTH, BF16, False), (DA_WIDTH, BF16, True))
    k, v, g, qt, kb, vt = _in_proj(hp, npre[2], [p['da_w_qkvg']], _da_epilogue_seq, widths,
                                   row_extras=rows_p, tm=tm)
    k_rows_p = k.reshape(bp, tp, DA_HEADS, 2 * DA_HEAD)
    v_rows_p = v.reshape(bp, tp, DA_HEADS, DA_VDIM)
    o = _flash(qt, kb, vt, bp, tp, p, lam_init)
    hp = _out_proj(o, g, hp, p['da_w_o'], npost[2])
    tabs_s = _rope_tables(jnp.full((bs,), PAST_LEN, dtype=jnp.int32))
    rows_s = [(tb, lambda i: (i, 0), LANES) for tb in tabs_s]
    q, k, v, g = _in_proj(hs, npre[2], [p['da_w_qkvg']], _da_epilogue, (DA_WIDTH,) * 4, row_extras=rows_s)
    k_rows_s = k.reshape(bs, 1, DA_HEADS, 2 * DA_HEAD)
    v_rows_s = v.reshape(bs, 1, DA_HEADS, DA_VDIM)
    o = _decode(q, k, v, cache_k, cache_v, page_table, p, lam_init)
    hs = _out_proj(o, g, hs, p['da_w_o'], npost[2])

    u, g = _in_proj(hp, npre[3], [p['cf_w_in']], _cf_epilogue, (CF_WIDTH,) * 2)
    cf_conv_p = u.reshape(bp, tp, CF_WIDTH)[:, tp - (CF_KERNEL - 1):]
    c = _cf_conv(u, bp, tp, p)
    hp = _out_proj(c, g, hp, p['cf_w_out'], npost[3])
    u, g = _in_proj(hs, npre[3], [p['cf_w_in']], _cf_epilogue, (CF_WIDTH,) * 2)
    cf_conv_s = jnp.concatenate([state_conformer_conv[:, 1:], u[:, None, :]], axis=1)
    c = _cf_step(u, state_conformer_conv, p)
    hs = _out_proj(c, g, hs, p['cf_w_out'], npost[3])

    return (hp.reshape(bp, tp, d), hs.reshape(bs, 1, d), rw_shift_p, rw_shift_s, rw_wkv_p, rw_wkv_s,
            mb_conv_p, mb_conv_s, mb_ssm_p, mb_ssm_s, k_rows_p, k_rows_s, v_rows_p, v_rows_s,
            cf_conv_p, cf_conv_s)


def kernel(x_prompt, x_sample, state_rwkv_shift, state_rwkv_wkv, state_mamba_conv, state_mamba_ssm, cache_k, cache_v, page_table, state_conformer_conv, norm_pre, norm_post, rw_mu, rw_w_r, rw_w_k, rw_w_v, rw_w_g, rw_w0, rw_w_w1, rw_w_w2, rw_a0, rw_w_a1, rw_w_a2, rw_k_k, rw_k_a, rw_r_k, rw_gn_w, rw_gn_b, rw_w_o, mb_w_in, mb_conv_w, mb_conv_b, mb_dt_bias, mb_a_log, mb_d, mb_norm_w, mb_w_out, da_w_qkvg, da_lq1, da_lk1, da_lq2, da_lk2, da_subln_w, da_w_o, cf_w_in, cf_conv_w, cf_conv_b, cf_ln_w, cf_ln_b, cf_w_out):
    p = dict(norm_pre=norm_pre, norm_post=norm_post, rw_mu=rw_mu, rw_w_r=rw_w_r, rw_w_k=rw_w_k, rw_w_v=rw_w_v,
             rw_w_g=rw_w_g, rw_w0=rw_w0, rw_w_w1=rw_w_w1, rw_w_w2=rw_w_w2, rw_a0=rw_a0, rw_w_a1=rw_w_a1,
             rw_w_a2=rw_w_a2, rw_k_k=rw_k_k, rw_k_a=rw_k_a, rw_r_k=rw_r_k, rw_gn_w=rw_gn_w, rw_gn_b=rw_gn_b,
             rw_w_o=rw_w_o, mb_w_in=mb_w_in, mb_conv_w=mb_conv_w, mb_conv_b=mb_conv_b, mb_dt_bias=mb_dt_bias,
             mb_a_log=mb_a_log, mb_d=mb_d, mb_norm_w=mb_norm_w, mb_w_out=mb_w_out, da_w_qkvg=da_w_qkvg,
             da_lq1=da_lq1, da_lk1=da_lk1, da_lq2=da_lq2, da_lk2=da_lk2, da_subln_w=da_subln_w, da_w_o=da_w_o,
             cf_w_in=cf_w_in, cf_conv_w=cf_conv_w, cf_conv_b=cf_conv_b, cf_ln_w=cf_ln_w, cf_ln_b=cf_ln_b,
             cf_w_out=cf_w_out)
    return _forward(x_prompt, x_sample, state_rwkv_shift, state_rwkv_wkv, state_mamba_conv, state_mamba_ssm,
                    cache_k, cache_v, page_table, state_conformer_conv, p)
```

```python
import functools
import math

import jax
import jax.numpy as jnp
from jax import lax
from jax.experimental import pallas as pl
from jax.experimental.pallas import tpu as pltpu

F32 = jnp.float32
BF16 = jnp.bfloat16

D_MODEL = 1024
PAST_LEN = 8192
PAGE_SIZE = 128
RMS_EPS = 1e-6

RW_HEAD = 64
RW_HEADS = D_MODEL // RW_HEAD
RW_LORA = 64
RW_GN_EPS = 64e-5
RW_CHUNK = 64
RW_GROUP = 8
RW_PAIRS = 4

MB_INNER = 2 * D_MODEL
MB_HEADDIM = 64
MB_HEADS = MB_INNER // MB_HEADDIM
MB_GROUPS = 4
MB_STATE = 128
MB_CONV = 4
MB_CHUNK = 128
MB_CONV_DIM = MB_INNER + 2 * MB_GROUPS * MB_STATE
MB_NORM_EPS = 1e-5
MB_GROUP_WIDTH = MB_INNER // MB_GROUPS

DA_HEADS = 8
DA_HEAD = 64
DA_VDIM = 2 * DA_HEAD
DA_WIDTH = DA_HEADS * DA_VDIM
DA_SCALE = DA_HEAD ** -0.5
LOG2_E = math.log2(math.e)
DA_HEADS_PER_STEP = 4
DA_SUBLN_EPS = 1e-5
ROPE_THETA = 500000.0
ROPE_DIM = DA_HEAD // 4
DEC_PAGES = 16

CF_WIDTH = D_MODEL
CF_KERNEL = 31
CF_LN_EPS = 1e-5
CF_HIST = 32
CF_ROWS = 32

LANES = 128
SUBLANES = 8
VMEM_LIMIT = 56 * 1024 * 1024

NEG_INF = float("-inf")


def _cparams(*sem):
    return pltpu.CompilerParams(dimension_semantics=sem, vmem_limit_bytes=VMEM_LIMIT)


def _dot(a, b):
    return jnp.dot(a.astype(BF16), b.astype(BF16), preferred_element_type=F32)


def _dot_nt(a, b):
    return lax.dot_general(a.astype(BF16), b.astype(BF16), (((1,), (1,)), ((), ())),
                           preferred_element_type=F32)


def _dot_tn(a, b):
    return lax.dot_general(a.astype(BF16), b.astype(BF16), (((0,), (0,)), ((), ())),
                           preferred_element_type=F32)


def _split3(a):
    hi = a.astype(BF16)
    r1 = a - hi.astype(F32)
    mid = r1.astype(BF16)
    lo = (r1 - mid.astype(F32)).astype(BF16)
    return hi, mid, lo


def _cumsum_rows(tri_bf16, x):
    hi, mid, lo = _split3(x)
    f = functools.partial(jnp.dot, preferred_element_type=F32)
    return f(tri_bf16, hi) + (f(tri_bf16, mid) + f(tri_bf16, lo))


def _sigmoid(x):
    return 1.0 / (1.0 + jnp.exp(-x))


def _silu(x):
    return x * _sigmoid(x)


def _softplus(x):
    return jnp.maximum(x, 0.0) + jnp.log(1.0 + jnp.exp(-jnp.abs(x)))


def _rms(x, w, eps):
    return x * lax.rsqrt(jnp.mean(x * x, axis=-1, keepdims=True) + eps) * w


def _row_tile(m):
    for t in (512, 256, 128, 64, 32, 16, 8):
        if m % t == 0:
            return t
    return m


def _full(shape):
    nd = len(shape)
    return pl.BlockSpec(shape, lambda *_: (0,) * nd)


def _out_proj_body(has_gate, *refs):
    if has_gate:
        y_ref, g_ref, h_ref, w_ref, pw_ref, o_ref = refs
        y = y_ref[...] * _silu(g_ref[...])
    else:
        y_ref, h_ref, w_ref, pw_ref, o_ref = refs
        y = y_ref[...]
    z = _dot(y, w_ref[...])
    o_ref[...] = h_ref[...] + _rms(z, pw_ref[...], RMS_EPS)


def _out_proj(y, gate, h, w, post_w):
    m, kdim = y.shape
    tm = _row_tile(m)
    row = lambda width: pl.BlockSpec((tm, width), lambda i: (i, 0))
    args, specs = [y], [row(kdim)]
    if gate is not None:
        args.append(gate)
        specs.append(row(D_MODEL))
    args += [h, w.astype(BF16), post_w.reshape(1, D_MODEL)]
    specs += [row(D_MODEL), _full((kdim, D_MODEL)), _full((1, D_MODEL))]
    return pl.pallas_call(
        functools.partial(_out_proj_body, gate is not None),
        grid=(m // tm,), in_specs=specs, out_specs=row(D_MODEL),
        out_shape=jax.ShapeDtypeStruct((m, D_MODEL), F32),
        compiler_params=_cparams("parallel"))(*args)


def _in_proj_body(n_w, n_row, n_full, epilogue, *refs):
    h_ref, nw_ref = refs[0], refs[1]
    w_refs = refs[2:2 + n_w]
    row_refs = refs[2 + n_w:2 + n_w + n_row]
    full_refs = refs[2 + n_w + n_row:2 + n_w + n_row + n_full]
    out_refs = refs[2 + n_w + n_row + n_full:]
    xn = _rms(h_ref[...], nw_ref[...], RMS_EPS).astype(BF16)
    ys = [jnp.dot(xn, w[...], preferred_element_type=F32) for w in w_refs]
    outs = epilogue(ys, [r[...] for r in row_refs], [r[...] for r in full_refs])
    for o_ref, o in zip(out_refs, outs):
        o_ref[...] = o.astype(o_ref.dtype).reshape(o_ref.shape)


def _in_proj(h, norm_w, weights, epilogue, out_widths, row_extras=(), full_extras=(), tm=None):
    m = h.shape[0]
    tm = tm or _row_tile(m)
    row = lambda width: pl.BlockSpec((tm, width), lambda i: (i, 0))
    args = [h, norm_w.reshape(1, D_MODEL)] + [w.astype(BF16) for w in weights]
    specs = [row(D_MODEL), _full((1, D_MODEL))] + [_full(w.shape) for w in weights]
    for arr, imap, width in row_extras:
        args.append(arr)
        specs.append(pl.BlockSpec((tm, width), imap))
    for arr in full_extras:
        args.append(arr)
        specs.append(_full(arr.shape))
    out_specs, out_shape = [], []
    for wd in out_widths:
        width, dtype, transposed = wd if isinstance(wd, tuple) else (wd, F32, False)
        if transposed == 'heads':
            shape3 = (width // LANES, LANES)
            out_specs.append(pl.BlockSpec((tm,) + shape3, lambda i: (i, 0, 0)))
            out_shape.append(jax.ShapeDtypeStruct((m,) + shape3, dtype))
        elif transposed:
            out_specs.append(pl.BlockSpec((width, tm), lambda i: (0, i)))
            out_shape.append(jax.ShapeDtypeStruct((width, m), dtype))
        else:
            out_specs.append(row(width))
            out_shape.append(jax.ShapeDtypeStruct((m, width), dtype))
    return pl.pallas_call(
        functools.partial(_in_proj_body, len(weights), len(row_extras), len(full_extras), epilogue),
        grid=(m // tm,), in_specs=specs, out_specs=out_specs, out_shape=out_shape,
        compiler_params=_cparams("parallel"))(*args)


def _rwkv_mix(xn, xprev, mu, wr, wk, wv, wg, w0, ww1, ww2, a0, wa1, wa2):
    xx = xprev - xn
    xm = [xn + xx * mu[i:i + 1, :] for i in range(6)]
    r = _dot(xm[0], wr)
    k = _dot(xm[2], wk)
    v = _dot(xm[3], wv)
    g = _dot(xm[5], wg)
    zw = w0 + _dot(jnp.tanh(_dot(xm[1], ww1)), ww2)
    w_log = -_softplus(-zw) - 0.5
    lw = -jnp.exp(w_log)
    a = _sigmoid(a0 + _dot(_dot(xm[4], wa1), wa2))
    return r, k, v, g, a, lw


def _rwkv_proj_seq_body(tiles_per_seq, h_ref, hp_ref, nw_ref, mu_ref, wr, wk, wv, wg, w0, ww1, ww2,
                        a0, wa1, wa2, r_ref, k_ref, v_ref, g_ref, a_ref, lw_ref, last_ref):
    i = pl.program_id(0)
    nw = nw_ref[...]
    xn = _rms(h_ref[...], nw, RMS_EPS)
    tm = xn.shape[0]
    pn = _rms(hp_ref[...], nw, RMS_EPS)[SUBLANES - 1:SUBLANES, :]
    first = (i % tiles_per_seq) == 0
    prow = jnp.where(first, 0.0, pn)
    rows = lax.broadcasted_iota(jnp.int32, xn.shape, 0)
    xprev = jnp.where(rows == 0, prow, pltpu.roll(xn, 1, axis=0))
    outs = _rwkv_mix(xn, xprev, mu_ref[...], wr[...], wk[...], wv[...], wg[...], w0[...], ww1[...],
                     ww2[...], a0[...], wa1[...], wa2[...])
    for o_ref, o in zip((r_ref, k_ref, v_ref, g_ref, a_ref, lw_ref), outs):
        o_ref[...] = o
    last_ref[0] = xn[tm - 1:tm, :]


def _rwkv_proj_step_body(h_ref, sh_ref, nw_ref, mu_ref, wr, wk, wv, wg, w0, ww1, ww2,
                         a0, wa1, wa2, r_ref, k_ref, v_ref, g_ref, a_ref, lw_ref, last_ref):
    xn = _rms(h_ref[...], nw_ref[...], RMS_EPS)
    outs = _rwkv_mix(xn, sh_ref[...], mu_ref[...], wr[...], wk[...], wv[...], wg[...], w0[...], ww1[...],
                     ww2[...], a0[...], wa1[...], wa2[...])
    for o_ref, o in zip((r_ref, k_ref, v_ref, g_ref, a_ref, lw_ref), outs):
        o_ref[...] = o
    last_ref[...] = xn


def _rwkv_weights(p):
    d = D_MODEL
    vec = lambda x: x.reshape(1, d)
    args = [p['rw_mu'], p['rw_w_r'].astype(BF16), p['rw_w_k'].astype(BF16), p['rw_w_v'].astype(BF16),
            p['rw_w_g'].astype(BF16), vec(p['rw_w0']), p['rw_w_w1'].astype(BF16), p['rw_w_w2'].astype(BF16),
            vec(p['rw_a0']), p['rw_w_a1'].astype(BF16), p['rw_w_a2'].astype(BF16)]
    return args, [_full(a.shape) for a in args]


def _rwkv_proj_seq(h2, b, t, norm_w, p):
    m = b * t
    tm = min(_row_tile(t), 256)
    tps = t // tm
    row = pl.BlockSpec((tm, D_MODEL), lambda i: (i, 0))
    prev = pl.BlockSpec((SUBLANES, D_MODEL), lambda i: (jnp.maximum(i * (tm // SUBLANES) - 1, 0), 0))
    wargs, wspecs = _rwkv_weights(p)
    outs = pl.pallas_call(
        functools.partial(_rwkv_proj_seq_body, tps),
        grid=(m // tm,),
        in_specs=[row, prev, _full((1, D_MODEL))] + wspecs,
        out_specs=[row] * 6 + [pl.BlockSpec((1, 1, D_MODEL), lambda i: (i // tps, 0, 0))],
        out_shape=[jax.ShapeDtypeStruct((m, D_MODEL), F32)] * 6
        + [jax.ShapeDtypeStruct((b, 1, D_MODEL), F32)],
        compiler_params=_cparams("arbitrary"))(h2, h2, norm_w.reshape(1, D_MODEL), *wargs)
    return outs[:6], outs[6].reshape(b, D_MODEL)


def _rwkv_proj_step(h2, shift, norm_w, p):
    m = h2.shape[0]
    wargs, wspecs = _rwkv_weights(p)
    outs = pl.pallas_call(
        _rwkv_proj_step_body,
        grid=(1,),
        in_specs=[_full((m, D_MODEL)), _full((m, D_MODEL)), _full((1, D_MODEL))] + wspecs,
        out_specs=[_full((m, D_MODEL))] * 7,
        out_shape=[jax.ShapeDtypeStruct((m, D_MODEL), F32)] * 7,
        compiler_params=_cparams("arbitrary"))(h2, shift, norm_w.reshape(1, D_MODEL), *wargs)
    return outs[:6], outs[6]


def _pair_sum(x, m0):
    s0 = jnp.sum(jnp.where(m0, x, 0.0), axis=-1, keepdims=True)
    s1 = jnp.sum(jnp.where(m0, 0.0, x), axis=-1, keepdims=True)
    return jnp.where(m0, s0, s1)


def _rwkv_chunk_body(gsz, npair, r_ref, k_ref, v_ref, a_ref, lw_ref, kk_ref, ka_ref, rk_ref, gw_ref, gb_ref,
                     o_ref, s_ref, bd_ref):
    cg = pl.program_id(2)
    ncg = pl.num_programs(2)
    cl = RW_CHUNK
    two = 2 * cl

    @pl.when(cg == 0)
    def _():
        bd_ref[...] = jnp.zeros_like(bd_ref)

    lane = lax.broadcasted_iota(jnp.int32, (cl, LANES), 1)
    m0 = lane < RW_HEAD
    ti = lax.broadcasted_iota(jnp.int32, (cl, cl), 0)
    tj = lax.broadcasted_iota(jnp.int32, (cl, cl), 1)
    tri = jnp.where(ti >= tj, 1.0, 0.0).astype(BF16)
    row2 = lax.broadcasted_iota(jnp.int32, (two, LANES), 0)
    col2 = lax.broadcasted_iota(jnp.int32, (two, LANES), 1)
    same = (row2 >= cl) == (col2 >= RW_HEAD)
    step_r = jnp.bitwise_and(row2, cl - 1)
    step_c = jnp.bitwise_and(col2, cl - 1)
    bd_strict = jnp.logical_and(same, step_r > step_c)
    bd_incl = jnp.logical_and(same, step_r >= step_c)
    eye = row2 == col2

    def stack(x):
        return jnp.where(same, jnp.concatenate([x, x], axis=0), 0.0)

    chunks = range(gsz)
    st = []
    for pi in range(npair):
        cols = slice(pi * LANES, (pi + 1) * LANES)
        kk_w, ka_w, rk_w = kk_ref[:, cols], ka_ref[:, cols], rk_ref[:, cols]
        for gi in chunks:
            sl = slice(gi * cl, (gi + 1) * cl)
            r = r_ref[0, sl, cols]
            k = k_ref[0, sl, cols]
            v = v_ref[0, sl, cols]
            a = a_ref[0, sl, cols]
            lw = lw_ref[0, sl, cols]
            kkv = k * kk_w
            kap = kkv * lax.rsqrt(jnp.maximum(_pair_sum(kkv * kkv, m0), 1e-24))
            kmod = k * (1.0 + (a - 1.0) * ka_w)
            st.append(dict(r=r, v=v, lw=lw, kap=kap, kmod=kmod, bvec=kap * a,
                           bonus=_pair_sum(r * kmod * rk_w, m0) * v))
    for s in st:
        s['cum'] = _cumsum_rows(tri, s['lw'])
    for s in st:
        cum = s['cum']
        cum_last = cum[cl - 1:cl, :]
        g_inv = jnp.exp(-cum)
        g_end = jnp.exp(cum_last - cum)
        s['g_all'] = jnp.exp(cum_last)
        s['kapm'] = stack(s['kap'] * jnp.exp(cum - s['lw']))
        s['rm'] = stack(s['r'] * jnp.exp(cum))
        s['v2m'] = stack(s['v'])
        s['b_h'] = s['bvec'] * g_inv
        s['k_h'] = s['kmod'] * g_inv
        s['b_e'] = stack(s['bvec'] * g_end)
        s['k_e'] = stack(s['kmod'] * g_end)
    for s in st:
        lhs = jnp.concatenate([s['kapm'], s['rm']], axis=0)
        gb = _dot_nt(lhs, jnp.concatenate([s['b_h'], s['b_h']], axis=0))
        gk = _dot_nt(lhs, jnp.concatenate([s['k_h'], s['k_h']], axis=0))
        s['a_ab'] = jnp.where(bd_strict, gb[:two], 0.0)
        s['a_rb'] = jnp.where(bd_incl, gb[two:], 0.0)
        s['a_ak'] = jnp.where(bd_strict, gk[:two], 0.0)
        s['a_rk'] = jnp.where(bd_incl, gk[two:], 0.0)
    for s in st:
        s['tinv'] = jnp.where(eye, 1.0, 0.0) - s['a_ab']
        s['lp'] = _dot(s['a_ab'], s['a_ab'])
        s['x2m'] = _dot(s['a_ak'], s['v2m'])
    for _ in range(4):
        for s in st:
            both = _dot(jnp.concatenate([s['tinv'], s['lp']], axis=0), s['lp'])
            s['tinv'] = s['tinv'] + both[:two]
            s['lp'] = both[two:]
    for s in st:
        s['tinv'] = s['tinv'] + _dot(s['tinv'], s['lp'])
    for s in st:
        pq = _dot(s['tinv'], jnp.concatenate([s['kapm'], s['x2m']], axis=1))
        s['pm'] = -pq[:, :LANES]
        s['qm'] = -pq[:, LANES:]
    for s in st:
        ro = _dot(s['a_rb'], jnp.concatenate([s['pm'], s['qm']], axis=1))
        rtm = s['rm'] + ro[:, :LANES]
        oim = ro[:, LANES:] + _dot(s['a_rk'], s['v2m'])
        s['rt'] = rtm[:cl] + rtm[cl:]
        s['oi'] = oim[:cl] + oim[cl:]
    for s in st:
        s['corr'] = _dot_tn(s['pm'], s['b_e'])
        s['nmat'] = _dot_tn(jnp.concatenate([s['qm'], s['v2m']], axis=0),
                            jnp.concatenate([s['b_e'], s['k_e']], axis=0))
    bds = [bd_ref[pi] for pi in range(npair)]
    for gi in chunks:
        for pi in range(npair):
            s = st[pi * gsz + gi]
            cols = slice(pi * LANES, (pi + 1) * LANES)
            o = _dot_nt(s['rt'], bds[pi]) + s['oi']
            bds[pi] = bds[pi] * s['g_all'] + _dot(bds[pi], s['corr']) + s['nmat']
            mean = _pair_sum(o, m0) * (1.0 / RW_HEAD)
            oc = o - mean
            var = _pair_sum(oc * oc, m0) * (1.0 / RW_HEAD)
            o_ref[0, gi * cl:(gi + 1) * cl, cols] = (
                oc * lax.rsqrt(var + RW_GN_EPS) * gw_ref[:, cols] + gb_ref[:, cols] + s['bonus'])
    for pi in range(npair):
        bd_ref[pi] = bds[pi]

    @pl.when(cg == ncg - 1)
    def _():
        for pi in range(npair):
            s_ref[0, 2 * pi] = bds[pi][:RW_HEAD, :RW_HEAD]
            s_ref[0, 2 * pi + 1] = bds[pi][RW_HEAD:, RW_HEAD:]


def _rwkv_chunk(r, k, v, a, lw, b, t, p):
    npair = RW_PAIRS
    hp = RW_HEADS // (2 * npair)
    gsz = RW_GROUP if t % (RW_GROUP * RW_CHUNK) == 0 else 1
    rows = gsz * RW_CHUNK
    shp = lambda x: x.reshape(b, t, D_MODEL)
    blk = pl.BlockSpec((1, rows, npair * LANES), lambda bi, hi, ci: (bi, ci, hi))
    vec = pl.BlockSpec((1, npair * LANES), lambda bi, hi, ci: (0, hi))
    vrow = lambda x: x.reshape(1, D_MODEL)
    o, s = pl.pallas_call(
        functools.partial(_rwkv_chunk_body, gsz, npair),
        grid=(b, hp, t // rows),
        in_specs=[blk] * 5 + [vec] * 5,
        out_specs=[blk, pl.BlockSpec((1, 2 * npair, RW_HEAD, RW_HEAD), lambda bi, hi, ci: (bi, hi, 0, 0))],
        out_shape=[jax.ShapeDtypeStruct((b, t, D_MODEL), F32),
                   jax.ShapeDtypeStruct((b, RW_HEADS, RW_HEAD, RW_HEAD), F32)],
        scratch_shapes=[pltpu.VMEM((npair, LANES, LANES), F32)],
        compiler_params=_cparams("parallel", "parallel", "arbitrary"))(
            shp(r), shp(k), shp(v), shp(a), shp(lw), vrow(p['rw_k_k']), vrow(p['rw_k_a']),
            vrow(p['rw_r_k']), vrow(p['rw_gn_w']), vrow(p['rw_gn_b']))
    return o.reshape(b * t, D_MODEL), s


def _rwkv_step_body(r_ref, k_ref, vc_ref, a_ref, lw_ref, s_ref, kk_ref, ka_ref, rk_ref, gw_ref, gb_ref,
                    o_ref, so_ref):
    r = r_ref[0]
    k = k_ref[0]
    a = a_ref[0]
    lw = lw_ref[0]
    vc = vc_ref[0]
    s = s_ref[0]
    kkv = k * kk_ref[...]
    kap = kkv * lax.rsqrt(jnp.maximum(jnp.sum(kkv * kkv, axis=-1, keepdims=True), 1e-24))
    kmod = k * (1.0 + (a - 1.0) * ka_ref[...])
    bvec = kap * a
    dec = jnp.exp(lw)
    sa = jnp.sum(s * kap, axis=-1, keepdims=True)
    s_new = s * dec - sa * bvec + vc * kmod
    so_ref[0] = s_new
    o = jnp.sum(s_new * r, axis=-1, keepdims=True)
    mean = jnp.mean(o, axis=1, keepdims=True)
    oc = o - mean
    var = jnp.mean(oc * oc, axis=1, keepdims=True)
    on = oc * lax.rsqrt(var + RW_GN_EPS) * gw_ref[...] + gb_ref[...]
    bonus = jnp.sum(r * kmod * rk_ref[...], axis=-1, keepdims=True)
    o_ref[0] = on + bonus * vc


def _rwkv_step(r, k, v, a, lw, state, p):
    b = r.shape[0]
    hrow = (RW_HEADS, 1, RW_HEAD)
    hcol = (RW_HEADS, RW_HEAD, 1)
    rows = lambda x: x.reshape((b,) + hrow)
    rblk = pl.BlockSpec((1,) + hrow, lambda i: (i, 0, 0, 0))
    cblk = pl.BlockSpec((1,) + hcol, lambda i: (i, 0, 0, 0))
    sblk = pl.BlockSpec((1, RW_HEADS, RW_HEAD, RW_HEAD), lambda i: (i, 0, 0, 0))
    o, s = pl.pallas_call(
        _rwkv_step_body,
        grid=(b,),
        in_specs=[rblk, rblk, cblk, rblk, rblk, sblk] + [_full(hrow)] * 3 + [_full(hcol)] * 2,
        out_specs=[cblk, sblk],
        out_shape=[jax.ShapeDtypeStruct((b,) + hcol, F32),
                   jax.ShapeDtypeStruct((b, RW_HEADS, RW_HEAD, RW_HEAD), F32)],
        compiler_params=_cparams("parallel"))(
            rows(r), rows(k), v.reshape((b,) + hcol), rows(a), rows(lw), state,
            p['rw_k_k'].reshape(hrow), p['rw_k_a'].reshape(hrow), p['rw_r_k'].reshape(hrow),
            p['rw_gn_w'].reshape(hcol), p['rw_gn_b'].reshape(hcol))
    return o.reshape(b, D_MODEL), s


def _mamba_weights(p):
    w = p['mb_w_in']
    wz = w[:, :MB_INNER]
    wx = w[:, MB_INNER:MB_INNER + MB_CONV_DIM]
    wdt = jnp.pad(w[:, MB_INNER + MB_CONV_DIM:], ((0, 0), (0, LANES - MB_HEADS)))
    return [wz, wx, wdt]


def _mamba_vecs(p):
    pad = lambda x: jnp.pad(x.reshape(1, MB_HEADS), ((0, 0), (0, LANES - MB_HEADS)))
    return dict(conv_w=p['mb_conv_w'], conv_b=p['mb_conv_b'].reshape(1, MB_CONV_DIM),
                dt_bias=pad(p['mb_dt_bias']), a_log=pad(p['mb_a_log']),
                d_skip=jnp.repeat(p['mb_d'], MB_HEADDIM).reshape(1, MB_INNER),
                norm_w=p['mb_norm_w'].reshape(1, MB_INNER))


def _mamba_epilogue(ys, rows, fulls):
    z, xbc, dt_raw = ys
    return _silu(z), xbc, _softplus(dt_raw + fulls[0])


def _mamba_chunk_body(x_ref, xp_ref, z_ref, dt_ref, cw_ref, cb_ref, alog_ref, dsk_ref, nw_ref,
                      y_ref, fs_ref, st_ref):
    c = pl.program_id(1)
    nc = pl.num_programs(1)
    cl = MB_CHUNK

    @pl.when(c == 0)
    def _():
        st_ref[...] = jnp.zeros_like(st_ref)

    hist = jnp.where(c == 0, 0.0, xp_ref[0])
    xcat = jnp.concatenate([hist, x_ref[0]], axis=0)
    cw = cw_ref[...]
    conv = cb_ref[...] + cw[0:1, :] * xcat[5:5 + cl]
    for j in range(1, MB_CONV):
        conv = conv + cw[j:j + 1, :] * xcat[5 + j:5 + j + cl]
    conv = _silu(conv)

    dt = dt_ref[0]
    da = dt * (-jnp.exp(alog_ref[...]))
    ti = lax.broadcasted_iota(jnp.int32, (cl, cl), 0)
    tj = lax.broadcasted_iota(jnp.int32, (cl, cl), 1)
    causal = ti >= tj
    tri = jnp.where(causal, 1.0, 0.0).astype(BF16)
    a_cs = _cumsum_rows(tri, da)
    a_cs_t = a_cs.T
    a_last = a_cs[cl - 1:cl, :]
    dec_to_end = jnp.exp(a_last - a_cs)
    e_cs = jnp.exp(a_cs)
    e_last = jnp.exp(a_last)

    lane = lax.broadcasted_iota(jnp.int32, (cl, LANES), 1)
    m0 = lane < MB_HEADDIM
    heads_per_group = MB_HEADS // MB_GROUPS
    pairs_per_group = heads_per_group // 2
    for g in range(MB_GROUPS):
        bm = conv[:, MB_INNER + g * MB_STATE:MB_INNER + (g + 1) * MB_STATE]
        cm = conv[:, MB_INNER + (MB_GROUPS + g) * MB_STATE:MB_INNER + (MB_GROUPS + g + 1) * MB_STATE]
        cb = _dot_nt(cm, bm)
        bm_t = bm.T.astype(BF16)
        blocks = []
        for j in range(pairs_per_group):
            pi = g * pairs_per_group + j
            h0, h1 = 2 * pi, 2 * pi + 1
            lo, hi = pi * LANES, (pi + 1) * LANES
            xs = conv[:, lo:hi]
            dt2 = jnp.where(m0, dt[:, h0:h0 + 1], dt[:, h1:h1 + 1])
            xdt = xs * dt2
            l0 = jnp.exp(jnp.where(causal, a_cs[:, h0:h0 + 1] - a_cs_t[h0:h0 + 1, :], NEG_INF))
            l1 = jnp.exp(jnp.where(causal, a_cs[:, h1:h1 + 1] - a_cs_t[h1:h1 + 1, :], NEG_INF))
            lhs = jnp.concatenate([(cb * l0).astype(BF16), (cb * l1).astype(BF16)], axis=1)
            rhs = jnp.concatenate([jnp.where(m0, xdt, 0.0).astype(BF16),
                                   jnp.where(m0, 0.0, xdt).astype(BF16)], axis=0)
            y = jnp.dot(lhs, rhs, preferred_element_type=F32)
            st = st_ref[pi]
            e2 = jnp.where(m0, e_cs[:, h0:h0 + 1], e_cs[:, h1:h1 + 1])
            y = y + _dot(cm, st) * e2
            d2 = jnp.where(m0, dec_to_end[:, h0:h0 + 1], dec_to_end[:, h1:h1 + 1])
            sc = jnp.where(m0, e_last[:, h0:h0 + 1], e_last[:, h1:h1 + 1])
            st_ref[pi] = st * sc + jnp.dot(bm_t, (xdt * d2).astype(BF16), preferred_element_type=F32)
            y = y + dsk_ref[:, lo:hi] * xs
            blocks.append(y * z_ref[0, :, lo:hi])
        ssq = blocks[0] * blocks[0]
        for blk in blocks[1:]:
            ssq = ssq + blk * blk
        scale = lax.rsqrt(jnp.sum(ssq, axis=-1, keepdims=True) * (1.0 / MB_GROUP_WIDTH) + MB_NORM_EPS)
        for j, blk in enumerate(blocks):
            lo = (g * pairs_per_group + j) * LANES
            y_ref[0, :, lo:lo + LANES] = blk * scale * nw_ref[:, lo:lo + LANES]

    @pl.when(c == nc - 1)
    def _():
        for pi in range(MB_HEADS // 2):
            fs_ref[0, pi] = st_ref[pi].T


def _mamba_chunk(z, xbc, dtr, b, t, vecs):
    nc = t // MB_CHUNK
    npairs = MB_HEADS // 2
    per = MB_CHUNK // SUBLANES
    y, fs = pl.pallas_call(
        _mamba_chunk_body,
        grid=(b, nc),
        in_specs=[pl.BlockSpec((1, MB_CHUNK, MB_CONV_DIM), lambda bi, ci: (bi, ci, 0)),
                  pl.BlockSpec((1, SUBLANES, MB_CONV_DIM),
                               lambda bi, ci: (bi, jnp.maximum(ci * per - 1, 0), 0)),
                  pl.BlockSpec((1, MB_CHUNK, MB_INNER), lambda bi, ci: (bi, ci, 0)),
                  pl.BlockSpec((1, MB_CHUNK, LANES), lambda bi, ci: (bi, ci, 0)),
                  _full((MB_CONV, MB_CONV_DIM)), _full((1, MB_CONV_DIM)),
                  _full((1, LANES)), _full((1, MB_INNER)), _full((1, MB_INNER))],
        out_specs=[pl.BlockSpec((1, MB_CHUNK, MB_INNER), lambda bi, ci: (bi, ci, 0)),
                   pl.BlockSpec((1, npairs, LANES, MB_STATE), lambda bi, ci: (bi, 0, 0, 0))],
        out_shape=[jax.ShapeDtypeStruct((b, t, MB_INNER), F32),
                   jax.ShapeDtypeStruct((b, npairs, LANES, MB_STATE), F32)],
        scratch_shapes=[pltpu.VMEM((npairs, LANES, MB_STATE), F32)],
        compiler_params=_cparams("parallel", "arbitrary"))(
            xbc.reshape(b, t, MB_CONV_DIM), xbc.reshape(b, t, MB_CONV_DIM), z.reshape(b, t, MB_INNER),
            dtr.reshape(b, t, LANES), vecs['conv_w'], vecs['conv_b'], vecs['a_log'],
            vecs['d_skip'], vecs['norm_w'])
    return y.reshape(b * t, MB_INNER), fs.reshape(b, MB_HEADS, MB_HEADDIM, MB_STATE)


def _mamba_step_conv_body(x_ref, cs_ref, dt_ref, cw_ref, cb_ref, alog_ref, conv_ref, ed_o):
    cw = cw_ref[...]
    conv = cb_ref[...] + cw[MB_CONV - 1:MB_CONV, :] * x_ref[...]
    for j in range(MB_CONV - 1):
        conv = conv + cw[j:j + 1, :] * cs_ref[:, j, :]
    conv_ref[...] = _silu(conv)
    ed_o[...] = jnp.exp(dt_ref[...] * (-jnp.exp(alog_ref[...])))


def _mamba_step_state_body(s_ref, xc_ref, dt_ref, ed_ref, bm_ref, cm_ref, so_ref, y_ref):
    hpg = MB_HEADS // MB_GROUPS
    for g in range(MB_GROUPS):
        hs = slice(g * hpg, (g + 1) * hpg)
        s = s_ref[0, hs]
        xdt = xc_ref[0, hs] * dt_ref[0, hs]
        fin = s * ed_ref[0, hs] + xdt * bm_ref[0, g:g + 1]
        so_ref[0, hs] = fin
        y_ref[0, hs] = jnp.sum(fin * cm_ref[0, g:g + 1], axis=-1, keepdims=True)


def _mamba_step_out_body(y_ref, x_ref, z_ref, dsk_ref, nw_ref, o_ref):
    y = (y_ref[...] + dsk_ref[...] * x_ref[...]) * z_ref[...]
    for g in range(MB_GROUPS):
        lo, hi = g * MB_GROUP_WIDTH, (g + 1) * MB_GROUP_WIDTH
        o_ref[:, lo:hi] = _rms(y[:, lo:hi], nw_ref[:, lo:hi], MB_NORM_EPS)


def _mamba_step(z, xbc, dt, conv_state, ssm_state, vecs):
    b = z.shape[0]
    conv, ed = pl.pallas_call(
        _mamba_step_conv_body,
        grid=(1,),
        in_specs=[_full((b, MB_CONV_DIM)), _full((b, MB_CONV - 1, MB_CONV_DIM)), _full((b, LANES)),
                  _full((MB_CONV, MB_CONV_DIM)), _full((1, MB_CONV_DIM)), _full((1, LANES))],
        out_specs=[_full((b, MB_CONV_DIM)), _full((b, LANES))],
        out_shape=[jax.ShapeDtypeStruct((b, MB_CONV_DIM), F32), jax.ShapeDtypeStruct((b, LANES), F32)],
        compiler_params=_cparams("arbitrary"))(
            xbc, conv_state, dt, vecs['conv_w'], vecs['conv_b'], vecs['a_log'])
    xs = conv[:, :MB_INNER]
    bm = conv[:, MB_INNER:MB_INNER + MB_GROUPS * MB_STATE].reshape(b, MB_GROUPS, MB_STATE)
    cm = conv[:, MB_INNER + MB_GROUPS * MB_STATE:].reshape(b, MB_GROUPS, MB_STATE)
    hp1 = (MB_HEADS, MB_HEADDIM, 1)
    h11 = (MB_HEADS, 1, 1)
    per = lambda shape: pl.BlockSpec((1,) + shape, lambda i: (i,) + (0,) * len(shape))
    st_shape = (MB_HEADS, MB_HEADDIM, MB_STATE)
    fin, y = pl.pallas_call(
        _mamba_step_state_body,
        grid=(b,),
        in_specs=[per(st_shape), per(hp1), per(h11), per(h11), per((MB_GROUPS, MB_STATE)),
                  per((MB_GROUPS, MB_STATE))],
        out_specs=[per(st_shape), per(hp1)],
        out_shape=[jax.ShapeDtypeStruct((b,) + st_shape, F32), jax.ShapeDtypeStruct((b,) + hp1, F32)],
        compiler_params=_cparams("parallel"))(
            ssm_state, xs.reshape((b,) + hp1), dt[:, :MB_HEADS].reshape((b,) + h11),
            ed[:, :MB_HEADS].reshape((b,) + h11), bm, cm)
    yg = pl.pallas_call(
        _mamba_step_out_body,
        grid=(1,),
        in_specs=[_full((b, MB_INNER))] * 3 + [_full((1, MB_INNER))] * 2,
        out_specs=_full((b, MB_INNER)),
        out_shape=jax.ShapeDtypeStruct((b, MB_INNER), F32),
        compiler_params=_cparams("arbitrary"))(
            y.reshape(b, MB_INNER), xs, z, vecs['d_skip'], vecs['norm_w'])
    return yg, fin


def _rope_tables(pos):
    half = ROPE_DIM // 2
    inv_freq = ROPE_THETA ** (-jnp.arange(0, ROPE_DIM, 2, dtype=F32) / ROPE_DIM)
    ang = pos.astype(F32)[:, None] * inv_freq[None, :]
    cos, sin = jnp.cos(ang), jnp.sin(ang)
    n = pos.shape[0]
    ones = jnp.ones((n, DA_HEAD - ROPE_DIM), F32)
    zeros = jnp.zeros((n, DA_HEAD - ROPE_DIM), F32)
    zh = jnp.zeros((n, half), F32)
    ct = jnp.concatenate([cos, cos, ones], axis=1)
    s_up = jnp.concatenate([-sin, zh, zeros], axis=1)
    s_dn = jnp.concatenate([zh, sin, zeros], axis=1)
    two = lambda x: jnp.concatenate([x, x], axis=1)
    return two(ct), two(s_up), two(s_dn)


def _rope(x, ct, s_up, s_dn):
    half = ROPE_DIM // 2
    cols = []
    for j in range(x.shape[1] // LANES):
        xj = x[:, j * LANES:(j + 1) * LANES]
        cols.append(xj * ct + pltpu.roll(xj, LANES - half, axis=1) * s_up + pltpu.roll(xj, half, axis=1) * s_dn)
    return jnp.concatenate(cols, axis=1)


def _da_epilogue(ys, rows, fulls):
    y = ys[0]
    ct, s_up, s_dn = rows
    q = _rope(y[:, :DA_WIDTH], ct, s_up, s_dn)
    k = _rope(y[:, DA_WIDTH:2 * DA_WIDTH], ct, s_up, s_dn)
    return q, k, y[:, 2 * DA_WIDTH:3 * DA_WIDTH], y[:, 3 * DA_WIDTH:]


def _lambda(lq1, lk1, lq2, lk2, lam_init):
    return (jnp.exp(jnp.sum(lq1 * lk1, axis=-1, keepdims=True))
            - jnp.exp(jnp.sum(lq2 * lk2, axis=-1, keepdims=True)) + lam_init)


def _da_epilogue_seq(ys, rows, fulls):
    q, k, v, g = _da_epilogue(ys, rows, fulls)
    return k, v, g, (q * (DA_SCALE * LOG2_E)).T, k, v.T


def _flash_body(lam_init, nh, qi_ref, kj_ref, qt_ref, k_ref, vt_ref, lq1, lk1, lq2, lk2, sw_ref, o_ref,
                m_s, l_s, acc):
    step = pl.program_id(2)
    qi = qi_ref[step]
    kj = kj_ref[step]

    @pl.when(kj == 0)
    def _():
        m_s[...] = jnp.full_like(m_s, NEG_INF)
        l_s[...] = jnp.zeros_like(l_s)
        acc[...] = jnp.zeros_like(acc)

    def update(masked):
        tk, tq = k_ref.shape[0], qt_ref.shape[1]
        frow = lax.broadcasted_iota(jnp.int32, (DA_VDIM, tq), 0)
        zero = jnp.zeros((DA_VDIM, tq), BF16)
        if masked:
            keep = (lax.broadcasted_iota(jnp.int32, (tk, tq), 0)
                    <= lax.broadcasted_iota(jnp.int32, (tk, tq), 1))
        scores, vts = [], []
        for hd in range(nh):
            sl = slice(hd * DA_VDIM, (hd + 1) * DA_VDIM)
            qt = qt_ref[sl, :]
            k = k_ref[:, sl]
            scores.append(jnp.dot(k, jnp.where(frow < DA_HEAD, qt, zero), preferred_element_type=F32))
            scores.append(jnp.dot(k, jnp.where(frow < DA_HEAD, zero, qt), preferred_element_type=F32))
            vts.append(vt_ref[sl, :])
        for i, s in enumerate(scores):
            if masked:
                s = jnp.where(keep, s, NEG_INF)
            m_old = m_s[i]
            m_new = jnp.maximum(m_old, jnp.max(s, axis=0, keepdims=True))
            alpha = jnp.exp2(m_old - m_new)
            pr = jnp.exp2(s - m_new)
            l_s[i] = l_s[i] * alpha + jnp.sum(pr, axis=0, keepdims=True)
            acc[i] = acc[i] * alpha + jnp.dot(vts[i // 2], pr.astype(BF16), preferred_element_type=F32)
            m_s[i] = m_new

    @pl.when(kj < qi)
    def _():
        update(False)

    @pl.when(kj == qi)
    def _():
        update(True)
        lam = _lambda(lq1[...], lk1[...], lq2[...], lk2[...], lam_init)
        for hd in range(nh):
            i = 2 * hd
            ot = acc[i] / l_s[i] - lam * (acc[i + 1] / l_s[i + 1])
            scale = lax.rsqrt(jnp.mean(ot * ot, axis=0, keepdims=True) + DA_SUBLN_EPS)
            o_ref[:, hd * DA_VDIM:(hd + 1) * DA_VDIM] = (ot * scale * (sw_ref[...] * (1.0 - lam_init))).T


def _flash(qt, kb, vt, b, t, p, lam_init):
    tq = min(t, 512)
    nq = t // tq
    pairs = [(qi, kj) for qi in range(nq) for kj in range(qi + 1)]
    qi_tab = jnp.asarray([pr[0] for pr in pairs], jnp.int32)
    kj_tab = jnp.asarray([pr[1] for pr in pairs], jnp.int32)
    nh = DA_HEADS_PER_STEP
    wd = nh * DA_VDIM
    qblk = pl.BlockSpec((wd, tq), lambda bi, hi, s, qt_, kt_: (hi, bi * nq + qt_[s]))
    vblk = pl.BlockSpec((wd, tq), lambda bi, hi, s, qt_, kt_: (hi, bi * nq + kt_[s]))
    kblk = pl.BlockSpec((tq, wd), lambda bi, hi, s, qt_, kt_: (bi * nq + kt_[s], hi))
    oblk = pl.BlockSpec((tq, wd), lambda bi, hi, s, qt_, kt_: (bi * nq + qt_[s], hi))
    small = lambda shape: pl.BlockSpec(shape, lambda bi, hi, s, qt_, kt_: (0, 0))
    vec = lambda x: x.reshape(1, -1)
    stat = pltpu.VMEM((2 * nh, 1, tq), F32)
    acc = pltpu.VMEM((2 * nh, DA_VDIM, tq), F32)
    return pl.pallas_call(
        functools.partial(_flash_body, lam_init, nh),
        grid_spec=pltpu.PrefetchScalarGridSpec(
            num_scalar_prefetch=2, grid=(b, DA_HEADS // nh, len(pairs)),
            in_specs=[qblk, kblk, vblk] + [small((1, DA_HEAD))] * 4 + [small((DA_VDIM, 1))],
            out_specs=oblk,
            scratch_shapes=[stat, stat, acc]),
        out_shape=jax.ShapeDtypeStruct((b * t, DA_WIDTH), F32),
        compiler_params=_cparams("parallel", "parallel", "arbitrary"))(
            qi_tab, kj_tab, qt, kb, vt, vec(p['da_lq1']), vec(p['da_lk1']), vec(p['da_lq2']),
            vec(p['da_lk2']), p['da_subln_w'].reshape(DA_VDIM, 1))


def _decode_body(lam_init, npg, pt_ref, q_ref, kn_ref, vn_ref, *refs):
    ck_refs = refs[:npg]
    cv_refs = refs[npg:2 * npg]
    lq1, lk1, lq2, lk2, sw_ref, o_ref, qm_s, m_s, l_s, acc = refs[2 * npg:]
    j = pl.program_id(1)
    nj = pl.num_programs(1)
    nrow = 2 * DA_HEADS
    ncol = PAGE_SIZE * DA_HEADS
    rid = lax.broadcasted_iota(jnp.int32, (nrow, LANES), 0)

    @pl.when(j == 0)
    def _():
        comp = lax.broadcasted_iota(jnp.int32, (nrow, LANES), 1) // DA_HEAD
        qm = jnp.where(jnp.bitwise_and(rid, 1) == comp, q_ref[0] * DA_SCALE, 0.0)
        qm_s[...] = qm
        m_s[...] = jnp.sum(qm * kn_ref[0], axis=-1, keepdims=True)
        l_s[...] = jnp.ones_like(l_s)
        acc[...] = vn_ref[0]

    qm = qm_s[...]
    keep = (lax.rem(lax.broadcasted_iota(jnp.int32, (nrow, ncol), 1), DA_HEADS)
            == lax.broadcasted_iota(jnp.int32, (nrow, ncol), 0) // 2)
    scores = [jnp.where(keep, _dot_nt(qm, ck[0].reshape(ncol, DA_VDIM)), NEG_INF) for ck in ck_refs]
    m_old = m_s[...]
    m_new = m_old
    for s in scores:
        m_new = jnp.maximum(m_new, jnp.max(s, axis=-1, keepdims=True))
    alpha = jnp.exp(m_old - m_new)
    l_new = l_s[...] * alpha
    a_new = acc[...] * alpha
    for s, cv in zip(scores, cv_refs):
        pr = jnp.exp(s - m_new)
        l_new = l_new + jnp.sum(pr, axis=-1, keepdims=True)
        a_new = a_new + _dot(pr, cv[0].reshape(ncol, DA_VDIM))
    m_s[...] = m_new
    l_s[...] = l_new
    acc[...] = a_new

    @pl.when(j == nj - 1)
    def _():
        lam = _lambda(lq1[...], lk1[...], lq2[...], lk2[...], lam_init)
        d = (a_new / l_new) * jnp.where(jnp.bitwise_and(rid, 1) == 0, 1.0, -lam)
        pair = d + pltpu.roll(d, nrow - 1, axis=0)
        o_ref[0] = _rms(pair, sw_ref[...], DA_SUBLN_EPS) * (1.0 - lam_init)


def _decode(q, k_new, v_new, cache_k, cache_v, page_table, p, lam_init):
    b = q.shape[0]
    n_pages = page_table.shape[1]
    npg = DEC_PAGES if n_pages % DEC_PAGES == 0 else 1
    nrow = 2 * DA_HEADS
    rows16 = lambda x: jnp.repeat(x.reshape(b, DA_HEADS, DA_VDIM), 2, axis=1)
    row = pl.BlockSpec((1, nrow, DA_VDIM), lambda bi, j, pt: (bi, 0, 0))
    pages = [pl.BlockSpec((1, PAGE_SIZE, DA_HEADS, DA_VDIM),
                          lambda bi, j, pt, i=i: (pt[bi, j * npg + i], 0, 0, 0)) for i in range(npg)]
    vec = lambda x: x.reshape(1, -1)
    small = lambda n: pl.BlockSpec((1, n), lambda bi, j, pt: (0, 0))
    o = pl.pallas_call(
        functools.partial(_decode_body, lam_init, npg),
        grid_spec=pltpu.PrefetchScalarGridSpec(
            num_scalar_prefetch=1, grid=(b, n_pages // npg),
            in_specs=[row, row, row] + pages + pages + [small(DA_HEAD)] * 4 + [small(DA_VDIM)],
            out_specs=row,
            scratch_shapes=[pltpu.VMEM((nrow, DA_VDIM), F32), pltpu.VMEM((nrow, 1), F32),
                            pltpu.VMEM((nrow, 1), F32), pltpu.VMEM((nrow, DA_VDIM), F32)]),
        out_shape=jax.ShapeDtypeStruct((b, nrow, DA_VDIM), F32),
        compiler_params=_cparams("parallel", "arbitrary"))(
            page_table, rows16(q), rows16(k_new), rows16(v_new), *([cache_k] * npg), *([cache_v] * npg),
            vec(p['da_lq1']), vec(p['da_lk1']), vec(p['da_lq2']), vec(p['da_lk2']), vec(p['da_subln_w']))
    return o[:, ::2, :].reshape(b, DA_WIDTH)


def _cf_epilogue(ys, rows, fulls):
    y = ys[0]
    u = y[:, :CF_WIDTH] * _sigmoid(y[:, CF_WIDTH:2 * CF_WIDTH])
    return u, y[:, 2 * CF_WIDTH:]


def _cf_conv_body(tc, u_ref, up_ref, cw_ref, cb_ref, lw_ref, lb_ref, o_ref, buf, sh):
    c = pl.program_id(1)
    buf[0:CF_HIST, :] = jnp.where(c == 0, 0.0, up_ref[0])
    buf[CF_HIST:CF_HIST + tc, :] = u_ref[0]
    off = CF_HIST - (CF_KERNEL - 1)
    span = tc + CF_HIST - SUBLANES
    for r in range(1, SUBLANES):
        sh[r - 1, 0:span, :] = buf[r:r + span, :]
    rb = CF_ROWS
    for i in range(tc // rb):
        acc = cb_ref[...]
        for j in range(CF_KERNEL):
            r, base = (off + j) % SUBLANES, (off + j) // SUBLANES * SUBLANES
            rows = slice(base + i * rb, base + (i + 1) * rb)
            acc = acc + cw_ref[j:j + 1, :] * (buf[rows, :] if r == 0 else sh[r - 1, rows, :])
        mu = jnp.mean(acc, axis=-1, keepdims=True)
        xc = acc - mu
        var = jnp.mean(xc * xc, axis=-1, keepdims=True)
        o_ref[0, i * rb:(i + 1) * rb, :] = _silu(xc * lax.rsqrt(var + CF_LN_EPS) * lw_ref[...] + lb_ref[...])


def _cf_conv(u, b, t, p):
    tc = min(t, 512)
    per = tc // CF_HIST
    blk = pl.BlockSpec((1, tc, CF_WIDTH), lambda bi, ci: (bi, ci, 0))
    prev = pl.BlockSpec((1, CF_HIST, CF_WIDTH), lambda bi, ci: (bi, jnp.maximum(ci * per - 1, 0), 0))
    vec = lambda x: x.reshape(1, CF_WIDTH)
    u3 = u.reshape(b, t, CF_WIDTH)
    o = pl.pallas_call(
        functools.partial(_cf_conv_body, tc),
        grid=(b, t // tc),
        in_specs=[blk, prev, _full((CF_KERNEL, CF_WIDTH))] + [_full((1, CF_WIDTH))] * 3,
        out_specs=blk,
        out_shape=jax.ShapeDtypeStruct((b, t, CF_WIDTH), F32),
        scratch_shapes=[pltpu.VMEM((CF_HIST + tc, CF_WIDTH), F32),
                        pltpu.VMEM((SUBLANES - 1, CF_HIST + tc - SUBLANES, CF_WIDTH), F32)],
        compiler_params=_cparams("parallel", "parallel"))(
            u3, u3, p['cf_conv_w'], vec(p['cf_conv_b']), vec(p['cf_ln_w']), vec(p['cf_ln_b']))
    return o.reshape(b * t, CF_WIDTH)


def _cf_step_body(u_ref, st_ref, cw_ref, cb_ref, lw_ref, lb_ref, o_ref):
    nb = u_ref.shape[0]
    cw = cw_ref[...]
    hist = jnp.concatenate([jnp.sum(st_ref[i] * cw[:CF_KERNEL - 1, :], axis=0, keepdims=True)
                            for i in range(nb)], axis=0)
    acc = cb_ref[...] + hist + cw[CF_KERNEL - 1:CF_KERNEL, :] * u_ref[...]
    mu = jnp.mean(acc, axis=-1, keepdims=True)
    xc = acc - mu
    var = jnp.mean(xc * xc, axis=-1, keepdims=True)
    o_ref[...] = _silu(xc * lax.rsqrt(var + CF_LN_EPS) * lw_ref[...] + lb_ref[...])


def _cf_step(u, state, p):
    b = u.shape[0]
    tb = SUBLANES
    vec = lambda x: x.reshape(1, CF_WIDTH)
    return pl.pallas_call(
        _cf_step_body,
        grid=(b // tb,),
        in_specs=[pl.BlockSpec((tb, CF_WIDTH), lambda i: (i, 0)),
                  pl.BlockSpec((tb, CF_KERNEL - 1, CF_WIDTH), lambda i: (i, 0, 0)),
                  _full((CF_KERNEL, CF_WIDTH))] + [_full((1, CF_WIDTH))] * 3,
        out_specs=pl.BlockSpec((tb, CF_WIDTH), lambda i: (i, 0)),
        out_shape=jax.ShapeDtypeStruct((b, CF_WIDTH), F32),
        compiler_params=_cparams("parallel"))(
            u, state, p['cf_conv_w'], vec(p['cf_conv_b']), vec(p['cf_ln_w']), vec(p['cf_ln_b']))


def _forward(x_prompt, x_sample, state_rwkv_shift, state_rwkv_wkv, state_mamba_conv, state_mamba_ssm,
             cache_k, cache_v, page_table, state_conformer_conv, p):
    bp, tp, d = x_prompt.shape
    bs = x_sample.shape[0]
    hp = x_prompt.reshape(bp * tp, d)
    hs = x_sample.reshape(bs, d)
    npre, npost = p['norm_pre'], p['norm_post']

    (r, k, v, g, a, lw), rw_shift_p = _rwkv_proj_seq(hp, bp, tp, npre[0], p)
    o, rw_wkv_p = _rwkv_chunk(r, k, v, a, lw, bp, tp, p)
    hp = _out_proj(o, g, hp, p['rw_w_o'], npost[0])
    (r, k, v, g, a, lw), rw_shift_s = _rwkv_proj_step(hs, state_rwkv_shift, npre[0], p)
    o, rw_wkv_s = _rwkv_step(r, k, v, a, lw, state_rwkv_wkv, p)
    hs = _out_proj(o, g, hs, p['rw_w_o'], npost[0])

    mw = _mamba_weights(p)
    mv = _mamba_vecs(p)
    widths = (MB_INNER, MB_CONV_DIM, LANES)
    dtb = (mv['dt_bias'],)
    zs, xbc, dt = _in_proj(hp, npre[1], mw, _mamba_epilogue, widths, full_extras=dtb, tm=256)
    yg, mb_ssm_p = _mamba_chunk(zs, xbc, dt, bp, tp, mv)
    mb_conv_p = xbc.reshape(bp, tp, MB_CONV_DIM)[:, tp - (MB_CONV - 1):]
    hp = _out_proj(yg, None, hp, p['mb_w_out'], npost[1])
    zs, xbc, dt = _in_proj(hs, npre[1], mw, _mamba_epilogue, widths, full_extras=dtb)
    yg, mb_ssm_s = _mamba_step(zs, xbc, dt, state_mamba_conv, state_mamba_ssm, mv)
    mb_conv_s = jnp.concatenate([state_mamba_conv[:, 1:], xbc[:, None, :]], axis=1)
    hs = _out_proj(yg, None, hs, p['mb_w_out'], npost[1])

    lam_init = 0.8 - 0.6 * math.exp(-0.3 * 2)
    tabs_p = _rope_tables(jnp.arange(tp, dtype=jnp.int32))
    tm = min(_row_tile(tp), 256)
    tps = tp // tm
    rows_p = [(tb, lambda i: (i % tps, 0), LANES) for tb in tabs_p]
    widths = ((DA_WIDTH, F32, 'heads'), (DA_WIDTH, F32, 'heads'), DA_WIDTH,
              (DA_WIDTH, BF16, True), (DA_WIDTH, BF16, False), (DA_WIDTH, BF16, True))
    k, v, g, qt, kb, vt = _in_proj(hp, npre[2], [p['da_w_qkvg']], _da_epilogue_seq, widths,
                                   row_extras=rows_p, tm=tm)
    k_rows_p = k.reshape(bp, tp, DA_HEADS, 2 * DA_HEAD)
    v_rows_p = v.reshape(bp, tp, DA_HEADS, DA_VDIM)
    o = _flash(qt, kb, vt, bp, tp, p, lam_init)
    hp = _out_proj(o, g, hp, p['da_w_o'], npost[2])
    tabs_s = _rope_tables(jnp.full((bs,), PAST_LEN, dtype=jnp.int32))
    rows_s = [(tb, lambda i: (i, 0), LANES) for tb in tabs_s]
    q, k, v, g = _in_proj(hs, npre[2], [p['da_w_qkvg']], _da_epilogue, (DA_WIDTH,) * 4, row_extras=rows_s)
    k_rows_s = k.reshape(bs, 1, DA_HEADS, 2 * DA_HEAD)
    v_rows_s = v.reshape(bs, 1, DA_HEADS, DA_VDIM)
    o = _decode(q, k, v, cache_k, cache_v, page_table, p, lam_init)
    hs = _out_proj(o, g, hs, p['da_w_o'], npost[2])

    u, g = _in_proj(hp, npre[3], [p['cf_w_in']], _cf_epilogue, (CF_WIDTH,) * 2)
    cf_conv_p = u.reshape(bp, tp, CF_WIDTH)[:, tp - (CF_KERNEL - 1):]
    c = _cf_conv(u, bp, tp, p)
    hp = _out_proj(c, g, hp, p['cf_w_out'], npost[3])
    u, g = _in_proj(hs, npre[3], [p['cf_w_in']], _cf_epilogue, (CF_WIDTH,) * 2)
    cf_conv_s = jnp.concatenate([state_conformer_conv[:, 1:], u[:, None, :]], axis=1)
    c = _cf_step(u, state_conformer_conv, p)
    hs = _out_proj(c, g, hs, p['cf_w_out'], npost[3])

    return (hp.reshape(bp, tp, d), hs.reshape(bs, 1, d), rw_shift_p, rw_shift_s, rw_wkv_p, rw_wkv_s,
            mb_conv_p, mb_conv_s, mb_ssm_p, mb_ssm_s, k_rows_p, k_rows_s, v_rows_p, v_rows_s,
            cf_conv_p, cf_conv_s)


def kernel(x_prompt, x_sample, state_rwkv_shift, state_rwkv_wkv, state_mamba_conv, state_mamba_ssm, cache_k, cache_v, page_table, state_conformer_conv, norm_pre, norm_post, rw_mu, rw_w_r, rw_w_k, rw_w_v, rw_w_g, rw_w0, rw_w_w1, rw_w_w2, rw_a0, rw_w_a1, rw_w_a2, rw_k_k, rw_k_a, rw_r_k, rw_gn_w, rw_gn_b, rw_w_o, mb_w_in, mb_conv_w, mb_conv_b, mb_dt_bias, mb_a_log, mb_d, mb_norm_w, mb_w_out, da_w_qkvg, da_lq1, da_lk1, da_lq2, da_lk2, da_subln_w, da_w_o, cf_w_in, cf_conv_w, cf_conv_b, cf_ln_w, cf_ln_b, cf_w_out):
    p = dict(norm_pre=norm_pre, norm_post=norm_post, rw_mu=rw_mu, rw_w_r=rw_w_r, rw_w_k=rw_w_k, rw_w_v=rw_w_v,
             rw_w_g=rw_w_g, rw_w0=rw_w0, rw_w_w1=rw_w_w1, rw_w_w2=rw_w_w2, rw_a0=rw_a0, rw_w_a1=rw_w_a1,
             rw_w_a2=rw_w_a2, rw_k_k=rw_k_k, rw_k_a=rw_k_a, rw_r_k=rw_r_k, rw_gn_w=rw_gn_w, rw_gn_b=rw_gn_b,
             rw_w_o=rw_w_o, mb_w_in=mb_w_in, mb_conv_w=mb_conv_w, mb_conv_b=mb_conv_b, mb_dt_bias=mb_dt_bias,
             mb_a_log=mb_a_log, mb_d=mb_d, mb_norm_w=mb_norm_w, mb_w_out=mb_w_out, da_w_qkvg=da_w_qkvg,
             da_lq1=da_lq1, da_lk1=da_lk1, da_lq2=da_lq2, da_lk2=da_lk2, da_subln_w=da_subln_w, da_w_o=da_w_o,
             cf_w_in=cf_w_in, cf_conv_w=cf_conv_w, cf_conv_b=cf_conv_b, cf_ln_w=cf_ln_w, cf_ln_b=cf_ln_b,
             cf_w_out=cf_w_out)
    return _forward(x_prompt, x_sample, state_rwkv_shift, state_rwkv_wkv, state_mamba_conv, state_mamba_ssm,
                    cache_k, cache_v, page_table, state_conformer_conv, p)
```

```python
import functools
import math

import jax
import jax.numpy as jnp
from jax import lax
from jax.experimental import pallas as pl
from jax.experimental.pallas import tpu as pltpu

F32 = jnp.float32
BF16 = jnp.bfloat16

D_MODEL = 1024
PAST_LEN = 8192
PAGE_SIZE = 128
RMS_EPS = 1e-6

RW_HEAD = 64
RW_HEADS = D_MODEL // RW_HEAD
RW_LORA = 64
RW_GN_EPS = 64e-5
RW_CHUNK = 64
RW_GROUP = 8
RW_PAIRS = 4

MB_INNER = 2 * D_MODEL
MB_HEADDIM = 64
MB_HEADS = MB_INNER // MB_HEADDIM
MB_GROUPS = 4
MB_STATE = 128
MB_CONV = 4
MB_CHUNK = 128
MB_CONV_DIM = MB_INNER + 2 * MB_GROUPS * MB_STATE
MB_NORM_EPS = 1e-5
MB_GROUP_WIDTH = MB_INNER // MB_GROUPS

DA_HEADS = 8
DA_HEAD = 64
DA_VDIM = 2 * DA_HEAD
DA_WIDTH = DA_HEADS * DA_VDIM
DA_SCALE = DA_HEAD ** -0.5
LOG2_E = math.log2(math.e)
DA_HEADS_PER_STEP = 4
DA_SUBLN_EPS = 1e-5
ROPE_THETA = 500000.0
ROPE_DIM = DA_HEAD // 4
DEC_PAGES = 16

CF_WIDTH = D_MODEL
CF_KERNEL = 31
CF_LN_EPS = 1e-5
CF_HIST = 32
CF_ROWS = 32

LANES = 128
SUBLANES = 8
VMEM_LIMIT = 56 * 1024 * 1024

NEG_INF = float("-inf")


def _cparams(*sem):
    return pltpu.CompilerParams(dimension_semantics=sem, vmem_limit_bytes=VMEM_LIMIT)


def _dot(a, b):
    return jnp.dot(a.astype(BF16), b.astype(BF16), preferred_element_type=F32)


def _dot_nt(a, b):
    return lax.dot_general(a.astype(BF16), b.astype(BF16), (((1,), (1,)), ((), ())),
                           preferred_element_type=F32)


def _dot_tn(a, b):
    return lax.dot_general(a.astype(BF16), b.astype(BF16), (((0,), (0,)), ((), ())),
                           preferred_element_type=F32)


def _split3(a):
    hi = a.astype(BF16)
    r1 = a - hi.astype(F32)
    mid = r1.astype(BF16)
    lo = (r1 - mid.astype(F32)).astype(BF16)
    return hi, mid, lo


def _cumsum_rows(tri_bf16, x):
    hi, mid, lo = _split3(x)
    f = functools.partial(jnp.dot, preferred_element_type=F32)
    return f(tri_bf16, hi) + (f(tri_bf16, mid) + f(tri_bf16, lo))


def _sigmoid(x):
    return 1.0 / (1.0 + jnp.exp(-x))


def _silu(x):
    return x * _sigmoid(x)


def _softplus(x):
    return jnp.maximum(x, 0.0) + jnp.log(1.0 + jnp.exp(-jnp.abs(x)))


def _rms(x, w, eps):
    return x * lax.rsqrt(jnp.mean(x * x, axis=-1, keepdims=True) + eps) * w


def _row_tile(m):
    for t in (512, 256, 128, 64, 32, 16, 8):
        if m % t == 0:
            return t
    return m


def _full(shape):
    nd = len(shape)
    return pl.BlockSpec(shape, lambda *_: (0,) * nd)


def _out_proj_body(has_gate, *refs):
    if has_gate:
        y_ref, g_ref, h_ref, w_ref, pw_ref, o_ref = refs
        y = y_ref[...] * _silu(g_ref[...])
    else:
        y_ref, h_ref, w_ref, pw_ref, o_ref = refs
        y = y_ref[...]
    z = _dot(y, w_ref[...])
    o_ref[...] = h_ref[...] + _rms(z, pw_ref[...], RMS_EPS)


def _out_proj(y, gate, h, w, post_w):
    m, kdim = y.shape
    tm = _row_tile(m)
    row = lambda width: pl.BlockSpec((tm, width), lambda i: (i, 0))
    args, specs = [y], [row(kdim)]
    if gate is not None:
        args.append(gate)
        specs.append(row(D_MODEL))
    args += [h, w.astype(BF16), post_w.reshape(1, D_MODEL)]
    specs += [row(D_MODEL), _full((kdim, D_MODEL)), _full((1, D_MODEL))]
    return pl.pallas_call(
        functools.partial(_out_proj_body, gate is not None),
        grid=(m // tm,), in_specs=specs, out_specs=row(D_MODEL),
        out_shape=jax.ShapeDtypeStruct((m, D_MODEL), F32),
        compiler_params=_cparams("parallel"))(*args)


def _in_proj_body(n_w, n_row, n_full, epilogue, *refs):
    h_ref, nw_ref = refs[0], refs[1]
    w_refs = refs[2:2 + n_w]
    row_refs = refs[2 + n_w:2 + n_w + n_row]
    full_refs = refs[2 + n_w + n_row:2 + n_w + n_row + n_full]
    out_refs = refs[2 + n_w + n_row + n_full:]
    xn = _rms(h_ref[...], nw_ref[...], RMS_EPS).astype(BF16)
    ys = [jnp.dot(xn, w[...], preferred_element_type=F32) for w in w_refs]
    outs = epilogue(ys, [r[...] for r in row_refs], [r[...] for r in full_refs])
    for o_ref, o in zip(out_refs, outs):
        o_ref[...] = o.astype(o_ref.dtype).reshape(o_ref.shape)


def _in_proj(h, norm_w, weights, epilogue, out_widths, row_extras=(), full_extras=(), tm=None):
    m = h.shape[0]
    tm = tm or _row_tile(m)
    row = lambda width: pl.BlockSpec((tm, width), lambda i: (i, 0))
    args = [h, norm_w.reshape(1, D_MODEL)] + [w.astype(BF16) for w in weights]
    specs = [row(D_MODEL), _full((1, D_MODEL))] + [_full(w.shape) for w in weights]
    for arr, imap, width in row_extras:
        args.append(arr)
        specs.append(pl.BlockSpec((tm, width), imap))
    for arr in full_extras:
        args.append(arr)
        specs.append(_full(arr.shape))
    out_specs, out_shape = [], []
    for wd in out_widths:
        width, dtype, transposed = wd if isinstance(wd, tuple) else (wd, F32, False)
        if transposed == 'heads':
            shape3 = (width // LANES, LANES)
            out_specs.append(pl.BlockSpec((tm,) + shape3, lambda i: (i, 0, 0)))
            out_shape.append(jax.ShapeDtypeStruct((m,) + shape3, dtype))
        elif transposed:
            out_specs.append(pl.BlockSpec((width, tm), lambda i: (0, i)))
            out_shape.append(jax.ShapeDtypeStruct((width, m), dtype))
        else:
            out_specs.append(row(width))
            out_shape.append(jax.ShapeDtypeStruct((m, width), dtype))
    return pl.pallas_call(
        functools.partial(_in_proj_body, len(weights), len(row_extras), len(full_extras), epilogue),
        grid=(m // tm,), in_specs=specs, out_specs=out_specs, out_shape=out_shape,
        compiler_params=_cparams("parallel"))(*args)


def _rwkv_mix(xn, xprev, mu, wr, wk, wv, wg, w0, ww1, ww2, a0, wa1, wa2):
    xx = xprev - xn
    xm = [xn + xx * mu[i:i + 1, :] for i in range(6)]
    r = _dot(xm[0], wr)
    k = _dot(xm[2], wk)
    v = _dot(xm[3], wv)
    g = _dot(xm[5], wg)
    zw = w0 + _dot(jnp.tanh(_dot(xm[1], ww1)), ww2)
    w_log = -_softplus(-zw) - 0.5
    lw = -jnp.exp(w_log)
    a = _sigmoid(a0 + _dot(_dot(xm[4], wa1), wa2))
    return r, k, v, g, a, lw


def _rwkv_proj_seq_body(tiles_per_seq, h_ref, hp_ref, nw_ref, mu_ref, wr, wk, wv, wg, w0, ww1, ww2,
                        a0, wa1, wa2, r_ref, k_ref, v_ref, g_ref, a_ref, lw_ref, last_ref):
    i = pl.program_id(0)
    nw = nw_ref[...]
    xn = _rms(h_ref[...], nw, RMS_EPS)
    tm = xn.shape[0]
    pn = _rms(hp_ref[...], nw, RMS_EPS)[SUBLANES - 1:SUBLANES, :]
    first = (i % tiles_per_seq) == 0
    prow = jnp.where(first, 0.0, pn)
    rows = lax.broadcasted_iota(jnp.int32, xn.shape, 0)
    xprev = jnp.where(rows == 0, prow, pltpu.roll(xn, 1, axis=0))
    outs = _rwkv_mix(xn, xprev, mu_ref[...], wr[...], wk[...], wv[...], wg[...], w0[...], ww1[...],
                     ww2[...], a0[...], wa1[...], wa2[...])
    for o_ref, o in zip((r_ref, k_ref, v_ref, g_ref, a_ref, lw_ref), outs):
        o_ref[...] = o
    last_ref[0] = xn[tm - 1:tm, :]


def _rwkv_proj_step_body(h_ref, sh_ref, nw_ref, mu_ref, wr, wk, wv, wg, w0, ww1, ww2,
                         a0, wa1, wa2, r_ref, k_ref, v_ref, g_ref, a_ref, lw_ref, last_ref):
    xn = _rms(h_ref[...], nw_ref[...], RMS_EPS)
    outs = _rwkv_mix(xn, sh_ref[...], mu_ref[...], wr[...], wk[...], wv[...], wg[...], w0[...], ww1[...],
                     ww2[...], a0[...], wa1[...], wa2[...])
    for o_ref, o in zip((r_ref, k_ref, v_ref, g_ref, a_ref, lw_ref), outs):
        o_ref[...] = o
    last_ref[...] = xn


def _rwkv_weights(p):
    d = D_MODEL
    vec = lambda x: x.reshape(1, d)
    args = [p['rw_mu'], p['rw_w_r'].astype(BF16), p['rw_w_k'].astype(BF16), p['rw_w_v'].astype(BF16),
            p['rw_w_g'].astype(BF16), vec(p['rw_w0']), p['rw_w_w1'].astype(BF16), p['rw_w_w2'].astype(BF16),
            vec(p['rw_a0']), p['rw_w_a1'].astype(BF16), p['rw_w_a2'].astype(BF16)]
    return args, [_full(a.shape) for a in args]


def _rwkv_proj_seq(h2, b, t, norm_w, p):
    m = b * t
    tm = min(_row_tile(t), 256)
    tps = t // tm
    row = pl.BlockSpec((tm, D_MODEL), lambda i: (i, 0))
    prev = pl.BlockSpec((SUBLANES, D_MODEL), lambda i: (jnp.maximum(i * (tm // SUBLANES) - 1, 0), 0))
    wargs, wspecs = _rwkv_weights(p)
    outs = pl.pallas_call(
        functools.partial(_rwkv_proj_seq_body, tps),
        grid=(m // tm,),
        in_specs=[row, prev, _full((1, D_MODEL))] + wspecs,
        out_specs=[row] * 6 + [pl.BlockSpec((1, 1, D_MODEL), lambda i: (i // tps, 0, 0))],
        out_shape=[jax.ShapeDtypeStruct((m, D_MODEL), F32)] * 6
        + [jax.ShapeDtypeStruct((b, 1, D_MODEL), F32)],
        compiler_params=_cparams("arbitrary"))(h2, h2, norm_w.reshape(1, D_MODEL), *wargs)
    return outs[:6], outs[6].reshape(b, D_MODEL)


def _rwkv_proj_step(h2, shift, norm_w, p):
    m = h2.shape[0]
    wargs, wspecs = _rwkv_weights(p)
    outs = pl.pallas_call(
        _rwkv_proj_step_body,
        grid=(1,),
        in_specs=[_full((m, D_MODEL)), _full((m, D_MODEL)), _full((1, D_MODEL))] + wspecs,
        out_specs=[_full((m, D_MODEL))] * 7,
        out_shape=[jax.ShapeDtypeStruct((m, D_MODEL), F32)] * 7,
        compiler_params=_cparams("arbitrary"))(h2, shift, norm_w.reshape(1, D_MODEL), *wargs)
    return outs[:6], outs[6]


def _pair_sum(x, m0):
    s0 = jnp.sum(jnp.where(m0, x, 0.0), axis=-1, keepdims=True)
    s1 = jnp.sum(jnp.where(m0, 0.0, x), axis=-1, keepdims=True)
    return jnp.where(m0, s0, s1)


def _rwkv_chunk_body(gsz, npair, r_ref, k_ref, v_ref, a_ref, lw_ref, kk_ref, ka_ref, rk_ref, gw_ref, gb_ref,
                     o_ref, s_ref, bd_ref):
    cg = pl.program_id(2)
    ncg = pl.num_programs(2)
    cl = RW_CHUNK
    two = 2 * cl

    @pl.when(cg == 0)
    def _():
        bd_ref[...] = jnp.zeros_like(bd_ref)

    lane = lax.broadcasted_iota(jnp.int32, (cl, LANES), 1)
    m0 = lane < RW_HEAD
    ti = lax.broadcasted_iota(jnp.int32, (cl, cl), 0)
    tj = lax.broadcasted_iota(jnp.int32, (cl, cl), 1)
    tri = jnp.where(ti >= tj, 1.0, 0.0).astype(BF16)
    row2 = lax.broadcasted_iota(jnp.int32, (two, LANES), 0)
    col2 = lax.broadcasted_iota(jnp.int32, (two, LANES), 1)
    same = (row2 >= cl) == (col2 >= RW_HEAD)
    step_r = jnp.bitwise_and(row2, cl - 1)
    step_c = jnp.bitwise_and(col2, cl - 1)
    bd_strict = jnp.logical_and(same, step_r > step_c)
    bd_incl = jnp.logical_and(same, step_r >= step_c)
    eye = row2 == col2

    def stack(x):
        return jnp.where(same, jnp.concatenate([x, x], axis=0), 0.0)

    chunks = range(gsz)
    st = []
    for pi in range(npair):
        cols = slice(pi * LANES, (pi + 1) * LANES)
        kk_w, ka_w, rk_w = kk_ref[:, cols], ka_ref[:, cols], rk_ref[:, cols]
        for gi in chunks:
            sl = slice(gi * cl, (gi + 1) * cl)
            r = r_ref[0, sl, cols]
            k = k_ref[0, sl, cols]
            v = v_ref[0, sl, cols]
            a = a_ref[0, sl, cols]
            lw = lw_ref[0, sl, cols]
            kkv = k * kk_w
            kap = kkv * lax.rsqrt(jnp.maximum(_pair_sum(kkv * kkv, m0), 1e-24))
            kmod = k * (1.0 + (a - 1.0) * ka_w)
            st.append(dict(r=r, v=v, lw=lw, kap=kap, kmod=kmod, bvec=kap * a,
                           bonus=_pair_sum(r * kmod * rk_w, m0) * v))
    for s in st:
        s['cum'] = _cumsum_rows(tri, s['lw'])
    for s in st:
        cum = s['cum']
        cum_last = cum[cl - 1:cl, :]
        g_inv = jnp.exp(-cum)
        g_end = jnp.exp(cum_last - cum)
        s['g_all'] = jnp.exp(cum_last)
        s['kapm'] = stack(s['kap'] * jnp.exp(cum - s['lw']))
        s['rm'] = stack(s['r'] * jnp.exp(cum))
        s['v2m'] = stack(s['v'])
        s['b_h'] = s['bvec'] * g_inv
        s['k_h'] = s['kmod'] * g_inv
        s['b_e'] = stack(s['bvec'] * g_end)
        s['k_e'] = stack(s['kmod'] * g_end)
    for s in st:
        lhs = jnp.concatenate([s['kapm'], s['rm']], axis=0)
        gb = _dot_nt(lhs, jnp.concatenate([s['b_h'], s['b_h']], axis=0))
        gk = _dot_nt(lhs, jnp.concatenate([s['k_h'], s['k_h']], axis=0))
        s['a_ab'] = jnp.where(bd_strict, gb[:two], 0.0)
        s['a_rb'] = jnp.where(bd_incl, gb[two:], 0.0)
        s['a_ak'] = jnp.where(bd_strict, gk[:two], 0.0)
        s['a_rk'] = jnp.where(bd_incl, gk[two:], 0.0)
    for s in st:
        s['tinv'] = jnp.where(eye, 1.0, 0.0) - s['a_ab']
        s['lp'] = _dot(s['a_ab'], s['a_ab'])
        s['x2m'] = _dot(s['a_ak'], s['v2m'])
    for _ in range(4):
        for s in st:
            both = _dot(jnp.concatenate([s['tinv'], s['lp']], axis=0), s['lp'])
            s['tinv'] = s['tinv'] + both[:two]
            s['lp'] = both[two:]
    for s in st:
        s['tinv'] = s['tinv'] + _dot(s['tinv'], s['lp'])
    for s in st:
        pq = _dot(s['tinv'], jnp.concatenate([s['kapm'], s['x2m']], axis=1))
        s['pm'] = -pq[:, :LANES]
        s['qm'] = -pq[:, LANES:]
    for s in st:
        ro = _dot(s['a_rb'], jnp.concatenate([s['pm'], s['qm']], axis=1))
        rtm = s['rm'] + ro[:, :LANES]
        oim = ro[:, LANES:] + _dot(s['a_rk'], s['v2m'])
        s['rt'] = rtm[:cl] + rtm[cl:]
        s['oi'] = oim[:cl] + oim[cl:]
    for s in st:
        s['corr'] = _dot_tn(s['pm'], s['b_e'])
        s['nmat'] = _dot_tn(jnp.concatenate([s['qm'], s['v2m']], axis=0),
                            jnp.concatenate([s['b_e'], s['k_e']], axis=0))
    bds = [bd_ref[pi] for pi in range(npair)]
    for gi in chunks:
        for pi in range(npair):
            s = st[pi * gsz + gi]
            cols = slice(pi * LANES, (pi + 1) * LANES)
            o = _dot_nt(s['rt'], bds[pi]) + s['oi']
            bds[pi] = bds[pi] * s['g_all'] + _dot(bds[pi], s['corr']) + s['nmat']
            mean = _pair_sum(o, m0) * (1.0 / RW_HEAD)
            oc = o - mean
            var = _pair_sum(oc * oc, m0) * (1.0 / RW_HEAD)
            o_ref[0, gi * cl:(gi + 1) * cl, cols] = (
                oc * lax.rsqrt(var + RW_GN_EPS) * gw_ref[:, cols] + gb_ref[:, cols] + s['bonus'])
    for pi in range(npair):
        bd_ref[pi] = bds[pi]

    @pl.when(cg == ncg - 1)
    def _():
        for pi in range(npair):
            s_ref[0, 2 * pi] = bds[pi][:RW_HEAD, :RW_HEAD]
            s_ref[0, 2 * pi + 1] = bds[pi][RW_HEAD:, RW_HEAD:]


def _rwkv_chunk(r, k, v, a, lw, b, t, p):
    npair = RW_PAIRS
    hp = RW_HEADS // (2 * npair)
    gsz = RW_GROUP if t % (RW_GROUP * RW_CHUNK) == 0 else 1
    rows = gsz * RW_CHUNK
    shp = lambda x: x.reshape(b, t, D_MODEL)
    blk = pl.BlockSpec((1, rows, npair * LANES), lambda bi, hi, ci: (bi, ci, hi))
    vec = pl.BlockSpec((1, npair * LANES), lambda bi, hi, ci: (0, hi))
    vrow = lambda x: x.reshape(1, D_MODEL)
    o, s = pl.pallas_call(
        functools.partial(_rwkv_chunk_body, gsz, npair),
        grid=(b, hp, t // rows),
        in_specs=[blk] * 5 + [vec] * 5,
        out_specs=[blk, pl.BlockSpec((1, 2 * npair, RW_HEAD, RW_HEAD), lambda bi, hi, ci: (bi, hi, 0, 0))],
        out_shape=[jax.ShapeDtypeStruct((b, t, D_MODEL), F32),
                   jax.ShapeDtypeStruct((b, RW_HEADS, RW_HEAD, RW_HEAD), F32)],
        scratch_shapes=[pltpu.VMEM((npair, LANES, LANES), F32)],
        compiler_params=_cparams("parallel", "parallel", "arbitrary"))(
            shp(r), shp(k), shp(v), shp(a), shp(lw), vrow(p['rw_k_k']), vrow(p['rw_k_a']),
            vrow(p['rw_r_k']), vrow(p['rw_gn_w']), vrow(p['rw_gn_b']))
    return o.reshape(b * t, D_MODEL), s


def _rwkv_step_body(r_ref, k_ref, vc_ref, a_ref, lw_ref, s_ref, kk_ref, ka_ref, rk_ref, gw_ref, gb_ref,
                    o_ref, so_ref):
    r = r_ref[0]
    k = k_ref[0]
    a = a_ref[0]
    lw = lw_ref[0]
    vc = vc_ref[0]
    s = s_ref[0]
    kkv = k * kk_ref[...]
    kap = kkv * lax.rsqrt(jnp.maximum(jnp.sum(kkv * kkv, axis=-1, keepdims=True), 1e-24))
    kmod = k * (1.0 + (a - 1.0) * ka_ref[...])
    bvec = kap * a
    dec = jnp.exp(lw)
    sa = jnp.sum(s * kap, axis=-1, keepdims=True)
    s_new = s * dec - sa * bvec + vc * kmod
    so_ref[0] = s_new
    o = jnp.sum(s_new * r, axis=-1, keepdims=True)
    mean = jnp.mean(o, axis=1, keepdims=True)
    oc = o - mean
    var = jnp.mean(oc * oc, axis=1, keepdims=True)
    on = oc * lax.rsqrt(var + RW_GN_EPS) * gw_ref[...] + gb_ref[...]
    bonus = jnp.sum(r * kmod * rk_ref[...], axis=-1, keepdims=True)
    o_ref[0] = on + bonus * vc


def _rwkv_step(r, k, v, a, lw, state, p):
    b = r.shape[0]
    hrow = (RW_HEADS, 1, RW_HEAD)
    hcol = (RW_HEADS, RW_HEAD, 1)
    rows = lambda x: x.reshape((b,) + hrow)
    rblk = pl.BlockSpec((1,) + hrow, lambda i: (i, 0, 0, 0))
    cblk = pl.BlockSpec((1,) + hcol, lambda i: (i, 0, 0, 0))
    sblk = pl.BlockSpec((1, RW_HEADS, RW_HEAD, RW_HEAD), lambda i: (i, 0, 0, 0))
    o, s = pl.pallas_call(
        _rwkv_step_body,
        grid=(b,),
        in_specs=[rblk, rblk, cblk, rblk, rblk, sblk] + [_full(hrow)] * 3 + [_full(hcol)] * 2,
        out_specs=[cblk, sblk],
        out_shape=[jax.ShapeDtypeStruct((b,) + hcol, F32),
                   jax.ShapeDtypeStruct((b, RW_HEADS, RW_HEAD, RW_HEAD), F32)],
        compiler_params=_cparams("parallel"))(
            rows(r), rows(k), v.reshape((b,) + hcol), rows(a), rows(lw), state,
            p['rw_k_k'].reshape(hrow), p['rw_k_a'].reshape(hrow), p['rw_r_k'].reshape(hrow),
            p['rw_gn_w'].reshape(hcol), p['rw_gn_b'].reshape(hcol))
    return o.reshape(b, D_MODEL), s


def _mamba_weights(p):
    w = p['mb_w_in']
    wz = w[:, :MB_INNER]
    wx = w[:, MB_INNER:MB_INNER + MB_CONV_DIM]
    wdt = jnp.pad(w[:, MB_INNER + MB_CONV_DIM:], ((0, 0), (0, LANES - MB_HEADS)))
    return [wz, wx, wdt]


def _mamba_vecs(p):
    pad = lambda x: jnp.pad(x.reshape(1, MB_HEADS), ((0, 0), (0, LANES - MB_HEADS)))
    return dict(conv_w=p['mb_conv_w'], conv_b=p['mb_conv_b'].reshape(1, MB_CONV_DIM),
                dt_bias=pad(p['mb_dt_bias']), a_log=pad(p['mb_a_log']),
                d_skip=jnp.repeat(p['mb_d'], MB_HEADDIM).reshape(1, MB_INNER),
                norm_w=p['mb_norm_w'].reshape(1, MB_INNER))


def _mamba_epilogue(ys, rows, fulls):
    z, xbc, dt_raw = ys
    return _silu(z), xbc, _softplus(dt_raw + fulls[0])


def _mamba_chunk_body(x_ref, xp_ref, z_ref, dt_ref, cw_ref, cb_ref, alog_ref, dsk_ref, nw_ref,
                      y_ref, fs_ref, st_ref):
    c = pl.program_id(1)
    nc = pl.num_programs(1)
    cl = MB_CHUNK

    @pl.when(c == 0)
    def _():
        st_ref[...] = jnp.zeros_like(st_ref)

    hist = jnp.where(c == 0, 0.0, xp_ref[0])
    xcat = jnp.concatenate([hist, x_ref[0]], axis=0)
    cw = cw_ref[...]
    conv = cb_ref[...] + cw[0:1, :] * xcat[5:5 + cl]
    for j in range(1, MB_CONV):
        conv = conv + cw[j:j + 1, :] * xcat[5 + j:5 + j + cl]
    conv = _silu(conv)

    dt = dt_ref[0]
    da = dt * (-jnp.exp(alog_ref[...]))
    ti = lax.broadcasted_iota(jnp.int32, (cl, cl), 0)
    tj = lax.broadcasted_iota(jnp.int32, (cl, cl), 1)
    causal = ti >= tj
    tri = jnp.where(causal, 1.0, 0.0).astype(BF16)
    a_cs = _cumsum_rows(tri, da)
    a_cs_t = a_cs.T
    a_last = a_cs[cl - 1:cl, :]
    dec_to_end = jnp.exp(a_last - a_cs)
    e_cs = jnp.exp(a_cs)
    e_last = jnp.exp(a_last)

    lane = lax.broadcasted_iota(jnp.int32, (cl, LANES), 1)
    m0 = lane < MB_HEADDIM
    heads_per_group = MB_HEADS // MB_GROUPS
    pairs_per_group = heads_per_group // 2
    for g in range(MB_GROUPS):
        bm = conv[:, MB_INNER + g * MB_STATE:MB_INNER + (g + 1) * MB_STATE]
        cm = conv[:, MB_INNER + (MB_GROUPS + g) * MB_STATE:MB_INNER + (MB_GROUPS + g + 1) * MB_STATE]
        cb = _dot_nt(cm, bm)
        bm_t = bm.T.astype(BF16)
        blocks = []
        for j in range(pairs_per_group):
            pi = g * pairs_per_group + j
            h0, h1 = 2 * pi, 2 * pi + 1
            lo, hi = pi * LANES, (pi + 1) * LANES
            xs = conv[:, lo:hi]
            dt2 = jnp.where(m0, dt[:, h0:h0 + 1], dt[:, h1:h1 + 1])
            xdt = xs * dt2
            l0 = jnp.exp(jnp.where(causal, a_cs[:, h0:h0 + 1] - a_cs_t[h0:h0 + 1, :], NEG_INF))
            l1 = jnp.exp(jnp.where(causal, a_cs[:, h1:h1 + 1] - a_cs_t[h1:h1 + 1, :], NEG_INF))
            lhs = jnp.concatenate([(cb * l0).astype(BF16), (cb * l1).astype(BF16)], axis=1)
            rhs = jnp.concatenate([jnp.where(m0, xdt, 0.0).astype(BF16),
                                   jnp.where(m0, 0.0, xdt).astype(BF16)], axis=0)
            y = jnp.dot(lhs, rhs, preferred_element_type=F32)
            st = st_ref[pi]
            e2 = jnp.where(m0, e_cs[:, h0:h0 + 1], e_cs[:, h1:h1 + 1])
            y = y + _dot(cm, st) * e2
            d2 = jnp.where(m0, dec_to_end[:, h0:h0 + 1], dec_to_end[:, h1:h1 + 1])
            sc = jnp.where(m0, e_last[:, h0:h0 + 1], e_last[:, h1:h1 + 1])
            st_ref[pi] = st * sc + jnp.dot(bm_t, (xdt * d2).astype(BF16), preferred_element_type=F32)
            y = y + dsk_ref[:, lo:hi] * xs
            blocks.append(y * z_ref[0, :, lo:hi])
        ssq = blocks[0] * blocks[0]
        for blk in blocks[1:]:
            ssq = ssq + blk * blk
        scale = lax.rsqrt(jnp.sum(ssq, axis=-1, keepdims=True) * (1.0 / MB_GROUP_WIDTH) + MB_NORM_EPS)
        for j, blk in enumerate(blocks):
            lo = (g * pairs_per_group + j) * LANES
            y_ref[0, :, lo:lo + LANES] = blk * scale * nw_ref[:, lo:lo + LANES]

    @pl.when(c == nc - 1)
    def _():
        for pi in range(MB_HEADS // 2):
            fs_ref[0, pi] = st_ref[pi].T


def _mamba_chunk(z, xbc, dtr, b, t, vecs):
    nc = t // MB_CHUNK
    npairs = MB_HEADS // 2
    per = MB_CHUNK // SUBLANES
    y, fs = pl.pallas_call(
        _mamba_chunk_body,
        grid=(b, nc),
        in_specs=[pl.BlockSpec((1, MB_CHUNK, MB_CONV_DIM), lambda bi, ci: (bi, ci, 0)),
                  pl.BlockSpec((1, SUBLANES, MB_CONV_DIM),
                               lambda bi, ci: (bi, jnp.maximum(ci * per - 1, 0), 0)),
                  pl.BlockSpec((1, MB_CHUNK, MB_INNER), lambda bi, ci: (bi, ci, 0)),
                  pl.BlockSpec((1, MB_CHUNK, LANES), lambda bi, ci: (bi, ci, 0)),
                  _full((MB_CONV, MB_CONV_DIM)), _full((1, MB_CONV_DIM)),
                  _full((1, LANES)), _full((1, MB_INNER)), _full((1, MB_INNER))],
        out_specs=[pl.BlockSpec((1, MB_CHUNK, MB_INNER), lambda bi, ci: (bi, ci, 0)),
                   pl.BlockSpec((1, npairs, LANES, MB_STATE), lambda bi, ci: (bi, 0, 0, 0))],
        out_shape=[jax.ShapeDtypeStruct((b, t, MB_INNER), F32),
                   jax.ShapeDtypeStruct((b, npairs, LANES, MB_STATE), F32)],
        scratch_shapes=[pltpu.VMEM((npairs, LANES, MB_STATE), F32)],
        compiler_params=_cparams("parallel", "arbitrary"))(
            xbc.reshape(b, t, MB_CONV_DIM), xbc.reshape(b, t, MB_CONV_DIM), z.reshape(b, t, MB_INNER),
            dtr.reshape(b, t, LANES), vecs['conv_w'], vecs['conv_b'], vecs['a_log'],
            vecs['d_skip'], vecs['norm_w'])
    return y.reshape(b * t, MB_INNER), fs.reshape(b, MB_HEADS, MB_HEADDIM, MB_STATE)


def _mamba_step_conv_body(x_ref, cs_ref, dt_ref, cw_ref, cb_ref, alog_ref, conv_ref, ed_o):
    cw = cw_ref[...]
    conv = cb_ref[...] + cw[MB_CONV - 1:MB_CONV, :] * x_ref[...]
    for j in range(MB_CONV - 1):
        conv = conv + cw[j:j + 1, :] * cs_ref[:, j, :]
    conv_ref[...] = _silu(conv)
    ed_o[...] = jnp.exp(dt_ref[...] * (-jnp.exp(alog_ref[...])))


def _mamba_step_state_body(s_ref, xc_ref, dt_ref, ed_ref, bm_ref, cm_ref, so_ref, y_ref):
    hpg = MB_HEADS // MB_GROUPS
    for g in range(MB_GROUPS):
        hs = slice(g * hpg, (g + 1) * hpg)
        s = s_ref[0, hs]
        xdt = xc_ref[0, hs] * dt_ref[0, hs]
        fin = s * ed_ref[0, hs] + xdt * bm_ref[0, g:g + 1]
        so_ref[0, hs] = fin
        y_ref[0, hs] = jnp.sum(fin * cm_ref[0, g:g + 1], axis=-1, keepdims=True)


def _mamba_step_out_body(y_ref, x_ref, z_ref, dsk_ref, nw_ref, o_ref):
    y = (y_ref[...] + dsk_ref[...] * x_ref[...]) * z_ref[...]
    for g in range(MB_GROUPS):
        lo, hi = g * MB_GROUP_WIDTH, (g + 1) * MB_GROUP_WIDTH
        o_ref[:, lo:hi] = _rms(y[:, lo:hi], nw_ref[:, lo:hi], MB_NORM_EPS)


def _mamba_step(z, xbc, dt, conv_state, ssm_state, vecs):
    b = z.shape[0]
    conv, ed = pl.pallas_call(
        _mamba_step_conv_body,
        grid=(1,),
        in_specs=[_full((b, MB_CONV_DIM)), _full((b, MB_CONV - 1, MB_CONV_DIM)), _full((b, LANES)),
                  _full((MB_CONV, MB_CONV_DIM)), _full((1, MB_CONV_DIM)), _full((1, LANES))],
        out_specs=[_full((b, MB_CONV_DIM)), _full((b, LANES))],
        out_shape=[jax.ShapeDtypeStruct((b, MB_CONV_DIM), F32), jax.ShapeDtypeStruct((b, LANES), F32)],
        compiler_params=_cparams("arbitrary"))(
            xbc, conv_state, dt, vecs['conv_w'], vecs['conv_b'], vecs['a_log'])
    xs = conv[:, :MB_INNER]
    bm = conv[:, MB_INNER:MB_INNER + MB_GROUPS * MB_STATE].reshape(b, MB_GROUPS, MB_STATE)
    cm = conv[:, MB_INNER + MB_GROUPS * MB_STATE:].reshape(b, MB_GROUPS, MB_STATE)
    hp1 = (MB_HEADS, MB_HEADDIM, 1)
    h11 = (MB_HEADS, 1, 1)
    per = lambda shape: pl.BlockSpec((1,) + shape, lambda i: (i,) + (0,) * len(shape))
    st_shape = (MB_HEADS, MB_HEADDIM, MB_STATE)
    fin, y = pl.pallas_call(
        _mamba_step_state_body,
        grid=(b,),
        in_specs=[per(st_shape), per(hp1), per(h11), per(h11), per((MB_GROUPS, MB_STATE)),
                  per((MB_GROUPS, MB_STATE))],
        out_specs=[per(st_shape), per(hp1)],
        out_shape=[jax.ShapeDtypeStruct((b,) + st_shape, F32), jax.ShapeDtypeStruct((b,) + hp1, F32)],
        compiler_params=_cparams("parallel"))(
            ssm_state, xs.reshape((b,) + hp1), dt[:, :MB_HEADS].reshape((b,) + h11),
            ed[:, :MB_HEADS].reshape((b,) + h11), bm, cm)
    yg = pl.pallas_call(
        _mamba_step_out_body,
        grid=(1,),
        in_specs=[_full((b, MB_INNER))] * 3 + [_full((1, MB_INNER))] * 2,
        out_specs=_full((b, MB_INNER)),
        out_shape=jax.ShapeDtypeStruct((b, MB_INNER), F32),
        compiler_params=_cparams("arbitrary"))(
            y.reshape(b, MB_INNER), xs, z, vecs['d_skip'], vecs['norm_w'])
    return yg, fin


def _rope_tables(pos):
    half = ROPE_DIM // 2
    inv_freq = ROPE_THETA ** (-jnp.arange(0, ROPE_DIM, 2, dtype=F32) / ROPE_DIM)
    ang = pos.astype(F32)[:, None] * inv_freq[None, :]
    cos, sin = jnp.cos(ang), jnp.sin(ang)
    n = pos.shape[0]
    ones = jnp.ones((n, DA_HEAD - ROPE_DIM), F32)
    zeros = jnp.zeros((n, DA_HEAD - ROPE_DIM), F32)
    zh = jnp.zeros((n, half), F32)
    ct = jnp.concatenate([cos, cos, ones], axis=1)
    s_up = jnp.concatenate([-sin, zh, zeros], axis=1)
    s_dn = jnp.concatenate([zh, sin, zeros], axis=1)
    two = lambda x: jnp.concatenate([x, x], axis=1)
    return two(ct), two(s_up), two(s_dn)


def _rope(x, ct, s_up, s_dn):
    half = ROPE_DIM // 2
    cols = []
    for j in range(x.shape[1] // LANES):
        xj = x[:, j * LANES:(j + 1) * LANES]
        cols.append(xj * ct + pltpu.roll(xj, LANES - half, axis=1) * s_up + pltpu.roll(xj, half, axis=1) * s_dn)
    return jnp.concatenate(cols, axis=1)


def _da_epilogue(ys, rows, fulls):
    y = ys[0]
    ct, s_up, s_dn = rows
    q = _rope(y[:, :DA_WIDTH], ct, s_up, s_dn)
    k = _rope(y[:, DA_WIDTH:2 * DA_WIDTH], ct, s_up, s_dn)
    return q, k, y[:, 2 * DA_WIDTH:3 * DA_WIDTH], y[:, 3 * DA_WIDTH:]


def _lambda(lq1, lk1, lq2, lk2, lam_init):
    return (jnp.exp(jnp.sum(lq1 * lk1, axis=-1, keepdims=True))
            - jnp.exp(jnp.sum(lq2 * lk2, axis=-1, keepdims=True)) + lam_init)


def _da_epilogue_seq(ys, rows, fulls):
    q, k, v, g = _da_epilogue(ys, rows, fulls)
    return k, v, g, (q * (DA_SCALE * LOG2_E)).T, k, v.T


def _flash_body(lam_init, nh, qi_ref, kj_ref, qt_ref, k_ref, vt_ref, lq1, lk1, lq2, lk2, sw_ref, o_ref,
                m_s, l_s, acc):
    step = pl.program_id(2)
    qi = qi_ref[step]
    kj = kj_ref[step]

    @pl.when(kj == 0)
    def _():
        m_s[...] = jnp.full_like(m_s, NEG_INF)
        l_s[...] = jnp.zeros_like(l_s)
        acc[...] = jnp.zeros_like(acc)

    def update(masked):
        tk, tq = k_ref.shape[0], qt_ref.shape[1]
        hq = tq // 2
        frow = lax.broadcasted_iota(jnp.int32, (DA_VDIM, hq), 0)
        zero = jnp.zeros((DA_VDIM, hq), BF16)
        items = []
        for hd in range(nh):
            sl = slice(hd * DA_VDIM, (hd + 1) * DA_VDIM)
            for half in range(2):
                qs = slice(half * hq, (half + 1) * hq)
                nk = hq if (masked and half == 0) else tk
                qt = qt_ref[sl, qs]
                k = k_ref[0:nk, sl]
                for comp in range(2):
                    qc = jnp.where(frow < DA_HEAD, qt, zero) if comp == 0 else jnp.where(frow < DA_HEAD, zero, qt)
                    items.append((2 * hd + comp, qs, half, nk, sl,
                                  jnp.dot(k, qc, preferred_element_type=F32)))
        for i, qs, half, nk, sl, s in items:
            if masked:
                keep = (lax.broadcasted_iota(jnp.int32, (nk, hq), 0)
                        <= lax.broadcasted_iota(jnp.int32, (nk, hq), 1) + half * hq)
                s = jnp.where(keep, s, NEG_INF)
            m_old = m_s[i, :, qs]
            m_new = jnp.maximum(m_old, jnp.max(s, axis=0, keepdims=True))
            alpha = jnp.exp2(m_old - m_new)
            pr = jnp.exp2(s - m_new)
            l_s[i, :, qs] = l_s[i, :, qs] * alpha + jnp.sum(pr, axis=0, keepdims=True)
            acc[i, :, qs] = acc[i, :, qs] * alpha + jnp.dot(vt_ref[sl, 0:nk], pr.astype(BF16),
                                                            preferred_element_type=F32)
            m_s[i, :, qs] = m_new

    @pl.when(kj < qi)
    def _():
        update(False)

    @pl.when(kj == qi)
    def _():
        update(True)
        lam = _lambda(lq1[...], lk1[...], lq2[...], lk2[...], lam_init)
        for hd in range(nh):
            i = 2 * hd
            ot = acc[i] / l_s[i] - lam * (acc[i + 1] / l_s[i + 1])
            scale = lax.rsqrt(jnp.mean(ot * ot, axis=0, keepdims=True) + DA_SUBLN_EPS)
            o_ref[:, hd * DA_VDIM:(hd + 1) * DA_VDIM] = (ot * scale * (sw_ref[...] * (1.0 - lam_init))).T


def _flash(qt, kb, vt, b, t, p, lam_init):
    tq = min(t, 512)
    nq = t // tq
    pairs = [(qi, kj) for qi in range(nq) for kj in range(qi + 1)]
    qi_tab = jnp.asarray([pr[0] for pr in pairs], jnp.int32)
    kj_tab = jnp.asarray([pr[1] for pr in pairs], jnp.int32)
    nh = DA_HEADS_PER_STEP
    wd = nh * DA_VDIM
    qblk = pl.BlockSpec((wd, tq), lambda bi, hi, s, qt_, kt_: (hi, bi * nq + qt_[s]))
    vblk = pl.BlockSpec((wd, tq), lambda bi, hi, s, qt_, kt_: (hi, bi * nq + kt_[s]))
    kblk = pl.BlockSpec((tq, wd), lambda bi, hi, s, qt_, kt_: (bi * nq + kt_[s], hi))
    oblk = pl.BlockSpec((tq, wd), lambda bi, hi, s, qt_, kt_: (bi * nq + qt_[s], hi))
    small = lambda shape: pl.BlockSpec(shape, lambda bi, hi, s, qt_, kt_: (0, 0))
    vec = lambda x: x.reshape(1, -1)
    stat = pltpu.VMEM((2 * nh, 1, tq), F32)
    acc = pltpu.VMEM((2 * nh, DA_VDIM, tq), F32)
    return pl.pallas_call(
        functools.partial(_flash_body, lam_init, nh),
        grid_spec=pltpu.PrefetchScalarGridSpec(
            num_scalar_prefetch=2, grid=(b, DA_HEADS // nh, len(pairs)),
            in_specs=[qblk, kblk, vblk] + [small((1, DA_HEAD))] * 4 + [small((DA_VDIM, 1))],
            out_specs=oblk,
            scratch_shapes=[stat, stat, acc]),
        out_shape=jax.ShapeDtypeStruct((b * t, DA_WIDTH), F32),
        compiler_params=_cparams("parallel", "parallel", "arbitrary"))(
            qi_tab, kj_tab, qt, kb, vt, vec(p['da_lq1']), vec(p['da_lk1']), vec(p['da_lq2']),
            vec(p['da_lk2']), p['da_subln_w'].reshape(DA_VDIM, 1))


def _decode_body(lam_init, npg, pt_ref, q_ref, kn_ref, vn_ref, *refs):
    ck_refs = refs[:npg]
    cv_refs = refs[npg:2 * npg]
    lq1, lk1, lq2, lk2, sw_ref, o_ref, qm_s, m_s, l_s, acc = refs[2 * npg:]
    j = pl.program_id(1)
    nj = pl.num_programs(1)
    nrow = 2 * DA_HEADS
    ncol = PAGE_SIZE * DA_HEADS
    rid = lax.broadcasted_iota(jnp.int32, (nrow, LANES), 0)

    @pl.when(j == 0)
    def _():
        comp = lax.broadcasted_iota(jnp.int32, (nrow, LANES), 1) // DA_HEAD
        qm = jnp.where(jnp.bitwise_and(rid, 1) == comp, q_ref[0] * DA_SCALE, 0.0)
        qm_s[...] = qm
        m_s[...] = jnp.sum(qm * kn_ref[0], axis=-1, keepdims=True)
        l_s[...] = jnp.ones_like(l_s)
        acc[...] = vn_ref[0]

    qm = qm_s[...]
    keep = (lax.rem(lax.broadcasted_iota(jnp.int32, (nrow, ncol), 1), DA_HEADS)
            == lax.broadcasted_iota(jnp.int32, (nrow, ncol), 0) // 2)
    scores = [jnp.where(keep, _dot_nt(qm, ck[0].reshape(ncol, DA_VDIM)), NEG_INF) for ck in ck_refs]
    m_old = m_s[...]
    m_new = m_old
    for s in scores:
        m_new = jnp.maximum(m_new, jnp.max(s, axis=-1, keepdims=True))
    alpha = jnp.exp(m_old - m_new)
    l_new = l_s[...] * alpha
    a_new = acc[...] * alpha
    for s, cv in zip(scores, cv_refs):
        pr = jnp.exp(s - m_new)
        l_new = l_new + jnp.sum(pr, axis=-1, keepdims=True)
        a_new = a_new + _dot(pr, cv[0].reshape(ncol, DA_VDIM))
    m_s[...] = m_new
    l_s[...] = l_new
    acc[...] = a_new

    @pl.when(j == nj - 1)
    def _():
        lam = _lambda(lq1[...], lk1[...], lq2[...], lk2[...], lam_init)
        d = (a_new / l_new) * jnp.where(jnp.bitwise_and(rid, 1) == 0, 1.0, -lam)
        pair = d + pltpu.roll(d, nrow - 1, axis=0)
        o_ref[0] = _rms(pair, sw_ref[...], DA_SUBLN_EPS) * (1.0 - lam_init)


def _decode(q, k_new, v_new, cache_k, cache_v, page_table, p, lam_init):
    b = q.shape[0]
    n_pages = page_table.shape[1]
    npg = DEC_PAGES if n_pages % DEC_PAGES == 0 else 1
    nrow = 2 * DA_HEADS
    rows16 = lambda x: jnp.repeat(x.reshape(b, DA_HEADS, DA_VDIM), 2, axis=1)
    row = pl.BlockSpec((1, nrow, DA_VDIM), lambda bi, j, pt: (bi, 0, 0))
    pages = [pl.BlockSpec((1, PAGE_SIZE, DA_HEADS, DA_VDIM),
                          lambda bi, j, pt, i=i: (pt[bi, j * npg + i], 0, 0, 0)) for i in range(npg)]
    vec = lambda x: x.reshape(1, -1)
    small = lambda n: pl.BlockSpec((1, n), lambda bi, j, pt: (0, 0))
    o = pl.pallas_call(
        functools.partial(_decode_body, lam_init, npg),
        grid_spec=pltpu.PrefetchScalarGridSpec(
            num_scalar_prefetch=1, grid=(b, n_pages // npg),
            in_specs=[row, row, row] + pages + pages + [small(DA_HEAD)] * 4 + [small(DA_VDIM)],
            out_specs=row,
            scratch_shapes=[pltpu.VMEM((nrow, DA_VDIM), F32), pltpu.VMEM((nrow, 1), F32),
                            pltpu.VMEM((nrow, 1), F32), pltpu.VMEM((nrow, DA_VDIM), F32)]),
        out_shape=jax.ShapeDtypeStruct((b, nrow, DA_VDIM), F32),
        compiler_params=_cparams("parallel", "arbitrary"))(
            page_table, rows16(q), rows16(k_new), rows16(v_new), *([cache_k] * npg), *([cache_v] * npg),
            vec(p['da_lq1']), vec(p['da_lk1']), vec(p['da_lq2']), vec(p['da_lk2']), vec(p['da_subln_w']))
    return o[:, ::2, :].reshape(b, DA_WIDTH)


def _cf_epilogue(ys, rows, fulls):
    y = ys[0]
    u = y[:, :CF_WIDTH] * _sigmoid(y[:, CF_WIDTH:2 * CF_WIDTH])
    return u, y[:, 2 * CF_WIDTH:]


def _cf_conv_body(tc, u_ref, up_ref, cw_ref, cb_ref, lw_ref, lb_ref, o_ref, buf, sh):
    c = pl.program_id(1)
    buf[0:CF_HIST, :] = jnp.where(c == 0, 0.0, up_ref[0])
    buf[CF_HIST:CF_HIST + tc, :] = u_ref[0]
    off = CF_HIST - (CF_KERNEL - 1)
    span = tc + CF_HIST - SUBLANES
    for r in range(1, SUBLANES):
        sh[r - 1, 0:span, :] = buf[r:r + span, :]
    rb = CF_ROWS
    for i in range(tc // rb):
        acc = cb_ref[...]
        for j in range(CF_KERNEL):
            r, base = (off + j) % SUBLANES, (off + j) // SUBLANES * SUBLANES
            rows = slice(base + i * rb, base + (i + 1) * rb)
            acc = acc + cw_ref[j:j + 1, :] * (buf[rows, :] if r == 0 else sh[r - 1, rows, :])
        mu = jnp.mean(acc, axis=-1, keepdims=True)
        xc = acc - mu
        var = jnp.mean(xc * xc, axis=-1, keepdims=True)
        o_ref[0, i * rb:(i + 1) * rb, :] = _silu(xc * lax.rsqrt(var + CF_LN_EPS) * lw_ref[...] + lb_ref[...])


def _cf_conv(u, b, t, p):
    tc = min(t, 512)
    per = tc // CF_HIST
    blk = pl.BlockSpec((1, tc, CF_WIDTH), lambda bi, ci: (bi, ci, 0))
    prev = pl.BlockSpec((1, CF_HIST, CF_WIDTH), lambda bi, ci: (bi, jnp.maximum(ci * per - 1, 0), 0))
    vec = lambda x: x.reshape(1, CF_WIDTH)
    u3 = u.reshape(b, t, CF_WIDTH)
    o = pl.pallas_call(
        functools.partial(_cf_conv_body, tc),
        grid=(b, t // tc),
        in_specs=[blk, prev, _full((CF_KERNEL, CF_WIDTH))] + [_full((1, CF_WIDTH))] * 3,
        out_specs=blk,
        out_shape=jax.ShapeDtypeStruct((b, t, CF_WIDTH), F32),
        scratch_shapes=[pltpu.VMEM((CF_HIST + tc, CF_WIDTH), F32),
                        pltpu.VMEM((SUBLANES - 1, CF_HIST + tc - SUBLANES, CF_WIDTH), F32)],
        compiler_params=_cparams("parallel", "parallel"))(
            u3, u3, p['cf_conv_w'], vec(p['cf_conv_b']), vec(p['cf_ln_w']), vec(p['cf_ln_b']))
    return o.reshape(b * t, CF_WIDTH)


def _cf_step_body(u_ref, st_ref, cw_ref, cb_ref, lw_ref, lb_ref, o_ref):
    nb = u_ref.shape[0]
    cw = cw_ref[...]
    hist = jnp.concatenate([jnp.sum(st_ref[i] * cw[:CF_KERNEL - 1, :], axis=0, keepdims=True)
                            for i in range(nb)], axis=0)
    acc = cb_ref[...] + hist + cw[CF_KERNEL - 1:CF_KERNEL, :] * u_ref[...]
    mu = jnp.mean(acc, axis=-1, keepdims=True)
    xc = acc - mu
    var = jnp.mean(xc * xc, axis=-1, keepdims=True)
    o_ref[...] = _silu(xc * lax.rsqrt(var + CF_LN_EPS) * lw_ref[...] + lb_ref[...])


def _cf_step(u, state, p):
    b = u.shape[0]
    tb = SUBLANES
    vec = lambda x: x.reshape(1, CF_WIDTH)
    return pl.pallas_call(
        _cf_step_body,
        grid=(b // tb,),
        in_specs=[pl.BlockSpec((tb, CF_WIDTH), lambda i: (i, 0)),
                  pl.BlockSpec((tb, CF_KERNEL - 1, CF_WIDTH), lambda i: (i, 0, 0)),
                  _full((CF_KERNEL, CF_WIDTH))] + [_full((1, CF_WIDTH))] * 3,
        out_specs=pl.BlockSpec((tb, CF_WIDTH), lambda i: (i, 0)),
        out_shape=jax.ShapeDtypeStruct((b, CF_WIDTH), F32),
        compiler_params=_cparams("parallel"))(
            u, state, p['cf_conv_w'], vec(p['cf_conv_b']), vec(p['cf_ln_w']), vec(p['cf_ln_b']))


def _forward(x_prompt, x_sample, state_rwkv_shift, state_rwkv_wkv, state_mamba_conv, state_mamba_ssm,
             cache_k, cache_v, page_table, state_conformer_conv, p):
    bp, tp, d = x_prompt.shape
    bs = x_sample.shape[0]
    hp = x_prompt.reshape(bp * tp, d)
    hs = x_sample.reshape(bs, d)
    npre, npost = p['norm_pre'], p['norm_post']

    (r, k, v, g, a, lw), rw_shift_p = _rwkv_proj_seq(hp, bp, tp, npre[0], p)
    o, rw_wkv_p = _rwkv_chunk(r, k, v, a, lw, bp, tp, p)
    hp = _out_proj(o, g, hp, p['rw_w_o'], npost[0])
    (r, k, v, g, a, lw), rw_shift_s = _rwkv_proj_step(hs, state_rwkv_shift, npre[0], p)
    o, rw_wkv_s = _rwkv_step(r, k, v, a, lw, state_rwkv_wkv, p)
    hs = _out_proj(o, g, hs, p['rw_w_o'], npost[0])

    mw = _mamba_weights(p)
    mv = _mamba_vecs(p)
    widths = (MB_INNER, MB_CONV_DIM, LANES)
    dtb = (mv['dt_bias'],)
    zs, xbc, dt = _in_proj(hp, npre[1], mw, _mamba_epilogue, widths, full_extras=dtb, tm=256)
    yg, mb_ssm_p = _mamba_chunk(zs, xbc, dt, bp, tp, mv)
    mb_conv_p = xbc.reshape(bp, tp, MB_CONV_DIM)[:, tp - (MB_CONV - 1):]
    hp = _out_proj(yg, None, hp, p['mb_w_out'], npost[1])
    zs, xbc, dt = _in_proj(hs, npre[1], mw, _mamba_epilogue, widths, full_extras=dtb)
    yg, mb_ssm_s = _mamba_step(zs, xbc, dt, state_mamba_conv, state_mamba_ssm, mv)
    mb_conv_s = jnp.concatenate([state_mamba_conv[:, 1:], xbc[:, None, :]], axis=1)
    hs = _out_proj(yg, None, hs, p['mb_w_out'], npost[1])

    lam_init = 0.8 - 0.6 * math.exp(-0.3 * 2)
    tabs_p = _rope_tables(jnp.arange(tp, dtype=jnp.int32))
    tm = min(_row_tile(tp), 256)
    tps = tp // tm
    rows_p = [(tb, lambda i: (i % tps, 0), LANES) for tb in tabs_p]
    widths = ((DA_WIDTH, F32, 'heads'), (DA_WIDTH, F32, 'heads'), DA_WIDTH,
              (DA_WIDTH, BF16, True), (DA_WIDTH, BF16, False), (DA_WIDTH, BF16, True))
    k, v, g, qt, kb, vt = _in_proj(hp, npre[2], [p['da_w_qkvg']], _da_epilogue_seq, widths,
                                   row_extras=rows_p, tm=tm)
    k_rows_p = k.reshape(bp, tp, DA_HEADS, 2 * DA_HEAD)
    v_rows_p = v.reshape(bp, tp, DA_HEADS, DA_VDIM)
    o = _flash(qt, kb, vt, bp, tp, p, lam_init)
    hp = _out_proj(o, g, hp, p['da_w_o'], npost[2])
    tabs_s = _rope_tables(jnp.full((bs,), PAST_LEN, dtype=jnp.int32))
    rows_s = [(tb, lambda i: (i, 0), LANES) for tb in tabs_s]
    q, k, v, g = _in_proj(hs, npre[2], [p['da_w_qkvg']], _da_epilogue, (DA_WIDTH,) * 4, row_extras=rows_s)
    k_rows_s = k.reshape(bs, 1, DA_HEADS, 2 * DA_HEAD)
    v_rows_s = v.reshape(bs, 1, DA_HEADS, DA_VDIM)
    o = _decode(q, k, v, cache_k, cache_v, page_table, p, lam_init)
    hs = _out_proj(o, g, hs, p['da_w_o'], npost[2])

    u, g = _in_proj(hp, npre[3], [p['cf_w_in']], _cf_epilogue, (CF_WIDTH,) * 2)
    cf_conv_p = u.reshape(bp, tp, CF_WIDTH)[:, tp - (CF_KERNEL - 1):]
    c = _cf_conv(u, bp, tp, p)
    hp = _out_proj(c, g, hp, p['cf_w_out'], npost[3])
    u, g = _in_proj(hs, npre[3], [p['cf_w_in']], _cf_epilogue, (CF_WIDTH,) * 2)
    cf_conv_s = jnp.concatenate([state_conformer_conv[:, 1:], u[:, None, :]], axis=1)
    c = _cf_step(u, state_conformer_conv, p)
    hs = _out_proj(c, g, hs, p['cf_w_out'], npost[3])

    return (hp.reshape(bp, tp, d), hs.reshape(bs, 1, d), rw_shift_p, rw_shift_s, rw_wkv_p, rw_wkv_s,
            mb_conv_p, mb_conv_s, mb_ssm_p, mb_ssm_s, k_rows_p, k_rows_s, v_rows_p, v_rows_s,
            cf_conv_p, cf_conv_s)


def kernel(x_prompt, x_sample, state_rwkv_shift, state_rwkv_wkv, state_mamba_conv, state_mamba_ssm, cache_k, cache_v, page_table, state_conformer_conv, norm_pre, norm_post, rw_mu, rw_w_r, rw_w_k, rw_w_v, rw_w_g, rw_w0, rw_w_w1, rw_w_w2, rw_a0, rw_w_a1, rw_w_a2, rw_k_k, rw_k_a, rw_r_k, rw_gn_w, rw_gn_b, rw_w_o, mb_w_in, mb_conv_w, mb_conv_b, mb_dt_bias, mb_a_log, mb_d, mb_norm_w, mb_w_out, da_w_qkvg, da_lq1, da_lk1, da_lq2, da_lk2, da_subln_w, da_w_o, cf_w_in, cf_conv_w, cf_conv_b, cf_ln_w, cf_ln_b, cf_w_out):
    p = dict(norm_pre=norm_pre, norm_post=norm_post, rw_mu=rw_mu, rw_w_r=rw_w_r, rw_w_k=rw_w_k, rw_w_v=rw_w_v,
             rw_w_g=rw_w_g, rw_w0=rw_w0, rw_w_w1=rw_w_w1, rw_w_w2=rw_w_w2, rw_a0=rw_a0, rw_w_a1=rw_w_a1,
             rw_w_a2=rw_w_a2, rw_k_k=rw_k_k, rw_k_a=rw_k_a, rw_r_k=rw_r_k, rw_gn_w=rw_gn_w, rw_gn_b=rw_gn_b,
             rw_w_o=rw_w_o, mb_w_in=mb_w_in, mb_conv_w=mb_conv_w, mb_conv_b=mb_conv_b, mb_dt_bias=mb_dt_bias,
             mb_a_log=mb_a_log, mb_d=mb_d, mb_norm_w=mb_norm_w, mb_w_out=mb_w_out, da_w_qkvg=da_w_qkvg,
             da_lq1=da_lq1, da_lk1=da_lk1, da_lq2=da_lq2, da_lk2=da_lk2, da_subln_w=da_subln_w, da_w_o=da_w_o,
             cf_w_in=cf_w_in, cf_conv_w=cf_conv_w, cf_conv_b=cf_conv_b, cf_ln_w=cf_ln_w, cf_ln_b=cf_ln_b,
             cf_w_out=cf_w_out)
    return _forward(x_prompt, x_sample, state_rwkv_shift, state_rwkv_wkv, state_mamba_conv, state_mamba_ssm,
                    cache_k, cache_v, page_table, state_conformer_conv, p)
```
